```python
import jax
import jax.numpy as jnp
from jax import lax
import numpy as np


D_MODEL = 2048
BATCH = 4
SEQ = 2048
DEPTH = 4

GRID_W = 64
CTX_LEN = 256
EPS = 1e-6

F_GROUPS = 4
F_GROUP_DIM = 128
F_DIM = F_GROUPS * F_GROUP_DIM
G_HEADS = 4
G_HEAD_DIM = 128
G_DIM = G_HEADS * G_HEAD_DIM
G_CHUNK = 128
ROWS_PER_CHUNK = G_CHUNK // GRID_W
L_HEADS = 4
L_DK = 128
L_DV = 256
L_KDIM = L_HEADS * L_DK
L_VDIM = L_HEADS * L_DV
L_RANK = 16
L_TAU = 16.0
L_CHUNK = 64
N_BRANCH = 3
IN_DIM = F_DIM + 2 * G_DIM + 2 * L_KDIM + 2 * L_VDIM + 2 * L_RANK + N_BRANCH * D_MODEL
SPLIT_SIZES = (F_DIM, 2 * G_DIM, L_KDIM, L_KDIM, L_VDIM, L_VDIM, 2 * L_RANK)
N_GROUPS = 4
EXPERTS_PER_GROUP = 8
N_EXPERTS = N_GROUPS * EXPERTS_PER_GROUP
EXPERT_TOPK = 2
D_EXPERT = 512

kernel_name = "hybrid_fourier_gmlp_gla_hmoe_dit"


def _rmsnorm(x, g):
    xf = x.astype(jnp.float32)
    y = xf * lax.rsqrt(jnp.mean(xf * xf, axis=-1, keepdims=True) + EPS)
    return (y * g.astype(jnp.float32)).astype(x.dtype)


def _split_proj(p):
    offs = [int(o) for o in np.cumsum(SPLIT_SIZES)]
    return jnp.split(p, offs, axis=-1)


def _fourier_mix(a):
    B, L, _ = a.shape
    af = a.astype(jnp.float32).reshape(B, L, F_GROUPS, F_GROUP_DIM)
    spec = jnp.fft.fft2(af, axes=(1, 3), norm='ortho')
    return jnp.real(spec).reshape(B, L, F_DIM).astype(a.dtype)


def _chunk_mlp(z, n_chunks, norm_g, ws, bs):
    B, L, _ = z.shape
    z = jax.nn.gelu(z)
    u, v = jnp.split(z, 2, axis=-1)
    v = _rmsnorm(v, norm_g).reshape(B, n_chunks, G_CHUNK, G_HEADS, G_HEAD_DIM)
    s = jnp.einsum('hpq,bnqhc->bnphc', ws, v) + bs.T[None, None, :, :, None]
    return u * s.reshape(B, L, G_DIM)


def _gla_prep(q, k, v, lr, w_a2, b_a):
    B, L, _ = q.shape
    f32 = jnp.float32
    q = q.astype(f32).reshape(B, L, L_HEADS, L_DK) * (L_DK ** -0.5)
    k = k.astype(f32).reshape(B, L, L_HEADS, L_DK)
    v = v.astype(f32).reshape(B, L, L_HEADS, L_DV)
    lr = lr.astype(f32).reshape(B, L, 2, L_RANK)
    logits = jnp.einsum('blnr,nrk->blnk', lr, w_a2.astype(f32)) + b_a.astype(f32)
    logdecay = (jax.nn.log_sigmoid(logits) / L_TAU).reshape(B, L, 2, L_HEADS, L_DK)
    return q, k, v, logdecay[:, :, 0], logdecay[:, :, 1]


def _gla_chunked(q, k, v, logdecay, s0, strict):
    B, L, H, DK = q.shape
    C = L_CHUNK
    N = L // C

    def split(t):
        return t.reshape(B, N, C, H, t.shape[-1]).transpose(0, 3, 1, 2, 4)

    q, k, v, g = split(q), split(k), split(v), split(logdecay)
    b = jnp.cumsum(g, axis=3)
    b_last = b[:, :, :, -1:, :]
    q_dec = q * jnp.exp(b)
    k_inv = k * jnp.exp(-b)
    k_tail = k * jnp.exp(b_last - b)
    scores = jnp.einsum('bhnid,bhnjd->bhnij', q_dec, k_inv)
    mask = jnp.tril(jnp.ones((C, C), dtype=bool), k=-1 if strict else 0)
    scores = jnp.where(mask, scores, 0.0)
    o_intra = jnp.einsum('bhnij,bhnje->bhnie', scores, v)
    ds = jnp.einsum('bhnjd,bhnje->nbhde', k_tail, v)
    gam = jnp.exp(b_last[:, :, :, 0, :]).transpose(2, 0, 1, 3)

    def step(s, inp):
        gam_n, ds_n = inp
        return gam_n[..., None] * s + ds_n, s

    s_final, s_starts = lax.scan(step, s0, (gam, ds))
    o_inter = jnp.einsum('bhnid,nbhde->bhnie', q_dec, s_starts)
    o = (o_intra + o_inter).transpose(0, 2, 3, 1, 4).reshape(B, L, H, v.shape[-1])
    return o, s_final


def _gla_bidir(q, k, v, g_f, g_b, s0_f, s0_b):
    o_f, s_f = _gla_chunked(q, k, v, g_f, s0_f, False)
    flip = lambda t: jnp.flip(t, axis=1)
    o_b, s_b = _gla_chunked(flip(q), flip(k), flip(v), flip(g_b), s0_b, True)
    return o_f + flip(o_b), s_f, s_b


def _merge_branches(a, z, o, r, gates, n_chunks, gmlp_norm_g, gmlp_ws, gmlp_bs, gla_norm_g,
                    p_fourier, p_gmlp, p_gla, w_out):
    B, L, _ = a.shape
    y_f = _fourier_mix(a) @ p_fourier
    y_g = _chunk_mlp(z, n_chunks, gmlp_norm_g, gmlp_ws, gmlp_bs) @ p_gmlp
    o = _rmsnorm(o, gla_norm_g.reshape(L_HEADS, L_DV)).reshape(B, L, L_VDIM)
    y_l = (o * jax.nn.silu(r)) @ p_gla
    gw = jax.nn.sigmoid(gates.reshape(B, L, N_BRANCH, D_MODEL))
    y = gw[:, :, 0] * y_f + gw[:, :, 1] * y_g + gw[:, :, 2] * y_l
    return y @ w_out


def _hier_moe(x, w_rg, b_rg, w_re, b_re, w_gate, w_up, w_down):
    B, L, D = x.shape
    t = x.reshape(B * L, D)
    g_logits = (t @ w_rg + b_rg).astype(jnp.float32)
    g_idx = jnp.argmax(g_logits, axis=-1)
    g_w = jnp.take_along_axis(jax.nn.softmax(g_logits, axis=-1), g_idx[:, None], axis=-1)
    e_logits = (t @ w_re + b_re).astype(jnp.float32).reshape(-1, N_GROUPS, EXPERTS_PER_GROUP)
    e_in_group = jnp.take_along_axis(e_logits, g_idx[:, None, None], axis=1)[:, 0]
    top_v, top_i = lax.top_k(e_in_group, EXPERT_TOPK)
    top_w = jax.nn.softmax(top_v, axis=-1) * g_w
    expert_id = g_idx[:, None] * EXPERTS_PER_GROUP + top_i
    combine = jnp.sum(jax.nn.one_hot(expert_id, N_EXPERTS, dtype=jnp.float32) * top_w[..., None],
                      axis=1).astype(x.dtype)
    out = jnp.zeros_like(t)
    for grp in range(N_GROUPS):
        sl = slice(grp * EXPERTS_PER_GROUP, (grp + 1) * EXPERTS_PER_GROUP)
        hg = jnp.einsum('td,edf->tef', t, w_gate[sl])
        hu = jnp.einsum('td,edf->tef', t, w_up[sl])
        act = jax.nn.silu(hg) * hu * combine[:, sl, None]
        out = out + jnp.einsum('tef,efd->td', act, w_down[sl])
    return out.reshape(B, L, D)


def setup_inputs(seed: int = 0) -> dict:
    key = jax.random.key(seed)
    ks = jax.random.split(key, 32)
    f32 = jnp.float32

    def nrm(k, shape, scale):
        return jax.random.normal(k, shape, f32) * scale

    D = D_MODEL
    return {
        'x': nrm(ks[0], (BATCH, SEQ, D), 1.0),
        'c': nrm(ks[1], (BATCH, D), 1.0),
        'ctx': nrm(ks[2], (BATCH, CTX_LEN, D), 1.0),
        'c_ctx': nrm(ks[3], (D,), 1.0),
        'w_mod': nrm(ks[4], (DEPTH, D, 6 * D), 0.5 * D ** -0.5),
        'b_mod': nrm(ks[5], (DEPTH, 6 * D), 0.02),
        'norm1_g': 1.0 + nrm(ks[6], (DEPTH, D), 0.02),
        'norm2_g': 1.0 + nrm(ks[7], (DEPTH, D), 0.02),
        'w_in': nrm(ks[8], (DEPTH, D, IN_DIM), D ** -0.5),
        'p_fourier': nrm(ks[9], (DEPTH, F_DIM, D), F_DIM ** -0.5),
        'gmlp_norm_g': 1.0 + nrm(ks[10], (DEPTH, G_DIM), 0.02),
        'gmlp_ws': nrm(ks[11], (DEPTH, G_HEADS, G_CHUNK, G_CHUNK), G_CHUNK ** -0.5),
        'gmlp_bs': 1.0 + nrm(ks[12], (DEPTH, G_HEADS, G_CHUNK), 0.02),
        'p_gmlp': nrm(ks[13], (DEPTH, G_DIM, D), G_DIM ** -0.5),
        'gla_w_a2': nrm(ks[14], (DEPTH, 2, L_RANK, L_KDIM), L_RANK ** -0.5),
        'gla_b_a': nrm(ks[15], (DEPTH, 2, L_KDIM), 0.1),
        'gla_norm_g': 1.0 + nrm(ks[16], (DEPTH, L_VDIM), 0.02),
        'p_gla': nrm(ks[17], (DEPTH, L_VDIM, D), L_VDIM ** -0.5),
        'w_out': nrm(ks[18], (DEPTH, D, D), D ** -0.5),
        'router_group_w': nrm(ks[19], (DEPTH, D, N_GROUPS), D ** -0.5),
        'router_group_b': nrm(ks[20], (DEPTH, N_GROUPS), 0.01),
        'router_expert_w': nrm(ks[21], (DEPTH, D, N_EXPERTS), D ** -0.5),
        'router_expert_b': nrm(ks[22], (DEPTH, N_EXPERTS), 0.01),
        'expert_w_gate': nrm(ks[23], (DEPTH, N_EXPERTS, D, D_EXPERT), D ** -0.5),
        'expert_w_up': nrm(ks[24], (DEPTH, N_EXPERTS, D, D_EXPERT), D ** -0.5),
        'expert_w_down': nrm(ks[25], (DEPTH, N_EXPERTS, D_EXPERT, D), D_EXPERT ** -0.5),
        'final_norm_g': 1.0 + nrm(ks[26], (D,), 0.02),
    }


def reference(x, c, ctx, c_ctx, w_mod, b_mod, norm1_g, norm2_g, w_in, p_fourier, gmlp_norm_g,
              gmlp_ws, gmlp_bs, p_gmlp, gla_w_a2, gla_b_a, gla_norm_g, p_gla, w_out,
              router_group_w, router_group_b, router_expert_w, router_expert_b,
              expert_w_gate, expert_w_up, expert_w_down, final_norm_g):
    B, n_lat, _ = x.shape
    rows = n_lat // GRID_W
    lat_chunks = rows // ROWS_PER_CHUNK
    ctx_chunks = ctx.shape[1] // G_CHUNK
    h, hc = x, ctx
    for l in range(DEPTH):
        last = l == DEPTH - 1
        mod = (jax.nn.silu(c) @ w_mod[l] + b_mod[l])[:, None, :]
        mod_c = (jax.nn.silu(c_ctx) @ w_mod[l] + b_mod[l])[None, None, :]
        sh1, sc1, gt1, sh2, sc2, gt2 = jnp.split(mod, 6, axis=-1)
        csh1, csc1, cgt1, csh2, csc2, cgt2 = jnp.split(mod_c, 6, axis=-1)

        pl = (_rmsnorm(h, norm1_g[l]) * (1.0 + sc1) + sh1) @ w_in[l]
        pc = (_rmsnorm(hc, norm1_g[l]) * (1.0 + csc1) + csh1) @ w_in[l]
        a, z, q, k, v, r, lr, gates = _split_proj(pl)
        ac, zc, qc, kc, vc, rc, lrc, gatesc = _split_proj(pc)

        qc_, kc_, vc_, gfc, gbc = _gla_prep(qc, kc, vc, lrc, gla_w_a2[l], gla_b_a[l])
        s_zero = jnp.zeros((B, L_HEADS, L_DK, L_DV), jnp.float32)
        oc, sfc, sbc = _gla_bidir(qc_, kc_, vc_, gfc, gbc, s_zero, s_zero)
        q_, k_, v_, gf, gb = _gla_prep(q, k, v, lr, gla_w_a2[l], gla_b_a[l])
        o, _, _ = _gla_bidir(q_, k_, v_, gf, gb, sfc, sbc)

        h = h + gt1 * _merge_branches(a, z, o.astype(h.dtype), r, gates, lat_chunks,
                                      gmlp_norm_g[l], gmlp_ws[l], gmlp_bs[l], gla_norm_g[l],
                                      p_fourier[l], p_gmlp[l], p_gla[l], w_out[l])
        if not last:
            hc = hc + cgt1 * _merge_branches(ac, zc, oc.astype(hc.dtype), rc, gatesc, ctx_chunks,
                                             gmlp_norm_g[l], gmlp_ws[l], gmlp_bs[l], gla_norm_g[l],
                                             p_fourier[l], p_gmlp[l], p_gla[l], w_out[l])

        h = h + gt2 * _hier_moe(_rmsnorm(h, norm2_g[l]) * (1.0 + sc2) + sh2,
                                router_group_w[l], router_group_b[l], router_expert_w[l],
                                router_expert_b[l], expert_w_gate[l], expert_w_up[l], expert_w_down[l])
        if not last:
            hc = hc + cgt2 * _hier_moe(_rmsnorm(hc, norm2_g[l]) * (1.0 + csc2) + csh2,
                                       router_group_w[l], router_group_b[l], router_expert_w[l],
                                       router_expert_b[l], expert_w_gate[l], expert_w_up[l],
                                       expert_w_down[l])
    return _rmsnorm(h, final_norm_g)
```

```python
import functools
import math

import numpy as np
import jax
import jax.numpy as jnp
from jax import lax
from jax.experimental import pallas as pl
from jax.experimental.pallas import tpu as pltpu

F32 = jnp.float32
BF16 = jnp.bfloat16
I32 = jnp.int32

D = 2048
NB = 4
SEQ = 2048
NLAYER = 4
CTX = 256
EPS = 1e-6
LB = CTX + SEQ
P = NB * LB
TM = 256
TPB = LB // TM
NT = P // TM

FG, FGD = 4, 128
FDIM = FG * FGD
GH, GHD, GCH = 4, 128, 128
GDIM = GH * GHD
LH, LDK, LDV, LRANK, LTAU, LC = 4, 128, 256, 16, 16.0, 64
LKD, LVD = LH * LDK, LH * LDV
NCH = LB // LC
NCTXCH = CTX // LC

C_G0, C_V, C_R, C_ZU, C_ZV, C_A, C_Q, C_K = 0, 6144, 7168, 8192, 8704, 9216, 9728, 10240
NMAIN = 10752
LRW = 128

NGRP, EPG, NEXP, DEXP = 4, 8, 32, 512
TR = 256
NTILE = (2 * P + NEXP * (TR - 1)) // TR + 1
SMAX = NTILE * TR
RW = 128

VMEM_LIMIT = 56 * 1024 * 1024


def _cp(sem, vmem=VMEM_LIMIT):
    return pltpu.CompilerParams(dimension_semantics=sem, vmem_limit_bytes=vmem)


def _dot(a, b):
    return jnp.dot(a, b, preferred_element_type=F32)


def _dot_t(a, b):
    return lax.dot_general(a, b, (((1,), (1,)), ((), ())), preferred_element_type=F32)


def _dot_lt(a, b):
    return lax.dot_general(a, b, (((0,), (0,)), ((), ())), preferred_element_type=F32)


def _split(x):
    hi = x.astype(BF16)
    lo = (x - hi.astype(F32)).astype(BF16)
    return hi, lo


def _sigmoid(x):
    return 1.0 / (1.0 + jnp.exp(-x))


def _silu(x):
    return x * _sigmoid(x)


def _gelu_tanh(x):
    return 0.5 * x * (1.0 + jnp.tanh(math.sqrt(2.0 / math.pi) * (x + 0.044715 * (x * x * x))))


def _rms(x, g):
    return x * lax.rsqrt(jnp.mean(x * x, axis=-1, keepdims=True) + EPS) * g


def _tile_row(t):
    return jnp.where(t % TPB == 0, 4, t // TPB)


def _mod_spec(layer, comp, row_fn):
    return pl.BlockSpec((1, 1, D), lambda *g: ((layer * 8 + row_fn(*g)) * 6 + comp, 0, 0))


def _mods_kernel(c_ref, w_ref, b_ref, o_ref):
    c = c_ref[...]
    s = _silu(c).astype(BF16)
    o_ref[0] = _dot(s, w_ref[0].astype(BF16)) + b_ref[0]


def _mods(cvec, w_mod, b_mod):
    tn = 1024
    return pl.pallas_call(
        _mods_kernel,
        grid=(NLAYER, 6 * D // tn),
        in_specs=[pl.BlockSpec((8, D), lambda l, j: (0, 0)),
                  pl.BlockSpec((1, D, tn), lambda l, j: (l, 0, j)),
                  pl.BlockSpec((1, 1, tn), lambda l, j: (l, 0, j))],
        out_specs=pl.BlockSpec((1, 8, tn), lambda l, j: (l, 0, j)),
        out_shape=jax.ShapeDtypeStruct((NLAYER, 8, 6 * D), F32),
        compiler_params=_cp(("arbitrary", "arbitrary")),
        name="mods",
    )(cvec, w_mod, b_mod.reshape(NLAYER, 1, 6 * D))


def _init_kernel(x_ref, c_ref, g_ref, sh_ref, sc_ref, h_ref, xn_ref):
    j = pl.program_id(1)

    def emit(v):
        h_ref[...] = v
        xn_ref[...] = (_rms(v, g_ref[0]) * (1.0 + sc_ref[0]) + sh_ref[0]).astype(BF16)

    @pl.when(j == 0)
    def _():
        emit(c_ref[0])

    @pl.when(j > 0)
    def _():
        emit(x_ref[0])


def _init(x, ctx, norm1_g3, mods3):
    row = lambda b, j: jnp.where(j == 0, 4, b)
    return pl.pallas_call(
        _init_kernel,
        grid=(NB, TPB),
        in_specs=[pl.BlockSpec((1, TM, D), lambda b, j: (b, jnp.maximum(j - 1, 0), 0)),
                  pl.BlockSpec((1, CTX, D), lambda b, j: (b, 0, 0)),
                  pl.BlockSpec((1, 1, D), lambda b, j: (0, 0, 0)),
                  _mod_spec(0, 0, row), _mod_spec(0, 1, row)],
        out_specs=[pl.BlockSpec((TM, D), lambda b, j: (b * TPB + j, 0)),
                   pl.BlockSpec((TM, D), lambda b, j: (b * TPB + j, 0))],
        out_shape=[jax.ShapeDtypeStruct((P, D), F32), jax.ShapeDtypeStruct((P, D), BF16)],
        compiler_params=_cp(("arbitrary", "arbitrary")),
        name="init",
    )(x, ctx, norm1_g3, mods3, mods3)


def _mm_kernel(x_ref, w_ref, o_ref):
    o_ref[...] = _dot(x_ref[...], w_ref[0]).astype(o_ref.dtype)


def _matmul(x, w_all, layer, tm, tn, out_dtype):
    m, k = x.shape
    n = w_all.shape[2]
    return pl.pallas_call(
        _mm_kernel,
        grid=(n // tn, m // tm),
        in_specs=[pl.BlockSpec((tm, k), lambda j, i: (i, 0)),
                  pl.BlockSpec((1, k, tn), lambda j, i: (layer, 0, j))],
        out_specs=pl.BlockSpec((tm, tn), lambda j, i: (i, j)),
        out_shape=jax.ShapeDtypeStruct((m, n), out_dtype),
        compiler_params=_cp(("arbitrary", "arbitrary")),
        name="inproj",
    )(x, w_all)


def _dft_consts():
    def cs(n):
        k = np.arange(n, dtype=np.int64)
        ang = 2.0 * np.pi * ((k[:, None] * k[None, :]) % n).astype(np.float64) / n
        return np.cos(ang) / math.sqrt(n), np.sin(ang) / math.sqrt(n)

    c_l, s_l = cs(SEQ)
    c_c, s_c = cs(CTX)
    c_g, s_g = cs(FGD)
    cs_lat = np.concatenate([c_l, -s_l], axis=1).astype(BF16)
    cs_ctx = np.concatenate([c_c, -s_c], axis=1).astype(BF16)
    cc = np.concatenate([c_g, s_g], axis=1).astype(BF16)
    return cs_lat, cs_ctx, cc


def _fourier_kernel(a_ref, csl_ref, csc_ref, cc_ref, o_ref, rl_ref, rc_ref):
    j = pl.program_id(1)

    @pl.when(j == 0)
    def _():
        for g in range(FG):
            cols = slice(g * FGD, (g + 1) * FGD)
            t = _dot(a_ref[:, cols], cc_ref[...]).astype(BF16)
            rc_ref[0:CTX, cols] = t[0:CTX, 0:FGD]
            rc_ref[CTX:2 * CTX, cols] = t[0:CTX, FGD:2 * FGD]
            rl_ref[0:SEQ, cols] = t[CTX:LB, 0:FGD]
            rl_ref[SEQ:2 * SEQ, cols] = t[CTX:LB, FGD:2 * FGD]
        o_ref[...] = _dot(csc_ref[...], rc_ref[...]).astype(BF16)

    @pl.when(j > 0)
    def _():
        o_ref[...] = _dot(csl_ref[...], rl_ref[...]).astype(BF16)


def _fourier(pm, cs_lat, cs_ctx, cc):
    return pl.pallas_call(
        _fourier_kernel,
        grid=(NB, TPB),
        in_specs=[pl.BlockSpec((LB, FDIM), lambda b, j: (b, C_A // FDIM)),
                  pl.BlockSpec((TM, 2 * SEQ), lambda b, j: (jnp.maximum(j - 1, 0), 0)),
                  pl.BlockSpec((CTX, 2 * CTX), lambda b, j: (0, 0)),
                  pl.BlockSpec((FGD, 2 * FGD), lambda b, j: (0, 0))],
        out_specs=pl.BlockSpec((TM, FDIM), lambda b, j: (b * TPB + j, 0)),
        out_shape=jax.ShapeDtypeStruct((P, FDIM), BF16),
        scratch_shapes=[pltpu.VMEM((2 * SEQ, FDIM), BF16), pltpu.VMEM((2 * CTX, FDIM), BF16)],
        compiler_params=_cp(("arbitrary", "arbitrary")),
        name="fourier",
    )(pm, cs_lat, cs_ctx, cc)


def _gla_kernel(q_ref, k_ref, v_ref, lr_ref, wa_ref, ba_ref, o_ref,
                qd_ref, oacc_ref, ds_ref, gam_ref, s_ref):
    ri = lax.broadcasted_iota(I32, (LC, LC), 0)
    ci = lax.broadcasted_iota(I32, (LC, LC), 1)
    scale = LDK ** -0.5

    for direction in range(2):
        fwd = direction == 0
        tri = (ci <= ri) if fwd else (ci >= ri)
        keep = (ci <= ri) if fwd else (ci > ri)
        tri_b = tri.astype(BF16)
        wa_hi, wa_lo = _split(wa_ref[direction])
        ba = ba_ref[direction]

        def phase1(n, carry):
            rows = pl.ds(pl.multiple_of(n * LC, LC), LC)
            lr_hi, lr_lo = _split(lr_ref[rows, :])
            logits = _dot(lr_hi, wa_hi) + _dot(lr_hi, wa_lo) + _dot(lr_lo, wa_hi) + ba
            g = (jnp.minimum(logits, 0.0) - jnp.log1p(jnp.exp(-jnp.abs(logits)))) * (1.0 / LTAU)
            g_hi, g_lo = _split(g)
            bc = _dot(tri_b, g_hi) + _dot(tri_b, g_lo)
            b_last = bc[LC - 1:LC, :] if fwd else bc[0:1, :]
            q = q_ref[rows, :].astype(F32) * scale
            k = k_ref[rows, :].astype(F32)
            v = v_ref[rows, :]
            q_dec = (q * jnp.exp(bc)).astype(BF16)
            k_inv = (k * jnp.exp(-bc)).astype(BF16)
            k_tail = (k * jnp.exp(b_last - bc)).astype(BF16)
            sc = jnp.where(keep, _dot_t(q_dec, k_inv), 0.0).astype(BF16)
            o_intra = _dot(sc, v)
            qd_ref[rows, :] = q_dec
            if fwd:
                oacc_ref[rows, :] = o_intra
            else:
                oacc_ref[rows, :] += o_intra
            ds_ref[n] = _dot_lt(v, k_tail)
            gam_ref[n] = jnp.exp(b_last)
            return carry

        lax.fori_loop(0, NCH, phase1, 0)

        s_ref[...] = jnp.zeros_like(s_ref)

        def phase2(i, carry):
            if fwd:
                n = i
            else:
                n = jnp.where(i < NCTXCH, NCTXCH - 1 - i, NCH + NCTXCH - 1 - i)
            rows = pl.ds(pl.multiple_of(n * LC, LC), LC)
            s = s_ref[...]
            oacc_ref[rows, :] += _dot_t(qd_ref[rows, :], s.astype(BF16))
            s_ref[...] = s * gam_ref[n] + ds_ref[n]
            return carry

        lax.fori_loop(0, NCH, phase2, 0)

    o_ref[...] = oacc_ref[...].astype(BF16)


def _gla(pm, lr, wa_pad, ba3, layer):
    return pl.pallas_call(
        _gla_kernel,
        grid=(NB, LH),
        in_specs=[pl.BlockSpec((LB, LDK), lambda b, h: (b, C_Q // LDK + h)),
                  pl.BlockSpec((LB, LDK), lambda b, h: (b, C_K // LDK + h)),
                  pl.BlockSpec((LB, LDV), lambda b, h: (b, C_V // LDV + h)),
                  pl.BlockSpec((LB, LRW), lambda b, h: (b, 0)),
                  pl.BlockSpec((None, 2, LRW, LDK), lambda b, h: (layer, 0, 0, h)),
                  pl.BlockSpec((None, 2, 1, LDK), lambda b, h: (layer, 0, 0, h))],
        out_specs=pl.BlockSpec((LB, LDV), lambda b, h: (b, h)),
        out_shape=jax.ShapeDtypeStruct((P, LVD), BF16),
        scratch_shapes=[pltpu.VMEM((LB, LDK), BF16), pltpu.VMEM((LB, LDV), F32),
                        pltpu.VMEM((NCH, LDV, LDK), F32), pltpu.VMEM((NCH, 1, LDK), F32),
                        pltpu.VMEM((LDV, LDK), F32)],
        compiler_params=_cp(("arbitrary", "arbitrary")),
        name="gla",
    )(pm, pm, pm, lr, wa_pad, ba3)


def _merge_kernel(yf_ref, zu_ref, zv_ref, o_ref, r_ref, g0_ref, g1_ref, g2_ref, h_ref,
                  gt1_ref, sh2_ref, sc2_ref, n2g_ref, gng_ref, ws_ref, bs_ref, lng_ref,
                  pf_ref, pg_ref, pl_ref, wo_ref, wrh_ref, wrl_ref, br_ref,
                  hmid_ref, xm_ref, lg_ref):
    y = _sigmoid(g0_ref[...].astype(F32)) * _dot(yf_ref[...], pf_ref[...])

    u = _gelu_tanh(zu_ref[...].astype(F32))
    v = _rms(_gelu_tanh(zv_ref[...].astype(F32)), gng_ref[...]).astype(BF16)
    chunks = []
    for ch in range(TM // GCH):
        rows = slice(ch * GCH, (ch + 1) * GCH)
        heads = [_dot(ws_ref[hd], v[rows, hd * GHD:(hd + 1) * GHD]) + bs_ref[hd] for hd in range(GH)]
        chunks.append(jnp.concatenate(heads, axis=1))
    s = jnp.concatenate(chunks, axis=0)
    y += _sigmoid(g1_ref[...].astype(F32)) * _dot((u * s).astype(BF16), pg_ref[...])

    o = o_ref[...].astype(F32)
    lng = lng_ref[...]
    heads = [_rms(o[:, hd * LDV:(hd + 1) * LDV], lng[:, hd * LDV:(hd + 1) * LDV]) for hd in range(LH)]
    ol = (jnp.concatenate(heads, axis=1) * _silu(r_ref[...].astype(F32))).astype(BF16)
    y += _sigmoid(g2_ref[...].astype(F32)) * _dot(ol, pl_ref[...])

    hmid = h_ref[...] + gt1_ref[0] * _dot(y.astype(BF16), wo_ref[...])
    hmid_ref[...] = hmid

    xm = _rms(hmid, n2g_ref[0]) * (1.0 + sc2_ref[0]) + sh2_ref[0]
    xm_ref[...] = xm
    xm_hi, xm_lo = _split(xm)
    lg_ref[...] = (_dot(xm_hi, wrh_ref[...]) + _dot(xm_hi, wrl_ref[...]) + _dot(xm_lo, wrh_ref[...])
                   + br_ref[...])


def _merge(layer, yf, pm, o, h, mods3, norm2_g3, gng, ws, bsb, lng, pf, pg, plw, wo, wrh, wrl, br):
    row = _tile_row
    tile = lambda w, c: pl.BlockSpec((TM, w), lambda t: (t, c))
    lay3 = lambda a, b: pl.BlockSpec((None, a, b), lambda t: (layer, 0, 0), pipeline_mode=pl.Buffered(1))
    lay4 = lambda a, b, c: pl.BlockSpec((None, a, b, c), lambda t: (layer, 0, 0, 0))
    return pl.pallas_call(
        _merge_kernel,
        grid=(NT,),
        in_specs=[tile(FDIM, 0), tile(GDIM, C_ZU // GDIM), tile(GDIM, C_ZV // GDIM), tile(LVD, 0),
                  tile(LVD, C_R // LVD), tile(D, 0), tile(D, 1), tile(D, 2), tile(D, 0),
                  _mod_spec(layer, 2, row), _mod_spec(layer, 3, row), _mod_spec(layer, 4, row),
                  pl.BlockSpec((1, 1, D), lambda t: (layer, 0, 0)),
                  lay3(1, GDIM), lay4(GH, GCH, GCH), lay4(GH, GCH, GCH), lay3(1, LVD),
                  lay3(FDIM, D), lay3(GDIM, D), lay3(LVD, D), lay3(D, D),
                  lay3(D, RW), lay3(D, RW), lay3(1, RW)],
        out_specs=[tile(D, 0), tile(D, 0), tile(RW, 0)],
        out_shape=[jax.ShapeDtypeStruct((P, D), F32), jax.ShapeDtypeStruct((P, D), F32),
                   jax.ShapeDtypeStruct((P, RW), F32)],
        compiler_params=_cp(("arbitrary",)),
        name="merge",
    )(yf, pm, pm, o, pm, pm, pm, pm, h, mods3, mods3, mods3, norm2_g3, gng, ws, bsb, lng,
      pf, pg, plw, wo, wrh, wrl, br)


def _route_kernel(lg_ref, meta_ref, wts_ref, cnt_ref, carry_ref):
    t = pl.program_id(0)

    @pl.when(t == 0)
    def _():
        carry_ref[...] = jnp.zeros_like(carry_ref)

    lg = lg_ref[...]
    lane = lax.broadcasted_iota(I32, (TM, RW), 1)
    lane_f = lane.astype(F32)
    ninf = jnp.float32(-jnp.inf)

    def first_max(x):
        m = jnp.max(x, axis=-1, keepdims=True)
        first = jnp.min(jnp.where(x == m, lane_f, float(RW)), axis=-1, keepdims=True)
        return m, first.astype(I32)

    is_g = (lane >= NEXP) & (lane < NEXP + NGRP)
    gmax, glane = first_max(jnp.where(is_g, lg, ninf))
    gsum = jnp.sum(jnp.where(is_g, jnp.exp(lg - gmax), 0.0), axis=-1, keepdims=True)
    g_w = 1.0 / gsum
    lo = (glane - NEXP) * EPG
    in_grp = (lane >= lo) & (lane < lo + EPG)
    el = jnp.where(in_grp, lg, ninf)
    v1, l1 = first_max(el)
    v2, l2 = first_max(jnp.where(lane == l1, ninf, el))
    e = jnp.exp(v2 - v1)
    w1 = g_w / (1.0 + e)
    w2 = g_w * e / (1.0 + e)

    hit1 = lane == l1
    hit2 = lane == l2
    m = (hit1 | hit2).astype(BF16)
    ri = lax.broadcasted_iota(I32, (TM, TM), 0)
    ci = lax.broadcasted_iota(I32, (TM, TM), 1)
    before = _dot((ci < ri).astype(BF16), m) + carry_ref[0:1, :]
    r1 = jnp.sum(jnp.where(hit1, before, 0.0), axis=-1, keepdims=True).astype(I32)
    r2 = jnp.sum(jnp.where(hit2, before, 0.0), axis=-1, keepdims=True).astype(I32)
    total = carry_ref[0:1, :] + jnp.sum(m.astype(F32), axis=0, keepdims=True)
    carry_ref[...] = jnp.broadcast_to(total, carry_ref.shape)
    cnt_ref[...] = jnp.broadcast_to(total, cnt_ref.shape).astype(I32)

    pk1 = l1 * 65536 + r1
    pk2 = l2 * 65536 + r2
    meta_ref[...] = jnp.where(lane == 0, pk1, jnp.where(lane == 1, pk2, 0))
    wts_ref[...] = jnp.where(lane == 0, w1, jnp.where(lane == 1, w2, 0.0))


def _route(logits):
    return pl.pallas_call(
        _route_kernel,
        grid=(NT,),
        in_specs=[pl.BlockSpec((TM, RW), lambda t: (t, 0))],
        out_specs=[pl.BlockSpec((TM, RW), lambda t: (t, 0)), pl.BlockSpec((TM, RW), lambda t: (t, 0)),
                   pl.BlockSpec((8, RW), lambda t: (0, 0))],
        out_shape=[jax.ShapeDtypeStruct((P, RW), I32), jax.ShapeDtypeStruct((P, RW), F32),
                   jax.ShapeDtypeStruct((8, RW), I32)],
        scratch_shapes=[pltpu.VMEM((8, RW), F32)],
        compiler_params=_cp(("arbitrary",)),
        name="route",
    )(logits)


def _invert_kernel(pk_ref, tstart_ref, inv_ref):
    def fill(i, c):
        inv_ref[i] = jnp.int32(-1)
        return c

    lax.fori_loop(0, SMAX, fill, 0, unroll=8)

    def body(i, c):
        pk = pk_ref[i]
        e = lax.shift_right_logical(pk, 16)
        rank = pk & 0xFFFF
        slot = tstart_ref[e] * TR + rank
        inv_ref[slot] = (i & 1) * P + lax.shift_right_logical(i, 1)
        return c

    lax.fori_loop(0, 2 * P, body, 0, unroll=8)


def _invert(pk, tstart):
    smem = pl.BlockSpec(memory_space=pltpu.SMEM)
    return pl.pallas_call(
        _invert_kernel,
        in_specs=[smem, smem],
        out_specs=smem,
        out_shape=jax.ShapeDtypeStruct((SMAX,), I32),
        name="invert",
    )(pk, tstart)


def _experts_kernel(te_ref, na_ref, inv_ref, invn_ref, x_hbm, wg_ref, wu_ref, wd_ref, y_hbm,
                    xbuf, ybuf, wgb, wub, wdb, gsem, ssem):
    i = pl.program_id(0)
    n_act = na_ref[0]
    slot = i % 2

    def gather(idx_ref, buf_slot):
        def body(r, c):
            code = idx_ref[0, 0, r]
            tok = jnp.where(code < 0, 0, jnp.where(code >= P, code - P, code))
            pltpu.make_async_copy(x_hbm.at[pl.ds(tok, 1)], xbuf.at[buf_slot, pl.ds(r, 1)],
                                  gsem.at[buf_slot]).start()
            return c
        lax.fori_loop(0, TR, body, 0, unroll=8)

    @pl.when(i == 0)
    def _():
        ybuf[...] = jnp.zeros_like(ybuf)
        spare = pltpu.make_async_copy(ybuf, y_hbm.at[pl.ds(2 * P, TR)], ssem.at[0])
        spare.start()
        spare.wait()
        gather(inv_ref, 0)

    @pl.when(i + 1 < n_act)
    def _():
        gather(invn_ref, 1 - slot)

    @pl.when(i < n_act)
    def _():
        first = jnp.logical_or(i == 0, te_ref[i] != te_ref[jnp.maximum(i - 1, 0)])

        @pl.when(first)
        def _():
            wgb[...] = wg_ref[0].astype(BF16)
            wub[...] = wu_ref[0].astype(BF16)
            wdb[...] = wd_ref[0].astype(BF16)

        pltpu.make_async_copy(xbuf.at[slot], xbuf.at[slot], gsem.at[slot]).wait()
        x = xbuf[slot].astype(BF16)
        hg = _dot(x, wgb[...])
        hu = _dot(x, wub[...])
        act = (_silu(hg) * hu).astype(BF16)
        y = _dot(act, wdb[...])

        @pl.when(i > 0)
        def _():
            pltpu.make_async_copy(ybuf, ybuf, ssem.at[0]).wait()

        ybuf[...] = y

        def scatter(r, c):
            code = inv_ref[0, 0, r]
            dst = jnp.where(code < 0, 2 * P + r, code)
            pltpu.make_async_copy(ybuf.at[pl.ds(r, 1)], y_hbm.at[pl.ds(dst, 1)], ssem.at[0]).start()
            return c
        lax.fori_loop(0, TR, scatter, 0, unroll=8)

        @pl.when(i == n_act - 1)
        def _():
            pltpu.make_async_copy(ybuf, ybuf, ssem.at[0]).wait()


def _experts(tile_expert, n_act, inv3, xm, wg, wu, wd, layer):
    wspec = lambda a, b: pl.BlockSpec((None, 1, a, b), lambda i, te, na: (layer, te[i], 0, 0))
    grid_spec = pltpu.PrefetchScalarGridSpec(
        num_scalar_prefetch=2,
        grid=(NTILE,),
        in_specs=[pl.BlockSpec((1, 1, TR), lambda i, te, na: (i, 0, 0), memory_space=pltpu.SMEM),
                  pl.BlockSpec((1, 1, TR), lambda i, te, na: (jnp.minimum(i + 1, NTILE - 1), 0, 0),
                               memory_space=pltpu.SMEM),
                  pl.BlockSpec(memory_space=pl.ANY),
                  wspec(D, DEXP), wspec(D, DEXP), wspec(DEXP, D)],
        out_specs=pl.BlockSpec(memory_space=pl.ANY),
        scratch_shapes=[pltpu.VMEM((2, TR, D), F32), pltpu.VMEM((TR, D), F32),
                        pltpu.VMEM((D, DEXP), BF16), pltpu.VMEM((D, DEXP), BF16),
                        pltpu.VMEM((DEXP, D), BF16),
                        pltpu.SemaphoreType.DMA((2,)), pltpu.SemaphoreType.DMA((1,))],
    )
    return pl.pallas_call(
        _experts_kernel,
        grid_spec=grid_spec,
        out_shape=jax.ShapeDtypeStruct((2 * P + TR, D), F32),
        compiler_params=_cp(("arbitrary",)),
        name="experts",
    )(tile_expert, n_act, inv3, inv3, xm, wg, wu, wd)


def _combine_kernel(y0_ref, y1_ref, w_ref, h_ref, gt2_ref, g_ref, sh_ref, sc_ref, h_out, xn_out):
    w = w_ref[...]
    moe = w[:, 0:1] * y0_ref[...].astype(F32) + w[:, 1:2] * y1_ref[...].astype(F32)
    h = h_ref[...] + gt2_ref[0] * moe
    h_out[...] = h
    xn_out[...] = (_rms(h, g_ref[0]) * (1.0 + sc_ref[0]) + sh_ref[0]).astype(BF16)


def _combine(layer, y2, wts, hmid, mods3, norm1_g3):
    row = _tile_row
    nxt = layer + 1
    return pl.pallas_call(
        _combine_kernel,
        grid=(NT,),
        in_specs=[pl.BlockSpec((TM, D), lambda t: (t, 0)), pl.BlockSpec((TM, D), lambda t: (NT + t, 0)),
                  pl.BlockSpec((TM, RW), lambda t: (t, 0)), pl.BlockSpec((TM, D), lambda t: (t, 0)),
                  _mod_spec(layer, 5, row), pl.BlockSpec((1, 1, D), lambda t: (nxt, 0, 0)),
                  _mod_spec(nxt, 0, row), _mod_spec(nxt, 1, row)],
        out_specs=[pl.BlockSpec((TM, D), lambda t: (t, 0)), pl.BlockSpec((TM, D), lambda t: (t, 0))],
        out_shape=[jax.ShapeDtypeStruct((P, D), F32), jax.ShapeDtypeStruct((P, D), BF16)],
        compiler_params=_cp(("arbitrary",)),
        name="combine",
    )(y2, y2, wts, hmid, mods3, norm1_g3, mods3, mods3)


def _final_kernel(y0_ref, y1_ref, w_ref, h_ref, gt2_ref, g_ref, out_ref):
    w = w_ref[...]
    moe = w[:, 0:1] * y0_ref[...].astype(F32) + w[:, 1:2] * y1_ref[...].astype(F32)
    h = h_ref[...] + gt2_ref[0] * moe
    out_ref[0] = _rms(h, g_ref[...])


def _final(y2, wts, hmid, mods3, final_g):
    layer = NLAYER - 1
    tl = lambda b, j: b * TPB + 1 + j
    return pl.pallas_call(
        _final_kernel,
        grid=(NB, TPB - 1),
        in_specs=[pl.BlockSpec((TM, D), lambda b, j: (tl(b, j), 0)),
                  pl.BlockSpec((TM, D), lambda b, j: (NT + tl(b, j), 0)),
                  pl.BlockSpec((TM, RW), lambda b, j: (tl(b, j), 0)),
                  pl.BlockSpec((TM, D), lambda b, j: (tl(b, j), 0)),
                  _mod_spec(layer, 5, lambda b, j: b),
                  pl.BlockSpec((1, D), lambda b, j: (0, 0))],
        out_specs=pl.BlockSpec((1, TM, D), lambda b, j: (b, j, 0)),
        out_shape=jax.ShapeDtypeStruct((NB, SEQ, D), F32),
        compiler_params=_cp(("arbitrary", "arbitrary")),
        name="final",
    )(y2, y2, wts, hmid, mods3, final_g)


def _tile_plan(cnt):
    counts = cnt[0, :NEXP]
    ntile = (counts + (TR - 1)) // TR
    tend = jnp.cumsum(ntile)
    tstart = tend - ntile
    n_act = tend[-1]
    tid = jnp.minimum(jnp.arange(NTILE, dtype=I32), n_act - 1)
    tile_expert = jnp.sum((tid[:, None] >= tend[None, :]).astype(I32), axis=1)
    return tstart.astype(I32), tile_expert.astype(I32), n_act.reshape(1).astype(I32)


def kernel(x, c, ctx, c_ctx, w_mod, b_mod, norm1_g, norm2_g, w_in, p_fourier, gmlp_norm_g, gmlp_ws,
           gmlp_bs, p_gmlp, gla_w_a2, gla_b_a, gla_norm_g, p_gla, w_out, router_group_w, router_group_b,
           router_expert_w, router_expert_b, expert_w_gate, expert_w_up, expert_w_down, final_norm_g):
    cvec = jnp.concatenate([c, c_ctx[None, :], jnp.zeros((3, D), F32)], axis=0)
    a_, z_, q_, k_, v_, r_, lr_, gt_ = (w_in[..., 0:512], w_in[..., 512:1536], w_in[..., 1536:2048],
                                        w_in[..., 2048:2560], w_in[..., 2560:3584], w_in[..., 3584:4608],
                                        w_in[..., 4608:4640], w_in[..., 4640:])
    w_main = jnp.concatenate([gt_, v_, r_, z_, a_, q_, k_], axis=-1).astype(BF16)
    w_lr = jnp.pad(lr_, ((0, 0), (0, 0), (0, LRW - 2 * LRANK))).astype(BF16)
    wa_pad = jnp.zeros((NLAYER, 2, LRW, LKD), F32)
    wa_pad = wa_pad.at[:, 0, 0:LRANK].set(gla_w_a2[:, 0]).at[:, 1, LRANK:2 * LRANK].set(gla_w_a2[:, 1])
    ba3 = gla_b_a.reshape(NLAYER, 2, 1, LKD)
    norm1_g3 = norm1_g.reshape(NLAYER, 1, D)
    norm2_g3 = norm2_g.reshape(NLAYER, 1, D)
    gng = gmlp_norm_g.reshape(NLAYER, 1, GDIM)
    ws = gmlp_ws.astype(BF16)
    bsb = jnp.broadcast_to(gmlp_bs[:, :, :, None], (NLAYER, GH, GCH, GCH))
    lng = gla_norm_g.reshape(NLAYER, 1, LVD)
    pf, pg, plw, wo = (p_fourier.astype(BF16), p_gmlp.astype(BF16), p_gla.astype(BF16), w_out.astype(BF16))
    w_r = jnp.pad(jnp.concatenate([router_expert_w, router_group_w], axis=-1),
                  ((0, 0), (0, 0), (0, RW - NEXP - NGRP)))
    wrh = w_r.astype(BF16)
    wrl = (w_r - wrh.astype(F32)).astype(BF16)
    br = jnp.pad(jnp.concatenate([router_expert_b, router_group_b], axis=-1),
                 ((0, 0), (0, RW - NEXP - NGRP))).reshape(NLAYER, 1, RW)
    cs_lat, cs_ctx, cc = _dft_consts()

    mods3 = _mods(cvec, w_mod, b_mod).reshape(NLAYER * 8 * 6, 1, D)

    h, xn = _init(x, ctx, norm1_g3, mods3)
    out = None
    for layer in range(NLAYER):
        pm = _matmul(xn, w_main, layer, 1024, 1536, BF16)
        lr = _matmul(xn, w_lr, layer, 1024, LRW, F32)
        o = _gla(pm, lr, wa_pad, ba3, layer)
        yf = _fourier(pm, cs_lat, cs_ctx, cc)
        hmid, xm, logits = _merge(layer, yf, pm, o, h, mods3, norm2_g3, gng, ws, bsb, lng,
                                  pf, pg, plw, wo, wrh, wrl, br)
        meta, wts, cnt = _route(logits)
        tstart, tile_expert, n_act = _tile_plan(cnt)
        inv = _invert(meta[:, 0:2].reshape(2 * P), tstart)
        y2 = _experts(tile_expert, n_act, inv.reshape(NTILE, 1, TR), xm,
                      expert_w_gate, expert_w_up, expert_w_down, layer)
        if layer + 1 < NLAYER:
            h, xn = _combine(layer, y2, wts, hmid, mods3, norm1_g3)
        else:
            out = _final(y2, wts, hmid, mods3, final_norm_g.reshape(1, D))
    return out
```

```python
import functools
import math

import numpy as np
import jax
import jax.numpy as jnp
from jax import lax
from jax.experimental import pallas as pl
from jax.experimental.pallas import tpu as pltpu

F32 = jnp.float32
BF16 = jnp.bfloat16
I32 = jnp.int32

D = 2048
NB = 4
SEQ = 2048
NLAYER = 4
CTX = 256
EPS = 1e-6
LB = CTX + SEQ
P = NB * LB
TM = 256
TPB = LB // TM
NT = P // TM

FG, FGD = 4, 128
FDIM = FG * FGD
GH, GHD, GCH = 4, 128, 128
GDIM = GH * GHD
LH, LDK, LDV, LRANK, LTAU, LC = 4, 128, 256, 16, 16.0, 64
LKD, LVD = LH * LDK, LH * LDV
NCH = LB // LC
NCTXCH = CTX // LC

C_G0, C_V, C_R, C_ZU, C_ZV, C_A, C_Q, C_K = 0, 6144, 7168, 8192, 8704, 9216, 9728, 10240
NMAIN = 10752
LRW = 128

NGRP, EPG, NEXP, DEXP = 4, 8, 32, 512
TR = 256
NTILE = (2 * P + NEXP * (TR - 1)) // TR + 1
SMAX = NTILE * TR
RW = 128

VMEM_LIMIT = 56 * 1024 * 1024


def _cp(sem, vmem=VMEM_LIMIT):
    return pltpu.CompilerParams(dimension_semantics=sem, vmem_limit_bytes=vmem)


def _dot(a, b):
    return jnp.dot(a, b, preferred_element_type=F32)


def _dot_t(a, b):
    return lax.dot_general(a, b, (((1,), (1,)), ((), ())), preferred_element_type=F32)


def _dot_lt(a, b):
    return lax.dot_general(a, b, (((0,), (0,)), ((), ())), preferred_element_type=F32)


def _split(x):
    hi = x.astype(BF16)
    lo = (x - hi.astype(F32)).astype(BF16)
    return hi, lo


def _sigmoid(x):
    return 1.0 / (1.0 + jnp.exp(-x))


def _silu(x):
    return x * _sigmoid(x)


def _gelu_tanh(x):
    return 0.5 * x * (1.0 + jnp.tanh(math.sqrt(2.0 / math.pi) * (x + 0.044715 * (x * x * x))))


def _rms(x, g):
    return x * lax.rsqrt(jnp.mean(x * x, axis=-1, keepdims=True) + EPS) * g


def _tile_row(t):
    return jnp.where(t % TPB == 0, 4, t // TPB)


def _mod_spec(layer, comp, row_fn):
    return pl.BlockSpec((1, 1, D), lambda *g: ((layer * 8 + row_fn(*g)) * 6 + comp, 0, 0))


def _mods_kernel(c_ref, w_ref, b_ref, o_ref):
    c = c_ref[...]
    s = _silu(c).astype(BF16)
    o_ref[0] = _dot(s, w_ref[0].astype(BF16)) + b_ref[0]


def _mods(cvec, w_mod, b_mod):
    tn = 1024
    return pl.pallas_call(
        _mods_kernel,
        grid=(NLAYER, 6 * D // tn),
        in_specs=[pl.BlockSpec((8, D), lambda l, j: (0, 0)),
                  pl.BlockSpec((1, D, tn), lambda l, j: (l, 0, j)),
                  pl.BlockSpec((1, 1, tn), lambda l, j: (l, 0, j))],
        out_specs=pl.BlockSpec((1, 8, tn), lambda l, j: (l, 0, j)),
        out_shape=jax.ShapeDtypeStruct((NLAYER, 8, 6 * D), F32),
        compiler_params=_cp(("arbitrary", "arbitrary")),
        name="mods",
    )(cvec, w_mod, b_mod.reshape(NLAYER, 1, 6 * D))


def _init_kernel(x_ref, c_ref, g_ref, sh_ref, sc_ref, h_ref, xn_ref):
    j = pl.program_id(1)

    def emit(v):
        h_ref[...] = v
        xn_ref[...] = (_rms(v, g_ref[0]) * (1.0 + sc_ref[0]) + sh_ref[0]).astype(BF16)

    @pl.when(j == 0)
    def _():
        emit(c_ref[0])

    @pl.when(j > 0)
    def _():
        emit(x_ref[0])


def _init(x, ctx, norm1_g3, mods3):
    row = lambda b, j: jnp.where(j == 0, 4, b)
    return pl.pallas_call(
        _init_kernel,
        grid=(NB, TPB),
        in_specs=[pl.BlockSpec((1, TM, D), lambda b, j: (b, jnp.maximum(j - 1, 0), 0)),
                  pl.BlockSpec((1, CTX, D), lambda b, j: (b, 0, 0)),
                  pl.BlockSpec((1, 1, D), lambda b, j: (0, 0, 0)),
                  _mod_spec(0, 0, row), _mod_spec(0, 1, row)],
        out_specs=[pl.BlockSpec((TM, D), lambda b, j: (b * TPB + j, 0)),
                   pl.BlockSpec((TM, D), lambda b, j: (b * TPB + j, 0))],
        out_shape=[jax.ShapeDtypeStruct((P, D), F32), jax.ShapeDtypeStruct((P, D), BF16)],
        compiler_params=_cp(("arbitrary", "arbitrary")),
        name="init",
    )(x, ctx, norm1_g3, mods3, mods3)


def _mm_kernel(x_ref, w_ref, o_ref):
    o_ref[...] = _dot(x_ref[...], w_ref[0]).astype(o_ref.dtype)


def _matmul(x, w_all, layer, tm, tn, out_dtype):
    m, k = x.shape
    n = w_all.shape[2]
    return pl.pallas_call(
        _mm_kernel,
        grid=(n // tn, m // tm),
        in_specs=[pl.BlockSpec((tm, k), lambda j, i: (i, 0)),
                  pl.BlockSpec((1, k, tn), lambda j, i: (layer, 0, j))],
        out_specs=pl.BlockSpec((tm, tn), lambda j, i: (i, j)),
        out_shape=jax.ShapeDtypeStruct((m, n), out_dtype),
        compiler_params=_cp(("arbitrary", "arbitrary")),
        name="inproj",
    )(x, w_all)


def _dft_consts():
    def cs(n):
        k = np.arange(n, dtype=np.int64)
        ang = 2.0 * np.pi * ((k[:, None] * k[None, :]) % n).astype(np.float64) / n
        return np.cos(ang) / math.sqrt(n), np.sin(ang) / math.sqrt(n)

    c_l, s_l = cs(SEQ)
    c_c, s_c = cs(CTX)
    c_g, s_g = cs(FGD)
    cs_lat = np.concatenate([c_l, -s_l], axis=1).astype(BF16)
    cs_ctx = np.concatenate([c_c, -s_c], axis=1).astype(BF16)
    cc = np.concatenate([c_g, s_g], axis=1).astype(BF16)
    return cs_lat, cs_ctx, cc


def _fourier_kernel(a_ref, csl_ref, csc_ref, cc_ref, o_ref, rl_ref, rc_ref):
    j = pl.program_id(1)

    @pl.when(j == 0)
    def _():
        for g in range(FG):
            cols = slice(g * FGD, (g + 1) * FGD)
            t = _dot(a_ref[:, cols], cc_ref[...]).astype(BF16)
            rc_ref[0:CTX, cols] = t[0:CTX, 0:FGD]
            rc_ref[CTX:2 * CTX, cols] = t[0:CTX, FGD:2 * FGD]
            rl_ref[0:SEQ, cols] = t[CTX:LB, 0:FGD]
            rl_ref[SEQ:2 * SEQ, cols] = t[CTX:LB, FGD:2 * FGD]
        o_ref[...] = _dot(csc_ref[...], rc_ref[...]).astype(BF16)

    @pl.when(j > 0)
    def _():
        o_ref[...] = _dot(csl_ref[...], rl_ref[...]).astype(BF16)


def _fourier(pm, cs_lat, cs_ctx, cc):
    return pl.pallas_call(
        _fourier_kernel,
        grid=(NB, TPB),
        in_specs=[pl.BlockSpec((LB, FDIM), lambda b, j: (b, C_A // FDIM)),
                  pl.BlockSpec((TM, 2 * SEQ), lambda b, j: (jnp.maximum(j - 1, 0), 0)),
                  pl.BlockSpec((CTX, 2 * CTX), lambda b, j: (0, 0)),
                  pl.BlockSpec((FGD, 2 * FGD), lambda b, j: (0, 0))],
        out_specs=pl.BlockSpec((TM, FDIM), lambda b, j: (b * TPB + j, 0)),
        out_shape=jax.ShapeDtypeStruct((P, FDIM), BF16),
        scratch_shapes=[pltpu.VMEM((2 * SEQ, FDIM), BF16), pltpu.VMEM((2 * CTX, FDIM), BF16)],
        compiler_params=_cp(("arbitrary", "arbitrary")),
        name="fourier",
    )(pm, cs_lat, cs_ctx, cc)


SB = 256
SBC = SB // LC
NSB = LB // SB


def _gla_kernel(q_ref, k_ref, v_ref, lr_ref, wa_ref, ba_ref, o_ref,
                qd_ref, oacc_ref, ds_ref, gam_ref, sall_ref, sf_ref, sb_ref):
    ri = lax.broadcasted_iota(I32, (SB, SB), 0)
    ci = lax.broadcasted_iota(I32, (SB, SB), 1)
    same = (ri // LC) == (ci // LC)
    tri = (same & (ci <= ri)).astype(BF16)
    keep_f = same & (ci <= ri)
    keep_b = same & (ci > ri)
    rchunk = lax.broadcasted_iota(I32, (SB, LDK), 0) // LC
    scale = LDK ** -0.5
    wa = wa_ref[...]
    ba = ba_ref[...]

    def phase1(sb, carry):
        rows = pl.ds(pl.multiple_of(sb * SB, SB), SB)
        logits = _dot(lr_ref[rows, :].astype(BF16), wa) + ba
        g = (jnp.minimum(logits, 0.0) - jnp.log1p(jnp.exp(-jnp.abs(logits)))) * (1.0 / LTAU)
        g_hi, g_lo = _split(g)
        pre = _dot(tri, g_hi) + _dot(tri, g_lo)
        tot = jnp.concatenate(
            [jnp.broadcast_to(pre[c * LC + LC - 1:c * LC + LC, :], (LC, 2 * LDK)) for c in range(SBC)], axis=0)
        q = q_ref[rows, :].astype(F32) * scale
        k = k_ref[rows, :].astype(F32)
        v = v_ref[rows, :]
        s_sum = None
        kts = []
        for d in range(2):
            cols = slice(d * LDK, (d + 1) * LDK)
            t_d = tot[:, cols]
            b_d = pre[:, cols] if d == 0 else t_d - pre[:, cols] + g[:, cols]
            q_dec = (q * jnp.exp(b_d)).astype(BF16)
            k_inv = (k * jnp.exp(-b_d)).astype(BF16)
            k_tail = (k * jnp.exp(t_d - b_d)).astype(BF16)
            sc = jnp.where(keep_f if d == 0 else keep_b, _dot_t(q_dec, k_inv), 0.0)
            s_sum = sc if s_sum is None else s_sum + sc
            qd_ref[rows, cols] = q_dec
            zero = jnp.zeros_like(k_tail)
            kts += [jnp.where(rchunk == c, k_tail, zero) for c in range(SBC)]
            for c in range(SBC):
                gam_ref[d, sb * SBC + c] = jnp.exp(t_d[c * LC:c * LC + 1, :])
        oacc_ref[rows, :] = _dot(s_sum.astype(BF16), v)
        dst = _dot_lt(v, jnp.concatenate(kts, axis=1))
        for d in range(2):
            for c in range(SBC):
                j = d * SBC + c
                ds_ref[d, sb * SBC + c] = dst[:, j * LDK:(j + 1) * LDK]
        return carry

    lax.fori_loop(0, NSB, phase1, 0, unroll=3)

    sf_ref[...] = jnp.zeros_like(sf_ref)
    sb_ref[...] = jnp.zeros_like(sb_ref)

    def phase2(i, carry):
        nb = jnp.where(i < NCTXCH, NCTXCH - 1 - i, NCH + NCTXCH - 1 - i)
        s_f = sf_ref[...]
        s_b = sb_ref[...]
        sall_ref[i, :, 0:LDK] = s_f.astype(BF16)
        sall_ref[nb, :, LDK:2 * LDK] = s_b.astype(BF16)
        sf_ref[...] = s_f * gam_ref[0, i] + ds_ref[0, i]
        sb_ref[...] = s_b * gam_ref[1, nb] + ds_ref[1, nb]
        return carry

    lax.fori_loop(0, NCH, phase2, 0)

    def phase3(n, carry):
        rows = pl.ds(pl.multiple_of(n * LC, LC), LC)
        o_ref[rows, :] = (oacc_ref[rows, :] + _dot_t(qd_ref[rows, :], sall_ref[n])).astype(BF16)
        return carry

    lax.fori_loop(0, NCH, phase3, 0, unroll=4)


def _gla(pm, lr, wa_cat, ba_cat, layer):
    return pl.pallas_call(
        _gla_kernel,
        grid=(NB, LH),
        in_specs=[pl.BlockSpec((LB, LDK), lambda b, h: (b, C_Q // LDK + h)),
                  pl.BlockSpec((LB, LDK), lambda b, h: (b, C_K // LDK + h)),
                  pl.BlockSpec((LB, LDV), lambda b, h: (b, C_V // LDV + h)),
                  pl.BlockSpec((LB, LRW), lambda b, h: (b, 0)),
                  pl.BlockSpec((None, None, LRW, 2 * LDK), lambda b, h: (layer, h, 0, 0)),
                  pl.BlockSpec((None, None, 1, 2 * LDK), lambda b, h: (layer, h, 0, 0))],
        out_specs=pl.BlockSpec((LB, LDV), lambda b, h: (b, h)),
        out_shape=jax.ShapeDtypeStruct((P, LVD), BF16),
        scratch_shapes=[pltpu.VMEM((LB, 2 * LDK), BF16), pltpu.VMEM((LB, LDV), F32),
                        pltpu.VMEM((2, NCH, LDV, LDK), F32), pltpu.VMEM((2, NCH, 1, LDK), F32),
                        pltpu.VMEM((NCH, LDV, 2 * LDK), BF16),
                        pltpu.VMEM((LDV, LDK), F32), pltpu.VMEM((LDV, LDK), F32)],
        compiler_params=_cp(("arbitrary", "arbitrary")),
        name="gla",
    )(pm, pm, pm, lr, wa_cat, ba_cat)


def _merge_kernel(yf_ref, zu_ref, zv_ref, o_ref, r_ref, g0_ref, g1_ref, g2_ref, h_ref,
                  gt1_ref, sh2_ref, sc2_ref, n2g_ref, gng_ref, ws_ref, bs_ref, lng_ref,
                  pf_ref, pg_ref, pl_ref, wo_ref, wrh_ref, wrl_ref, br_ref,
                  hmid_ref, xm_ref, lg_ref):
    y = _sigmoid(g0_ref[...].astype(F32)) * _dot(yf_ref[...], pf_ref[...])

    u = _gelu_tanh(zu_ref[...].astype(F32))
    v = _rms(_gelu_tanh(zv_ref[...].astype(F32)), gng_ref[...]).astype(BF16)
    chunks = []
    for ch in range(TM // GCH):
        rows = slice(ch * GCH, (ch + 1) * GCH)
        heads = [_dot(ws_ref[hd], v[rows, hd * GHD:(hd + 1) * GHD]) + bs_ref[hd] for hd in range(GH)]
        chunks.append(jnp.concatenate(heads, axis=1))
    s = jnp.concatenate(chunks, axis=0)
    y += _sigmoid(g1_ref[...].astype(F32)) * _dot((u * s).astype(BF16), pg_ref[...])

    o = o_ref[...].astype(F32)
    lng = lng_ref[...]
    heads = [_rms(o[:, hd * LDV:(hd + 1) * LDV], lng[:, hd * LDV:(hd + 1) * LDV]) for hd in range(LH)]
    ol = (jnp.concatenate(heads, axis=1) * _silu(r_ref[...].astype(F32))).astype(BF16)
    y += _sigmoid(g2_ref[...].astype(F32)) * _dot(ol, pl_ref[...])

    hmid = h_ref[...] + gt1_ref[0] * _dot(y.astype(BF16), wo_ref[...])
    hmid_ref[...] = hmid

    xm = _rms(hmid, n2g_ref[0]) * (1.0 + sc2_ref[0]) + sh2_ref[0]
    xm_ref[...] = xm
    xm_hi, xm_lo = _split(xm)
    lg_ref[...] = (_dot(xm_hi, wrh_ref[...]) + _dot(xm_hi, wrl_ref[...]) + _dot(xm_lo, wrh_ref[...])
                   + br_ref[...])


def _merge(layer, yf, pm, o, h, mods3, norm2_g3, gng, ws, bsb, lng, pf, pg, plw, wo, wrh, wrl, br):
    row = _tile_row
    tile = lambda w, c: pl.BlockSpec((TM, w), lambda t: (t, c))
    lay3 = lambda a, b: pl.BlockSpec((None, a, b), lambda t: (layer, 0, 0), pipeline_mode=pl.Buffered(1))
    lay4 = lambda a, b, c: pl.BlockSpec((None, a, b, c), lambda t: (layer, 0, 0, 0))
    return pl.pallas_call(
        _merge_kernel,
        grid=(NT,),
        in_specs=[tile(FDIM, 0), tile(GDIM, C_ZU // GDIM), tile(GDIM, C_ZV // GDIM), tile(LVD, 0),
                  tile(LVD, C_R // LVD), tile(D, 0), tile(D, 1), tile(D, 2), tile(D, 0),
                  _mod_spec(layer, 2, row), _mod_spec(layer, 3, row), _mod_spec(layer, 4, row),
                  pl.BlockSpec((1, 1, D), lambda t: (layer, 0, 0)),
                  lay3(1, GDIM), lay4(GH, GCH, GCH), lay4(GH, GCH, GCH), lay3(1, LVD),
                  lay3(FDIM, D), lay3(GDIM, D), lay3(LVD, D), lay3(D, D),
                  lay3(D, RW), lay3(D, RW), lay3(1, RW)],
        out_specs=[tile(D, 0), tile(D, 0), tile(RW, 0)],
        out_shape=[jax.ShapeDtypeStruct((P, D), F32), jax.ShapeDtypeStruct((P, D), F32),
                   jax.ShapeDtypeStruct((P, RW), F32)],
        compiler_params=_cp(("arbitrary",)),
        name="merge",
    )(yf, pm, pm, o, pm, pm, pm, pm, h, mods3, mods3, mods3, norm2_g3, gng, ws, bsb, lng,
      pf, pg, plw, wo, wrh, wrl, br)


def _route_kernel(lg_ref, meta_ref, wts_ref, cnt_ref, carry_ref):
    t = pl.program_id(0)

    @pl.when(t == 0)
    def _():
        carry_ref[...] = jnp.zeros_like(carry_ref)

    lg = lg_ref[...]
    lane = lax.broadcasted_iota(I32, (TM, RW), 1)
    lane_f = lane.astype(F32)
    ninf = jnp.float32(-jnp.inf)

    def first_max(x):
        m = jnp.max(x, axis=-1, keepdims=True)
        first = jnp.min(jnp.where(x == m, lane_f, float(RW)), axis=-1, keepdims=True)
        return m, first.astype(I32)

    is_g = (lane >= NEXP) & (lane < NEXP + NGRP)
    gmax, glane = first_max(jnp.where(is_g, lg, ninf))
    gsum = jnp.sum(jnp.where(is_g, jnp.exp(lg - gmax), 0.0), axis=-1, keepdims=True)
    g_w = 1.0 / gsum
    lo = (glane - NEXP) * EPG
    in_grp = (lane >= lo) & (lane < lo + EPG)
    el = jnp.where(in_grp, lg, ninf)
    v1, l1 = first_max(el)
    v2, l2 = first_max(jnp.where(lane == l1, ninf, el))
    e = jnp.exp(v2 - v1)
    w1 = g_w / (1.0 + e)
    w2 = g_w * e / (1.0 + e)

    hit1 = lane == l1
    hit2 = lane == l2
    m = (hit1 | hit2).astype(BF16)
    ri = lax.broadcasted_iota(I32, (TM, TM), 0)
    ci = lax.broadcasted_iota(I32, (TM, TM), 1)
    before = _dot((ci < ri).astype(BF16), m) + carry_ref[0:1, :]
    r1 = jnp.sum(jnp.where(hit1, before, 0.0), axis=-1, keepdims=True).astype(I32)
    r2 = jnp.sum(jnp.where(hit2, before, 0.0), axis=-1, keepdims=True).astype(I32)
    total = carry_ref[0:1, :] + jnp.sum(m.astype(F32), axis=0, keepdims=True)
    carry_ref[...] = jnp.broadcast_to(total, carry_ref.shape)
    cnt_ref[...] = jnp.broadcast_to(total, cnt_ref.shape).astype(I32)

    pk1 = l1 * 65536 + r1
    pk2 = l2 * 65536 + r2
    meta_ref[...] = jnp.where(lane == 0, pk1, jnp.where(lane == 1, pk2, 0))
    wts_ref[...] = jnp.where(lane == 0, w1, jnp.where(lane == 1, w2, 0.0))


def _route(logits):
    return pl.pallas_call(
        _route_kernel,
        grid=(NT,),
        in_specs=[pl.BlockSpec((TM, RW), lambda t: (t, 0))],
        out_specs=[pl.BlockSpec((TM, RW), lambda t: (t, 0)), pl.BlockSpec((TM, RW), lambda t: (t, 0)),
                   pl.BlockSpec((8, RW), lambda t: (0, 0))],
        out_shape=[jax.ShapeDtypeStruct((P, RW), I32), jax.ShapeDtypeStruct((P, RW), F32),
                   jax.ShapeDtypeStruct((8, RW), I32)],
        scratch_shapes=[pltpu.VMEM((8, RW), F32)],
        compiler_params=_cp(("arbitrary",)),
        name="route",
    )(logits)


def _invert_kernel(pk_ref, tstart_ref, inv_ref):
    def fill(i, c):
        inv_ref[i] = jnp.int32(-1)
        return c

    lax.fori_loop(0, SMAX, fill, 0, unroll=8)

    def body(i, c):
        pk = pk_ref[i]
        e = lax.shift_right_logical(pk, 16)
        rank = pk & 0xFFFF
        slot = tstart_ref[e] * TR + rank
        inv_ref[slot] = (i & 1) * P + lax.shift_right_logical(i, 1)
        return c

    lax.fori_loop(0, 2 * P, body, 0, unroll=8)


def _invert(pk, tstart):
    smem = pl.BlockSpec(memory_space=pltpu.SMEM)
    return pl.pallas_call(
        _invert_kernel,
        in_specs=[smem, smem],
        out_specs=smem,
        out_shape=jax.ShapeDtypeStruct((SMAX,), I32),
        name="invert",
    )(pk, tstart)


def _experts_kernel(te_ref, na_ref, inv_ref, invn_ref, x_hbm, wg_ref, wu_ref, wd_ref, y_hbm,
                    xbuf, ybuf, wgb, wub, wdb, gsem, ssem):
    i = pl.program_id(0)
    n_act = na_ref[0]
    slot = i % 2

    def gather(idx_ref, buf_slot):
        def body(r, c):
            code = idx_ref[0, 0, r]
            tok = jnp.where(code < 0, 0, jnp.where(code >= P, code - P, code))
            pltpu.make_async_copy(x_hbm.at[pl.ds(tok, 1)], xbuf.at[buf_slot, pl.ds(r, 1)],
                                  gsem.at[buf_slot]).start()
            return c
        lax.fori_loop(0, TR, body, 0, unroll=8)

    @pl.when(i == 0)
    def _():
        ybuf[...] = jnp.zeros_like(ybuf)
        spare = pltpu.make_async_copy(ybuf, y_hbm.at[pl.ds(2 * P, TR)], ssem.at[0])
        spare.start()
        spare.wait()
        gather(inv_ref, 0)

    @pl.when(i + 1 < n_act)
    def _():
        gather(invn_ref, 1 - slot)

    @pl.when(i < n_act)
    def _():
        first = jnp.logical_or(i == 0, te_ref[i] != te_ref[jnp.maximum(i - 1, 0)])

        @pl.when(first)
        def _():
            wgb[...] = wg_ref[0].astype(BF16)
            wub[...] = wu_ref[0].astype(BF16)
            wdb[...] = wd_ref[0].astype(BF16)

        pltpu.make_async_copy(xbuf.at[slot], xbuf.at[slot], gsem.at[slot]).wait()
        x = xbuf[slot].astype(BF16)
        hg = _dot(x, wgb[...])
        hu = _dot(x, wub[...])
        act = (_silu(hg) * hu).astype(BF16)
        y = _dot(act, wdb[...])

        @pl.when(i > 0)
        def _():
            pltpu.make_async_copy(ybuf, ybuf, ssem.at[0]).wait()

        ybuf[...] = y

        def scatter(r, c):
            code = inv_ref[0, 0, r]
            dst = jnp.where(code < 0, 2 * P + r, code)
            pltpu.make_async_copy(ybuf.at[pl.ds(r, 1)], y_hbm.at[pl.ds(dst, 1)], ssem.at[0]).start()
            return c
        lax.fori_loop(0, TR, scatter, 0, unroll=8)

        @pl.when(i == n_act - 1)
        def _():
            pltpu.make_async_copy(ybuf, ybuf, ssem.at[0]).wait()


def _experts(tile_expert, n_act, inv3, xm, wg, wu, wd, layer):
    wspec = lambda a, b: pl.BlockSpec((None, 1, a, b), lambda i, te, na: (layer, te[i], 0, 0))
    grid_spec = pltpu.PrefetchScalarGridSpec(
        num_scalar_prefetch=2,
        grid=(NTILE,),
        in_specs=[pl.BlockSpec((1, 1, TR), lambda i, te, na: (i, 0, 0), memory_space=pltpu.SMEM),
                  pl.BlockSpec((1, 1, TR), lambda i, te, na: (jnp.minimum(i + 1, NTILE - 1), 0, 0),
                               memory_space=pltpu.SMEM),
                  pl.BlockSpec(memory_space=pl.ANY),
                  wspec(D, DEXP), wspec(D, DEXP), wspec(DEXP, D)],
        out_specs=pl.BlockSpec(memory_space=pl.ANY),
        scratch_shapes=[pltpu.VMEM((2, TR, D), F32), pltpu.VMEM((TR, D), F32),
                        pltpu.VMEM((D, DEXP), BF16), pltpu.VMEM((D, DEXP), BF16),
                        pltpu.VMEM((DEXP, D), BF16),
                        pltpu.SemaphoreType.DMA((2,)), pltpu.SemaphoreType.DMA((1,))],
    )
    return pl.pallas_call(
        _experts_kernel,
        grid_spec=grid_spec,
        out_shape=jax.ShapeDtypeStruct((2 * P + TR, D), F32),
        compiler_params=_cp(("arbitrary",)),
        name="experts",
    )(tile_expert, n_act, inv3, inv3, xm, wg, wu, wd)


def _combine_kernel(y0_ref, y1_ref, w_ref, h_ref, gt2_ref, g_ref, sh_ref, sc_ref, h_out, xn_out):
    w = w_ref[...]
    moe = w[:, 0:1] * y0_ref[...].astype(F32) + w[:, 1:2] * y1_ref[...].astype(F32)
    h = h_ref[...] + gt2_ref[0] * moe
    h_out[...] = h
    xn_out[...] = (_rms(h, g_ref[0]) * (1.0 + sc_ref[0]) + sh_ref[0]).astype(BF16)


def _combine(layer, y2, wts, hmid, mods3, norm1_g3):
    row = _tile_row
    nxt = layer + 1
    return pl.pallas_call(
        _combine_kernel,
        grid=(NT,),
        in_specs=[pl.BlockSpec((TM, D), lambda t: (t, 0)), pl.BlockSpec((TM, D), lambda t: (NT + t, 0)),
                  pl.BlockSpec((TM, RW), lambda t: (t, 0)), pl.BlockSpec((TM, D), lambda t: (t, 0)),
                  _mod_spec(layer, 5, row), pl.BlockSpec((1, 1, D), lambda t: (nxt, 0, 0)),
                  _mod_spec(nxt, 0, row), _mod_spec(nxt, 1, row)],
        out_specs=[pl.BlockSpec((TM, D), lambda t: (t, 0)), pl.BlockSpec((TM, D), lambda t: (t, 0))],
        out_shape=[jax.ShapeDtypeStruct((P, D), F32), jax.ShapeDtypeStruct((P, D), BF16)],
        compiler_params=_cp(("arbitrary",)),
        name="combine",
    )(y2, y2, wts, hmid, mods3, norm1_g3, mods3, mods3)


def _final_kernel(y0_ref, y1_ref, w_ref, h_ref, gt2_ref, g_ref, out_ref):
    w = w_ref[...]
    moe = w[:, 0:1] * y0_ref[...].astype(F32) + w[:, 1:2] * y1_ref[...].astype(F32)
    h = h_ref[...] + gt2_ref[0] * moe
    out_ref[0] = _rms(h, g_ref[...])


def _final(y2, wts, hmid, mods3, final_g):
    layer = NLAYER - 1
    tl = lambda b, j: b * TPB + 1 + j
    return pl.pallas_call(
        _final_kernel,
        grid=(NB, TPB - 1),
        in_specs=[pl.BlockSpec((TM, D), lambda b, j: (tl(b, j), 0)),
                  pl.BlockSpec((TM, D), lambda b, j: (NT + tl(b, j), 0)),
                  pl.BlockSpec((TM, RW), lambda b, j: (tl(b, j), 0)),
                  pl.BlockSpec((TM, D), lambda b, j: (tl(b, j), 0)),
                  _mod_spec(layer, 5, lambda b, j: b),
                  pl.BlockSpec((1, D), lambda b, j: (0, 0))],
        out_specs=pl.BlockSpec((1, TM, D), lambda b, j: (b, j, 0)),
        out_shape=jax.ShapeDtypeStruct((NB, SEQ, D), F32),
        compiler_params=_cp(("arbitrary", "arbitrary")),
        name="final",
    )(y2, y2, wts, hmid, mods3, final_g)


def _tile_plan(cnt):
    counts = cnt[0, :NEXP]
    ntile = (counts + (TR - 1)) // TR
    tend = jnp.cumsum(ntile)
    tstart = tend - ntile
    n_act = tend[-1]
    tid = jnp.minimum(jnp.arange(NTILE, dtype=I32), n_act - 1)
    tile_expert = jnp.sum((tid[:, None] >= tend[None, :]).astype(I32), axis=1)
    return tstart.astype(I32), tile_expert.astype(I32), n_act.reshape(1).astype(I32)


def kernel(x, c, ctx, c_ctx, w_mod, b_mod, norm1_g, norm2_g, w_in, p_fourier, gmlp_norm_g, gmlp_ws,
           gmlp_bs, p_gmlp, gla_w_a2, gla_b_a, gla_norm_g, p_gla, w_out, router_group_w, router_group_b,
           router_expert_w, router_expert_b, expert_w_gate, expert_w_up, expert_w_down, final_norm_g):
    cvec = jnp.concatenate([c, c_ctx[None, :], jnp.zeros((3, D), F32)], axis=0)
    a_, z_, q_, k_, v_, r_, lr_, gt_ = (w_in[..., 0:512], w_in[..., 512:1536], w_in[..., 1536:2048],
                                        w_in[..., 2048:2560], w_in[..., 2560:3584], w_in[..., 3584:4608],
                                        w_in[..., 4608:4640], w_in[..., 4640:])
    w_main = jnp.concatenate([gt_, v_, r_, z_, a_, q_, k_], axis=-1).astype(BF16)
    w_lr = jnp.pad(lr_, ((0, 0), (0, 0), (0, LRW - 2 * LRANK))).astype(BF16)
    wa_pad = jnp.zeros((NLAYER, 2, LRW, LKD), F32)
    wa_pad = wa_pad.at[:, 0, 0:LRANK].set(gla_w_a2[:, 0]).at[:, 1, LRANK:2 * LRANK].set(gla_w_a2[:, 1])
    wa_cat = wa_pad.reshape(NLAYER, 2, LRW, LH, LDK).transpose(0, 3, 2, 1, 4).reshape(NLAYER, LH, LRW, 2 * LDK)
    wa_cat = wa_cat.astype(BF16)
    ba_cat = gla_b_a.reshape(NLAYER, 2, LH, LDK).transpose(0, 2, 1, 3).reshape(NLAYER, LH, 1, 2 * LDK)
    norm1_g3 = norm1_g.reshape(NLAYER, 1, D)
    norm2_g3 = norm2_g.reshape(NLAYER, 1, D)
    gng = gmlp_norm_g.reshape(NLAYER, 1, GDIM)
    ws = gmlp_ws.astype(BF16)
    bsb = jnp.broadcast_to(gmlp_bs[:, :, :, None], (NLAYER, GH, GCH, GCH))
    lng = gla_norm_g.reshape(NLAYER, 1, LVD)
    pf, pg, plw, wo = (p_fourier.astype(BF16), p_gmlp.astype(BF16), p_gla.astype(BF16), w_out.astype(BF16))
    w_r = jnp.pad(jnp.concatenate([router_expert_w, router_group_w], axis=-1),
                  ((0, 0), (0, 0), (0, RW - NEXP - NGRP)))
    wrh = w_r.astype(BF16)
    wrl = (w_r - wrh.astype(F32)).astype(BF16)
    br = jnp.pad(jnp.concatenate([router_expert_b, router_group_b], axis=-1),
                 ((0, 0), (0, RW - NEXP - NGRP))).reshape(NLAYER, 1, RW)
    cs_lat, cs_ctx, cc = _dft_consts()

    mods3 = _mods(cvec, w_mod, b_mod).reshape(NLAYER * 8 * 6, 1, D)

    h, xn = _init(x, ctx, norm1_g3, mods3)
    out = None
    for layer in range(NLAYER):
        pm = _matmul(xn, w_main, layer, 1024, 1536, BF16)
        lr = _matmul(xn, w_lr, layer, 1024, LRW, F32)
        o = _gla(pm, lr, wa_cat, ba_cat, layer)
        yf = _fourier(pm, cs_lat, cs_ctx, cc)
        hmid, xm, logits = _merge(layer, yf, pm, o, h, mods3, norm2_g3, gng, ws, bsb, lng,
                                  pf, pg, plw, wo, wrh, wrl, br)
        meta, wts, cnt = _route(logits)
        tstart, tile_expert, n_act = _tile_plan(cnt)
        inv = _invert(meta[:, 0:2].reshape(2 * P), tstart)
        y2 = _experts(tile_expert, n_act, inv.reshape(NTILE, 1, TR), xm,
                      expert_w_gate, expert_w_up, expert_w_down, layer)
        if layer + 1 < NLAYER:
            h, xn = _combine(layer, y2, wts, hmid, mods3, norm1_g3)
        else:
            out = _final(y2, wts, hmid, mods3, final_norm_g.reshape(1, D))
    return out
```

```python
import functools
import math

import numpy as np
import jax
import jax.numpy as jnp
from jax import lax
from jax.experimental import pallas as pl
from jax.experimental.pallas import tpu as pltpu

F32 = jnp.float32
BF16 = jnp.bfloat16
I32 = jnp.int32

D = 2048
NB = 4
SEQ = 2048
NLAYER = 4
CTX = 256
EPS = 1e-6
LB = CTX + SEQ
P = NB * LB
TM = 256
TPB = LB // TM
NT = P // TM

FG, FGD = 4, 128
FDIM = FG * FGD
GH, GHD, GCH = 4, 128, 128
GDIM = GH * GHD
LH, LDK, LDV, LRANK, LTAU, LC = 4, 128, 256, 16, 16.0, 64
LKD, LVD = LH * LDK, LH * LDV
NCH = LB // LC
NCTXCH = CTX // LC

C_G0, C_V, C_R, C_ZU, C_ZV, C_A, C_Q, C_K = 0, 6144, 7168, 8192, 8704, 9216, 9728, 10240
NMAIN = 10752
LRW = 128

NGRP, EPG, NEXP, DEXP = 4, 8, 32, 512
TR = 256
CH = 8
LCH = (2 * TM + NEXP * (CH - 1)) // CH
LROWS = 768
TCH = TR // CH
NCHUNK = 2 * P // CH + NT * NEXP * (CH - 1) // CH + NEXP * (TCH - 1)
NTILE = NCHUNK // TCH + 1
NSLOT = NTILE * TR
RW = 128

VMEM_LIMIT = 56 * 1024 * 1024


def _cp(sem, vmem=VMEM_LIMIT):
    return pltpu.CompilerParams(dimension_semantics=sem, vmem_limit_bytes=vmem)


def _dot(a, b):
    return jnp.dot(a, b, preferred_element_type=F32)


def _dot_t(a, b):
    return lax.dot_general(a, b, (((1,), (1,)), ((), ())), preferred_element_type=F32)


def _dot_lt(a, b):
    return lax.dot_general(a, b, (((0,), (0,)), ((), ())), preferred_element_type=F32)


def _split(x):
    hi = x.astype(BF16)
    lo = (x - hi.astype(F32)).astype(BF16)
    return hi, lo


def _sigmoid(x):
    return 1.0 / (1.0 + jnp.exp(-x))


def _silu(x):
    return x * _sigmoid(x)


def _gelu_tanh(x):
    return 0.5 * x * (1.0 + jnp.tanh(math.sqrt(2.0 / math.pi) * (x + 0.044715 * (x * x * x))))


def _rms(x, g):
    return x * lax.rsqrt(jnp.mean(x * x, axis=-1, keepdims=True) + EPS) * g


def _tile_row(t):
    return jnp.where(t % TPB == 0, 4, t // TPB)


def _mod_spec(layer, comp, row_fn):
    return pl.BlockSpec((1, 1, D), lambda *g: ((layer * 8 + row_fn(*g)) * 6 + comp, 0, 0))


def _mods_kernel(c_ref, w_ref, b_ref, o_ref):
    c = c_ref[...]
    s = _silu(c).astype(BF16)
    o_ref[0] = _dot(s, w_ref[0].astype(BF16)) + b_ref[0]


def _mods(cvec, w_mod, b_mod):
    tn = 1024
    return pl.pallas_call(
        _mods_kernel,
        grid=(NLAYER, 6 * D // tn),
        in_specs=[pl.BlockSpec((8, D), lambda l, j: (0, 0)),
                  pl.BlockSpec((1, D, tn), lambda l, j: (l, 0, j)),
                  pl.BlockSpec((1, 1, tn), lambda l, j: (l, 0, j))],
        out_specs=pl.BlockSpec((1, 8, tn), lambda l, j: (l, 0, j)),
        out_shape=jax.ShapeDtypeStruct((NLAYER, 8, 6 * D), F32),
        compiler_params=_cp(("arbitrary", "arbitrary")),
        name="mods",
    )(cvec, w_mod, b_mod.reshape(NLAYER, 1, 6 * D))


def _init_kernel(x_ref, c_ref, g_ref, sh_ref, sc_ref, h_ref, xn_ref):
    j = pl.program_id(1)

    def emit(v):
        h_ref[...] = v
        xn_ref[...] = (_rms(v, g_ref[0]) * (1.0 + sc_ref[0]) + sh_ref[0]).astype(BF16)

    @pl.when(j == 0)
    def _():
        emit(c_ref[0])

    @pl.when(j > 0)
    def _():
        emit(x_ref[0])


def _init(x, ctx, norm1_g3, mods3):
    row = lambda b, j: jnp.where(j == 0, 4, b)
    return pl.pallas_call(
        _init_kernel,
        grid=(NB, TPB),
        in_specs=[pl.BlockSpec((1, TM, D), lambda b, j: (b, jnp.maximum(j - 1, 0), 0)),
                  pl.BlockSpec((1, CTX, D), lambda b, j: (b, 0, 0)),
                  pl.BlockSpec((1, 1, D), lambda b, j: (0, 0, 0)),
                  _mod_spec(0, 0, row), _mod_spec(0, 1, row)],
        out_specs=[pl.BlockSpec((TM, D), lambda b, j: (b * TPB + j, 0)),
                   pl.BlockSpec((TM, D), lambda b, j: (b * TPB + j, 0))],
        out_shape=[jax.ShapeDtypeStruct((P, D), F32), jax.ShapeDtypeStruct((P, D), BF16)],
        compiler_params=_cp(("arbitrary", "arbitrary")),
        name="init",
    )(x, ctx, norm1_g3, mods3, mods3)


def _mm_kernel(x_ref, w_ref, o_ref):
    o_ref[...] = _dot(x_ref[...], w_ref[0]).astype(o_ref.dtype)


def _matmul(x, w_all, layer, tm, tn, out_dtype):
    m, k = x.shape
    n = w_all.shape[2]
    return pl.pallas_call(
        _mm_kernel,
        grid=(n // tn, m // tm),
        in_specs=[pl.BlockSpec((tm, k), lambda j, i: (i, 0)),
                  pl.BlockSpec((1, k, tn), lambda j, i: (layer, 0, j))],
        out_specs=pl.BlockSpec((tm, tn), lambda j, i: (i, j)),
        out_shape=jax.ShapeDtypeStruct((m, n), out_dtype),
        compiler_params=_cp(("arbitrary", "arbitrary")),
        name="inproj",
    )(x, w_all)


def _dft_consts():
    def cs(n):
        k = np.arange(n, dtype=np.int64)
        ang = 2.0 * np.pi * ((k[:, None] * k[None, :]) % n).astype(np.float64) / n
        return np.cos(ang) / math.sqrt(n), np.sin(ang) / math.sqrt(n)

    c_l, s_l = cs(SEQ)
    c_c, s_c = cs(CTX)
    c_g, s_g = cs(FGD)
    cs_lat = np.concatenate([c_l, -s_l], axis=1).astype(BF16)
    cs_ctx = np.concatenate([c_c, -s_c], axis=1).astype(BF16)
    cc = np.concatenate([c_g, s_g], axis=1).astype(BF16)
    return cs_lat, cs_ctx, cc


def _fourier_kernel(a_ref, csl_ref, csc_ref, cc_ref, o_ref, rl_ref, rc_ref):
    j = pl.program_id(1)

    @pl.when(j == 0)
    def _():
        for g in range(FG):
            cols = slice(g * FGD, (g + 1) * FGD)
            t = _dot(a_ref[:, cols], cc_ref[...]).astype(BF16)
            rc_ref[0:CTX, cols] = t[0:CTX, 0:FGD]
            rc_ref[CTX:2 * CTX, cols] = t[0:CTX, FGD:2 * FGD]
            rl_ref[0:SEQ, cols] = t[CTX:LB, 0:FGD]
            rl_ref[SEQ:2 * SEQ, cols] = t[CTX:LB, FGD:2 * FGD]
        o_ref[...] = _dot(csc_ref[...], rc_ref[...]).astype(BF16)

    @pl.when(j > 0)
    def _():
        o_ref[...] = _dot(csl_ref[...], rl_ref[...]).astype(BF16)


def _fourier(pm, cs_lat, cs_ctx, cc):
    return pl.pallas_call(
        _fourier_kernel,
        grid=(NB, TPB),
        in_specs=[pl.BlockSpec((LB, FDIM), lambda b, j: (b, C_A // FDIM)),
                  pl.BlockSpec((TM, 2 * SEQ), lambda b, j: (jnp.maximum(j - 1, 0), 0)),
                  pl.BlockSpec((CTX, 2 * CTX), lambda b, j: (0, 0)),
                  pl.BlockSpec((FGD, 2 * FGD), lambda b, j: (0, 0))],
        out_specs=pl.BlockSpec((TM, FDIM), lambda b, j: (b * TPB + j, 0)),
        out_shape=jax.ShapeDtypeStruct((P, FDIM), BF16),
        scratch_shapes=[pltpu.VMEM((2 * SEQ, FDIM), BF16), pltpu.VMEM((2 * CTX, FDIM), BF16)],
        compiler_params=_cp(("arbitrary", "arbitrary")),
        name="fourier",
    )(pm, cs_lat, cs_ctx, cc)


SB = 256
SBC = SB // LC
NSB = LB // SB


def _gla_kernel(q_ref, k_ref, v_ref, lr_ref, wa_ref, ba_ref, o_ref,
                qd_ref, oacc_ref, ds_ref, gam_ref, sall_ref, sf_ref, sb_ref):
    ri = lax.broadcasted_iota(I32, (SB, SB), 0)
    ci = lax.broadcasted_iota(I32, (SB, SB), 1)
    same = (ri // LC) == (ci // LC)
    tri = (same & (ci <= ri)).astype(BF16)
    keep_f = same & (ci <= ri)
    keep_b = same & (ci > ri)
    rchunk = lax.broadcasted_iota(I32, (SB, LDK), 0) // LC
    scale = LDK ** -0.5
    wa = wa_ref[...]
    ba = ba_ref[...]

    def phase1(sb, carry):
        rows = pl.ds(pl.multiple_of(sb * SB, SB), SB)
        logits = _dot(lr_ref[rows, :].astype(BF16), wa) + ba
        g = (jnp.minimum(logits, 0.0) - jnp.log1p(jnp.exp(-jnp.abs(logits)))) * (1.0 / LTAU)
        g_hi, g_lo = _split(g)
        pre = _dot(tri, g_hi) + _dot(tri, g_lo)
        tot = jnp.concatenate(
            [jnp.broadcast_to(pre[c * LC + LC - 1:c * LC + LC, :], (LC, 2 * LDK)) for c in range(SBC)], axis=0)
        q = q_ref[rows, :].astype(F32) * scale
        k = k_ref[rows, :].astype(F32)
        v = v_ref[rows, :]
        s_sum = None
        kts = []
        for d in range(2):
            cols = slice(d * LDK, (d + 1) * LDK)
            t_d = tot[:, cols]
            b_d = pre[:, cols] if d == 0 else t_d - pre[:, cols] + g[:, cols]
            q_dec = (q * jnp.exp(b_d)).astype(BF16)
            k_inv = (k * jnp.exp(-b_d)).astype(BF16)
            k_tail = (k * jnp.exp(t_d - b_d)).astype(BF16)
            sc = jnp.where(keep_f if d == 0 else keep_b, _dot_t(q_dec, k_inv), 0.0)
            s_sum = sc if s_sum is None else s_sum + sc
            qd_ref[rows, cols] = q_dec
            zero = jnp.zeros_like(k_tail)
            kts += [jnp.where(rchunk == c, k_tail, zero) for c in range(SBC)]
            for c in range(SBC):
                gam_ref[d, sb * SBC + c] = jnp.exp(t_d[c * LC:c * LC + 1, :])
        oacc_ref[rows, :] = _dot(s_sum.astype(BF16), v)
        dst = _dot_lt(v, jnp.concatenate(kts, axis=1))
        for d in range(2):
            for c in range(SBC):
                j = d * SBC + c
                ds_ref[d, sb * SBC + c] = dst[:, j * LDK:(j + 1) * LDK]
        return carry

    lax.fori_loop(0, NSB, phase1, 0, unroll=3)

    sf_ref[...] = jnp.zeros_like(sf_ref)
    sb_ref[...] = jnp.zeros_like(sb_ref)

    def phase2(i, carry):
        nb = jnp.where(i < NCTXCH, NCTXCH - 1 - i, NCH + NCTXCH - 1 - i)
        s_f = sf_ref[...]
        s_b = sb_ref[...]
        sall_ref[i, :, 0:LDK] = s_f.astype(BF16)
        sall_ref[nb, :, LDK:2 * LDK] = s_b.astype(BF16)
        sf_ref[...] = s_f * gam_ref[0, i] + ds_ref[0, i]
        sb_ref[...] = s_b * gam_ref[1, nb] + ds_ref[1, nb]
        return carry

    lax.fori_loop(0, NCH, phase2, 0)

    def phase3(n, carry):
        rows = pl.ds(pl.multiple_of(n * LC, LC), LC)
        o_ref[rows, :] = (oacc_ref[rows, :] + _dot_t(qd_ref[rows, :], sall_ref[n])).astype(BF16)
        return carry

    lax.fori_loop(0, NCH, phase3, 0, unroll=4)


def _gla(pm, lr, wa_cat, ba_cat, layer):
    return pl.pallas_call(
        _gla_kernel,
        grid=(NB, LH),
        in_specs=[pl.BlockSpec((LB, LDK), lambda b, h: (b, C_Q // LDK + h)),
                  pl.BlockSpec((LB, LDK), lambda b, h: (b, C_K // LDK + h)),
                  pl.BlockSpec((LB, LDV), lambda b, h: (b, C_V // LDV + h)),
                  pl.BlockSpec((LB, LRW), lambda b, h: (b, 0)),
                  pl.BlockSpec((None, None, LRW, 2 * LDK), lambda b, h: (layer, h, 0, 0)),
                  pl.BlockSpec((None, None, 1, 2 * LDK), lambda b, h: (layer, h, 0, 0))],
        out_specs=pl.BlockSpec((LB, LDV), lambda b, h: (b, h)),
        out_shape=jax.ShapeDtypeStruct((P, LVD), BF16),
        scratch_shapes=[pltpu.VMEM((LB, 2 * LDK), BF16), pltpu.VMEM((LB, LDV), F32),
                        pltpu.VMEM((2, NCH, LDV, LDK), F32), pltpu.VMEM((2, NCH, 1, LDK), F32),
                        pltpu.VMEM((NCH, LDV, 2 * LDK), BF16),
                        pltpu.VMEM((LDV, LDK), F32), pltpu.VMEM((LDV, LDK), F32)],
        compiler_params=_cp(("arbitrary", "arbitrary")),
        name="gla",
    )(pm, pm, pm, lr, wa_cat, ba_cat)


def _merge_kernel(yf_ref, zu_ref, zv_ref, o_ref, r_ref, g0_ref, g1_ref, g2_ref, h_ref,
                  gt1_ref, sh2_ref, sc2_ref, n2g_ref, gng_ref, ws_ref, bs_ref, lng_ref,
                  pf_ref, pg_ref, pl_ref, wo_ref, wrh_ref, wrl_ref, br_ref,
                  hmid_ref, xm_ref, lg_ref):
    y = _sigmoid(g0_ref[...].astype(F32)) * _dot(yf_ref[...], pf_ref[...])

    u = _gelu_tanh(zu_ref[...].astype(F32))
    v = _rms(_gelu_tanh(zv_ref[...].astype(F32)), gng_ref[...]).astype(BF16)
    chunks = []
    for ch in range(TM // GCH):
        rows = slice(ch * GCH, (ch + 1) * GCH)
        heads = [_dot(ws_ref[hd], v[rows, hd * GHD:(hd + 1) * GHD]) + bs_ref[hd] for hd in range(GH)]
        chunks.append(jnp.concatenate(heads, axis=1))
    s = jnp.concatenate(chunks, axis=0)
    y += _sigmoid(g1_ref[...].astype(F32)) * _dot((u * s).astype(BF16), pg_ref[...])

    o = o_ref[...].astype(F32)
    lng = lng_ref[...]
    heads = [_rms(o[:, hd * LDV:(hd + 1) * LDV], lng[:, hd * LDV:(hd + 1) * LDV]) for hd in range(LH)]
    ol = (jnp.concatenate(heads, axis=1) * _silu(r_ref[...].astype(F32))).astype(BF16)
    y += _sigmoid(g2_ref[...].astype(F32)) * _dot(ol, pl_ref[...])

    hmid = h_ref[...] + gt1_ref[0] * _dot(y.astype(BF16), wo_ref[...])
    hmid_ref[...] = hmid

    xm = _rms(hmid, n2g_ref[0]) * (1.0 + sc2_ref[0]) + sh2_ref[0]
    xm_ref[...] = xm
    xm_hi, xm_lo = _split(xm)
    lg_ref[...] = (_dot(xm_hi, wrh_ref[...]) + _dot(xm_hi, wrl_ref[...]) + _dot(xm_lo, wrh_ref[...])
                   + br_ref[...])


def _merge(layer, yf, pm, o, h, mods3, norm2_g3, gng, ws, bsb, lng, pf, pg, plw, wo, wrh, wrl, br):
    row = _tile_row
    tile = lambda w, c: pl.BlockSpec((TM, w), lambda t: (t, c))
    lay3 = lambda a, b: pl.BlockSpec((None, a, b), lambda t: (layer, 0, 0), pipeline_mode=pl.Buffered(1))
    lay4 = lambda a, b, c: pl.BlockSpec((None, a, b, c), lambda t: (layer, 0, 0, 0))
    return pl.pallas_call(
        _merge_kernel,
        grid=(NT,),
        in_specs=[tile(FDIM, 0), tile(GDIM, C_ZU // GDIM), tile(GDIM, C_ZV // GDIM), tile(LVD, 0),
                  tile(LVD, C_R // LVD), tile(D, 0), tile(D, 1), tile(D, 2), tile(D, 0),
                  _mod_spec(layer, 2, row), _mod_spec(layer, 3, row), _mod_spec(layer, 4, row),
                  pl.BlockSpec((1, 1, D), lambda t: (layer, 0, 0)),
                  lay3(1, GDIM), lay4(GH, GCH, GCH), lay4(GH, GCH, GCH), lay3(1, LVD),
                  lay3(FDIM, D), lay3(GDIM, D), lay3(LVD, D), lay3(D, D),
                  lay3(D, RW), lay3(D, RW), lay3(1, RW)],
        out_specs=[tile(D, 0), tile(D, 0), tile(RW, 0)],
        out_shape=[jax.ShapeDtypeStruct((P, D), F32), jax.ShapeDtypeStruct((P, D), F32),
                   jax.ShapeDtypeStruct((P, RW), F32)],
        compiler_params=_cp(("arbitrary",)),
        name="merge",
    )(yf, pm, pm, o, pm, pm, pm, pm, h, mods3, mods3, mods3, norm2_g3, gng, ws, bsb, lng,
      pf, pg, plw, wo, wrh, wrl, br)


def _route_kernel(lg_ref, meta_ref, wts_ref, cnt_ref):
    lg = lg_ref[...]
    lane = lax.broadcasted_iota(I32, (TM, RW), 1)
    lane_f = lane.astype(F32)
    ninf = jnp.float32(-jnp.inf)

    def first_max(x):
        m = jnp.max(x, axis=-1, keepdims=True)
        first = jnp.min(jnp.where(x == m, lane_f, float(RW)), axis=-1, keepdims=True)
        return m, first.astype(I32)

    is_g = (lane >= NEXP) & (lane < NEXP + NGRP)
    gmax, glane = first_max(jnp.where(is_g, lg, ninf))
    gsum = jnp.sum(jnp.where(is_g, jnp.exp(lg - gmax), 0.0), axis=-1, keepdims=True)
    g_w = 1.0 / gsum
    lo = (glane - NEXP) * EPG
    in_grp = (lane >= lo) & (lane < lo + EPG)
    el = jnp.where(in_grp, lg, ninf)
    v1, l1 = first_max(el)
    v2, l2 = first_max(jnp.where(lane == l1, ninf, el))
    e = jnp.exp(v2 - v1)
    w1 = g_w / (1.0 + e)
    w2 = g_w * e / (1.0 + e)

    hit1 = lane == l1
    hit2 = lane == l2
    m = (hit1 | hit2).astype(BF16)
    ri = lax.broadcasted_iota(I32, (TM, TM), 0)
    ci = lax.broadcasted_iota(I32, (TM, TM), 1)
    before = _dot((ci < ri).astype(BF16), m)
    r1 = jnp.sum(jnp.where(hit1, before, 0.0), axis=-1, keepdims=True).astype(I32)
    r2 = jnp.sum(jnp.where(hit2, before, 0.0), axis=-1, keepdims=True).astype(I32)
    total = jnp.sum(m.astype(F32), axis=0, keepdims=True)
    cnt_ref[...] = jnp.broadcast_to(total, cnt_ref.shape).astype(I32)
    meta_ref[...] = jnp.where(lane == 0, l1, jnp.where(lane == 1, l2, jnp.where(lane == 2, r1,
                              jnp.where(lane == 3, r2, 0))))
    wts_ref[...] = jnp.where(lane == 0, w1, jnp.where(lane == 1, w2, 0.0))


def _route(logits):
    return pl.pallas_call(
        _route_kernel,
        grid=(NT,),
        in_specs=[pl.BlockSpec((TM, RW), lambda t: (t, 0))],
        out_specs=[pl.BlockSpec((TM, RW), lambda t: (t, 0)), pl.BlockSpec((TM, RW), lambda t: (t, 0)),
                   pl.BlockSpec((8, RW), lambda t: (t, 0))],
        out_shape=[jax.ShapeDtypeStruct((P, RW), I32), jax.ShapeDtypeStruct((P, RW), F32),
                   jax.ShapeDtypeStruct((NT * 8, RW), I32)],
        compiler_params=_cp(("arbitrary",)),
        name="route",
    )(logits)


def _plan(cnt8):
    cnt = cnt8.reshape(NT, 8, RW)[:, 0, :NEXP]
    c8 = (cnt + (CH - 1)) // CH
    lend = jnp.cumsum(c8, axis=1)
    lstart = lend - c8
    nloc = lend[:, -1]
    reg = jnp.sum(c8, axis=0)
    rpad = (reg + (TCH - 1)) // TCH * TCH
    rend = jnp.cumsum(rpad)
    rstart = rend - rpad
    gbase = rstart[None, :] + jnp.cumsum(c8, axis=0) - c8
    j = jnp.arange(LCH, dtype=I32)
    owner = jnp.sum((lend[:, None, :] <= j[None, :, None]).astype(I32), axis=2)
    sel = (owner[:, :, None] == jnp.arange(NEXP, dtype=I32)[None, None, :]).astype(I32)
    dch = jnp.sum(sel * (gbase - lstart)[:, None, :], axis=2) + j[None, :]
    n_act = rend[-1] // TCH
    tid = jnp.minimum(jnp.arange(NTILE, dtype=I32), n_act - 1)
    tile_expert = jnp.minimum(jnp.sum((rend[None, :] <= (tid * TCH)[:, None]).astype(I32), axis=1), NEXP - 1)
    lrow = jnp.pad((lstart * CH).astype(F32), ((0, 0), (0, RW - NEXP))).reshape(NT, 1, RW)
    return dict(nloc=nloc.astype(I32), dch=dch.astype(I32).reshape(NT, 1, LCH), lrow=lrow,
                pstart=(rstart + reg).astype(I32), npad=(rpad - reg).astype(I32),
                tile_expert=tile_expert.astype(I32), n_act=n_act.reshape(1).astype(I32))


def _local_pos(meta, lrow):
    lane = lax.broadcasted_iota(I32, (TM, RW), 1)
    meta_f = meta.astype(F32)
    col = lambda k: jnp.sum(jnp.where(lane == k, meta_f, 0.0), axis=-1, keepdims=True).astype(I32)
    l1, l2, r1, r2 = col(0), col(1), col(2), col(3)
    off = lambda l: jnp.sum(jnp.where(lane == l, lrow, 0.0), axis=-1, keepdims=True).astype(I32)
    return off(l1) + r1, off(l2) + r2


def _dispatch_kernel(nloc_ref, pstart_ref, npad_ref, na_ref, dch_ref, x_ref, meta_ref, lrow_ref, xs_hbm,
                     xloc, zbuf, sem, zsem):
    t = pl.program_id(0)
    slot = t % 2

    def chunk_copy(buf_slot, j, g):
        return pltpu.make_async_copy(xloc.at[buf_slot, pl.ds(pl.multiple_of(j * CH, CH), CH)],
                                     xs_hbm.at[pl.ds(pl.multiple_of(g * CH, CH), CH)], sem.at[buf_slot])

    @pl.when(t == 0)
    def _():
        zbuf[...] = jnp.zeros_like(zbuf)

        def zero_chunks(first, n):
            zcopy = lambda c: pltpu.make_async_copy(
                zbuf, xs_hbm.at[pl.ds(pl.multiple_of((first + c) * CH, CH), CH)], zsem.at[0])
            lax.fori_loop(0, n, lambda c, z: (zcopy(c).start(), z)[1], 0)
            lax.fori_loop(0, n, lambda c, z: (zcopy(0).wait(), z)[1], 0)

        for e in range(NEXP):
            zero_chunks(pstart_ref[e], npad_ref[e])
        zero_chunks(na_ref[0] * TCH, NTILE * TCH - na_ref[0] * TCH)

    p1, p2 = _local_pos(meta_ref[...], lrow_ref[0])
    pos = lax.broadcasted_iota(I32, (TM, LROWS), 1)
    sel = ((pos == p1) | (pos == p2)).astype(BF16)
    xloc[slot] = _dot_lt(sel, x_ref[...].astype(BF16))

    @pl.when(t > 0)
    def _():
        lax.fori_loop(0, nloc_ref[t - 1], lambda j, z: (chunk_copy(1 - slot, 0, 0).wait(), z)[1], 0)

    lax.fori_loop(0, nloc_ref[t], lambda j, z: (chunk_copy(slot, j, dch_ref[0, 0, j]).start(), z)[1], 0)

    @pl.when(t == NT - 1)
    def _():
        lax.fori_loop(0, nloc_ref[t], lambda j, z: (chunk_copy(slot, 0, 0).wait(), z)[1], 0)


def _dispatch(plan, xm, meta):
    grid_spec = pltpu.PrefetchScalarGridSpec(
        num_scalar_prefetch=4,
        grid=(NT,),
        in_specs=[pl.BlockSpec((1, 1, LCH), lambda t, *_: (t, 0, 0), memory_space=pltpu.SMEM),
                  pl.BlockSpec((TM, D), lambda t, *_: (t, 0)),
                  pl.BlockSpec((TM, RW), lambda t, *_: (t, 0)),
                  pl.BlockSpec((1, 1, RW), lambda t, *_: (t, 0, 0))],
        out_specs=pl.BlockSpec(memory_space=pl.ANY),
        scratch_shapes=[pltpu.VMEM((2, LROWS, D), F32), pltpu.VMEM((CH, D), F32),
                        pltpu.SemaphoreType.DMA((2,)), pltpu.SemaphoreType.DMA((1,))],
    )
    return pl.pallas_call(
        _dispatch_kernel,
        grid_spec=grid_spec,
        out_shape=jax.ShapeDtypeStruct((NSLOT, D), F32),
        compiler_params=_cp(("arbitrary",)),
        name="dispatch",
    )(plan["nloc"], plan["pstart"], plan["npad"], plan["n_act"], plan["dch"], xm, meta, plan["lrow"])


def _experts_kernel(te_ref, na_ref, x_ref, wg_ref, wu_ref, wd_ref, y_ref, wgb, wub, wdb):
    i = pl.program_id(0)

    @pl.when(i < na_ref[0])
    def _():
        first = jnp.logical_or(i == 0, te_ref[i] != te_ref[jnp.maximum(i - 1, 0)])

        @pl.when(first)
        def _():
            wgb[...] = wg_ref[0].astype(BF16)
            wub[...] = wu_ref[0].astype(BF16)
            wdb[...] = wd_ref[0].astype(BF16)

        x = x_ref[...].astype(BF16)
        act = (_silu(_dot(x, wgb[...])) * _dot(x, wub[...])).astype(BF16)
        y_ref[...] = _dot(act, wdb[...])

    @pl.when(i >= na_ref[0])
    def _():
        y_ref[...] = jnp.zeros_like(y_ref)


def _experts(plan, xs, wg, wu, wd, layer):
    wspec = lambda a, b: pl.BlockSpec((None, 1, a, b), lambda i, te, na: (layer, te[i], 0, 0))
    grid_spec = pltpu.PrefetchScalarGridSpec(
        num_scalar_prefetch=2,
        grid=(NTILE,),
        in_specs=[pl.BlockSpec((TR, D), lambda i, te, na: (jnp.minimum(i, na[0] - 1), 0)),
                  wspec(D, DEXP), wspec(D, DEXP), wspec(DEXP, D)],
        out_specs=pl.BlockSpec((TR, D), lambda i, te, na: (i, 0)),
        scratch_shapes=[pltpu.VMEM((D, DEXP), BF16), pltpu.VMEM((D, DEXP), BF16), pltpu.VMEM((DEXP, D), BF16)],
    )
    return pl.pallas_call(
        _experts_kernel,
        grid_spec=grid_spec,
        out_shape=jax.ShapeDtypeStruct((NSLOT, D), F32),
        compiler_params=_cp(("arbitrary",)),
        name="experts",
    )(plan["tile_expert"], plan["n_act"], xs, wg, wu, wd)


def _combine_kernel(last, nloc_ref, dch_ref, dchn_ref, meta_ref, w_ref, lrow_ref, h_ref, gt2_ref, g_ref,
                    sh_ref, sc_ref, ys_hbm, *rest):
    if last:
        out_ref, yloc, sem = rest
    else:
        h_out, xn_out, yloc, sem = rest
    t = pl.program_id(0)
    slot = t % 2

    def fetch(idx_ref, buf_slot, n):
        def body(j, z):
            g = idx_ref[0, 0, j]
            pltpu.make_async_copy(ys_hbm.at[pl.ds(pl.multiple_of(g * CH, CH), CH)],
                                  yloc.at[buf_slot, pl.ds(pl.multiple_of(j * CH, CH), CH)],
                                  sem.at[buf_slot]).start()
            return z
        lax.fori_loop(0, n, body, 0)

    @pl.when(t == 0)
    def _():
        yloc[...] = jnp.zeros_like(yloc)
        fetch(dch_ref, 0, nloc_ref[0])

    @pl.when(t + 1 < NT)
    def _():
        fetch(dchn_ref, 1 - slot, nloc_ref[jnp.minimum(t + 1, NT - 1)])

    def wait_one(j, z):
        pltpu.make_async_copy(ys_hbm.at[pl.ds(0, CH)], yloc.at[slot, pl.ds(0, CH)], sem.at[slot]).wait()
        return z
    lax.fori_loop(0, nloc_ref[t], wait_one, 0)

    p1, p2 = _local_pos(meta_ref[...], lrow_ref[0])
    pos = lax.broadcasted_iota(I32, (TM, LROWS), 1)
    y = yloc[slot].astype(BF16)
    w = w_ref[...]
    moe = (w[:, 0:1] * _dot((pos == p1).astype(BF16), y) + w[:, 1:2] * _dot((pos == p2).astype(BF16), y))
    h = h_ref[...] + gt2_ref[0] * moe
    if last:
        @pl.when(t % TPB > 0)
        def _():
            out_ref[0] = _rms(h, g_ref[0])
    else:
        h_out[...] = h
        xn_out[...] = (_rms(h, g_ref[0]) * (1.0 + sc_ref[0]) + sh_ref[0]).astype(BF16)


def _combine(layer, plan, ys, meta, wts, hmid, mods3, norm_g3):
    last = layer == NLAYER - 1
    row = lambda t, *_: _tile_row(t)
    nxt = 0 if last else layer + 1
    tile = lambda w: pl.BlockSpec((TM, w), lambda t, *_: (t, 0))
    if last:
        out_specs = pl.BlockSpec((1, TM, D), lambda t, *_: (t // TPB, jnp.maximum(t % TPB - 1, 0), 0))
        out_shape = jax.ShapeDtypeStruct((NB, SEQ, D), F32)
    else:
        out_specs = [tile(D), tile(D)]
        out_shape = [jax.ShapeDtypeStruct((P, D), F32), jax.ShapeDtypeStruct((P, D), BF16)]
    grid_spec = pltpu.PrefetchScalarGridSpec(
        num_scalar_prefetch=1,
        grid=(NT,),
        in_specs=[pl.BlockSpec((1, 1, LCH), lambda t, *_: (t, 0, 0), memory_space=pltpu.SMEM),
                  pl.BlockSpec((1, 1, LCH), lambda t, *_: (jnp.minimum(t + 1, NT - 1), 0, 0),
                               memory_space=pltpu.SMEM),
                  tile(RW), tile(RW), pl.BlockSpec((1, 1, RW), lambda t, *_: (t, 0, 0)), tile(D),
                  _mod_spec(layer, 5, row),
                  pl.BlockSpec((1, 1, D), lambda t, *_: (NLAYER if last else nxt, 0, 0)),
                  _mod_spec(nxt, 0, row), _mod_spec(nxt, 1, row),
                  pl.BlockSpec(memory_space=pl.ANY)],
        out_specs=out_specs,
        scratch_shapes=[pltpu.VMEM((2, LROWS, D), F32), pltpu.SemaphoreType.DMA((2,))],
    )
    return pl.pallas_call(
        functools.partial(_combine_kernel, last),
        grid_spec=grid_spec,
        out_shape=out_shape,
        compiler_params=_cp(("arbitrary",)),
        name="final" if last else "combine",
    )(plan["nloc"], plan["dch"], plan["dch"], meta, wts, plan["lrow"], hmid, mods3, norm_g3, mods3, mods3, ys)


def kernel(x, c, ctx, c_ctx, w_mod, b_mod, norm1_g, norm2_g, w_in, p_fourier, gmlp_norm_g, gmlp_ws,
           gmlp_bs, p_gmlp, gla_w_a2, gla_b_a, gla_norm_g, p_gla, w_out, router_group_w, router_group_b,
           router_expert_w, router_expert_b, expert_w_gate, expert_w_up, expert_w_down, final_norm_g):
    cvec = jnp.concatenate([c, c_ctx[None, :], jnp.zeros((3, D), F32)], axis=0)
    a_, z_, q_, k_, v_, r_, lr_, gt_ = (w_in[..., 0:512], w_in[..., 512:1536], w_in[..., 1536:2048],
                                        w_in[..., 2048:2560], w_in[..., 2560:3584], w_in[..., 3584:4608],
                                        w_in[..., 4608:4640], w_in[..., 4640:])
    w_main = jnp.concatenate([gt_, v_, r_, z_, a_, q_, k_], axis=-1).astype(BF16)
    w_lr = jnp.pad(lr_, ((0, 0), (0, 0), (0, LRW - 2 * LRANK))).astype(BF16)
    wa_pad = jnp.zeros((NLAYER, 2, LRW, LKD), F32)
    wa_pad = wa_pad.at[:, 0, 0:LRANK].set(gla_w_a2[:, 0]).at[:, 1, LRANK:2 * LRANK].set(gla_w_a2[:, 1])
    wa_cat = wa_pad.reshape(NLAYER, 2, LRW, LH, LDK).transpose(0, 3, 2, 1, 4).reshape(NLAYER, LH, LRW, 2 * LDK)
    wa_cat = wa_cat.astype(BF16)
    ba_cat = gla_b_a.reshape(NLAYER, 2, LH, LDK).transpose(0, 2, 1, 3).reshape(NLAYER, LH, 1, 2 * LDK)
    norm1_g3 = jnp.concatenate([norm1_g, final_norm_g[None, :]], axis=0).reshape(NLAYER + 1, 1, D)
    norm2_g3 = norm2_g.reshape(NLAYER, 1, D)
    gng = gmlp_norm_g.reshape(NLAYER, 1, GDIM)
    ws = gmlp_ws.astype(BF16)
    bsb = jnp.broadcast_to(gmlp_bs[:, :, :, None], (NLAYER, GH, GCH, GCH))
    lng = gla_norm_g.reshape(NLAYER, 1, LVD)
    pf, pg, plw, wo = (p_fourier.astype(BF16), p_gmlp.astype(BF16), p_gla.astype(BF16), w_out.astype(BF16))
    w_r = jnp.pad(jnp.concatenate([router_expert_w, router_group_w], axis=-1),
                  ((0, 0), (0, 0), (0, RW - NEXP - NGRP)))
    wrh = w_r.astype(BF16)
    wrl = (w_r - wrh.astype(F32)).astype(BF16)
    br = jnp.pad(jnp.concatenate([router_expert_b, router_group_b], axis=-1),
                 ((0, 0), (0, RW - NEXP - NGRP))).reshape(NLAYER, 1, RW)
    cs_lat, cs_ctx, cc = _dft_consts()

    mods3 = _mods(cvec, w_mod, b_mod).reshape(NLAYER * 8 * 6, 1, D)

    h, xn = _init(x, ctx, norm1_g3, mods3)
    res = None
    for layer in range(NLAYER):
        pm = _matmul(xn, w_main, layer, 1024, 1536, BF16)
        lr = _matmul(xn, w_lr, layer, 1024, LRW, F32)
        o = _gla(pm, lr, wa_cat, ba_cat, layer)
        yf = _fourier(pm, cs_lat, cs_ctx, cc)
        hmid, xm, logits = _merge(layer, yf, pm, o, h, mods3, norm2_g3, gng, ws, bsb, lng,
                                  pf, pg, plw, wo, wrh, wrl, br)
        meta, wts, cnt8 = _route(logits)
        plan = _plan(cnt8)
        xs = _dispatch(plan, xm, meta)
        ys = _experts(plan, xs, expert_w_gate, expert_w_up, expert_w_down, layer)
        res = _combine(layer, plan, ys, meta, wts, hmid, mods3, norm1_g3)
        if layer + 1 < NLAYER:
            h, xn = res
    return res
```

```python
import functools
import math

import numpy as np
import jax
import jax.numpy as jnp
from jax import lax
from jax.experimental import pallas as pl
from jax.experimental.pallas import tpu as pltpu

F32 = jnp.float32
BF16 = jnp.bfloat16
I32 = jnp.int32

D = 2048
NB = 4
SEQ = 2048
NLAYER = 4
CTX = 256
EPS = 1e-6
LB = CTX + SEQ
P = NB * LB
TM = 256
TPB = LB // TM
NT = P // TM

FG, FGD = 4, 128
FDIM = FG * FGD
GH, GHD, GCH = 4, 128, 128
GDIM = GH * GHD
LH, LDK, LDV, LRANK, LTAU, LC = 4, 128, 256, 16, 16.0, 64
LKD, LVD = LH * LDK, LH * LDV
NCH = LB // LC
NCTXCH = CTX // LC

C_G0, C_V, C_R, C_ZU, C_ZV, C_A, C_Q, C_K = 0, 6144, 7168, 8192, 8704, 9216, 9728, 10240
NMAIN = 10752
LRW = 128

NGRP, EPG, NEXP, DEXP = 4, 8, 32, 512
TR = 256
CH = 8
LCH = (2 * TM + NEXP * (CH - 1)) // CH
LROWS = 768
TCH = TR // CH
NCHUNK = 2 * P // CH + NT * NEXP * (CH - 1) // CH + NEXP * (TCH - 1)
NTILE = NCHUNK // TCH + 1
NSLOT = NTILE * TR
RW = 128

VMEM_LIMIT = 56 * 1024 * 1024


def _cp(sem, vmem=VMEM_LIMIT):
    return pltpu.CompilerParams(dimension_semantics=sem, vmem_limit_bytes=vmem)


def _dot(a, b):
    return jnp.dot(a, b, preferred_element_type=F32)


def _dot_t(a, b):
    return lax.dot_general(a, b, (((1,), (1,)), ((), ())), preferred_element_type=F32)


def _dot_lt(a, b):
    return lax.dot_general(a, b, (((0,), (0,)), ((), ())), preferred_element_type=F32)


def _split(x):
    hi = x.astype(BF16)
    lo = (x - hi.astype(F32)).astype(BF16)
    return hi, lo


U32 = jnp.uint32
DH = D // 2


def _pack_bf16_pair(x):
    lo = lax.bitcast_convert_type(x[:, :DH], U32)
    hi = lax.bitcast_convert_type(x[:, DH:], U32)
    return lax.shift_right_logical(lo, jnp.uint32(16)) | (hi & jnp.uint32(0xFFFF0000))


def _unpack_bf16_pair(u):
    lo = lax.bitcast_convert_type(lax.shift_left(u, jnp.uint32(16)), F32)
    hi = lax.bitcast_convert_type(u & jnp.uint32(0xFFFF0000), F32)
    return lo.astype(BF16), hi.astype(BF16)


def _sigmoid(x):
    return 1.0 / (1.0 + jnp.exp(-x))


def _silu(x):
    return x * _sigmoid(x)


def _gelu_tanh(x):
    return 0.5 * x * (1.0 + jnp.tanh(math.sqrt(2.0 / math.pi) * (x + 0.044715 * (x * x * x))))


def _rms(x, g):
    return x * lax.rsqrt(jnp.mean(x * x, axis=-1, keepdims=True) + EPS) * g


def _tile_row(t):
    return jnp.where(t % TPB == 0, 4, t // TPB)


def _mod_spec(layer, comp, row_fn):
    return pl.BlockSpec((1, 1, D), lambda *g: ((layer * 8 + row_fn(*g)) * 6 + comp, 0, 0))


def _mods_kernel(c_ref, w_ref, b_ref, o_ref):
    c = c_ref[...]
    s = _silu(c).astype(BF16)
    o_ref[0] = _dot(s, w_ref[0].astype(BF16)) + b_ref[0]


def _mods(cvec, w_mod, b_mod):
    tn = 1024
    return pl.pallas_call(
        _mods_kernel,
        grid=(NLAYER, 6 * D // tn),
        in_specs=[pl.BlockSpec((8, D), lambda l, j: (0, 0)),
                  pl.BlockSpec((1, D, tn), lambda l, j: (l, 0, j)),
                  pl.BlockSpec((1, 1, tn), lambda l, j: (l, 0, j))],
        out_specs=pl.BlockSpec((1, 8, tn), lambda l, j: (l, 0, j)),
        out_shape=jax.ShapeDtypeStruct((NLAYER, 8, 6 * D), F32),
        compiler_params=_cp(("arbitrary", "arbitrary")),
        name="mods",
    )(cvec, w_mod, b_mod.reshape(NLAYER, 1, 6 * D))


def _init_kernel(x_ref, c_ref, g_ref, sh_ref, sc_ref, h_ref, xn_ref):
    j = pl.program_id(1)

    def emit(v):
        h_ref[...] = v
        xn_ref[...] = (_rms(v, g_ref[0]) * (1.0 + sc_ref[0]) + sh_ref[0]).astype(BF16)

    @pl.when(j == 0)
    def _():
        emit(c_ref[0])

    @pl.when(j > 0)
    def _():
        emit(x_ref[0])


def _init(x, ctx, norm1_g3, mods3):
    row = lambda b, j: jnp.where(j == 0, 4, b)
    return pl.pallas_call(
        _init_kernel,
        grid=(NB, TPB),
        in_specs=[pl.BlockSpec((1, TM, D), lambda b, j: (b, jnp.maximum(j - 1, 0), 0)),
                  pl.BlockSpec((1, CTX, D), lambda b, j: (b, 0, 0)),
                  pl.BlockSpec((1, 1, D), lambda b, j: (0, 0, 0)),
                  _mod_spec(0, 0, row), _mod_spec(0, 1, row)],
        out_specs=[pl.BlockSpec((TM, D), lambda b, j: (b * TPB + j, 0)),
                   pl.BlockSpec((TM, D), lambda b, j: (b * TPB + j, 0))],
        out_shape=[jax.ShapeDtypeStruct((P, D), F32), jax.ShapeDtypeStruct((P, D), BF16)],
        compiler_params=_cp(("arbitrary", "arbitrary")),
        name="init",
    )(x, ctx, norm1_g3, mods3, mods3)


def _mm_kernel(x_ref, w_ref, o_ref):
    o_ref[...] = _dot(x_ref[...], w_ref[0]).astype(o_ref.dtype)


def _matmul(x, w_all, layer, tm, tn, out_dtype):
    m, k = x.shape
    n = w_all.shape[2]
    return pl.pallas_call(
        _mm_kernel,
        grid=(n // tn, m // tm),
        in_specs=[pl.BlockSpec((tm, k), lambda j, i: (i, 0)),
                  pl.BlockSpec((1, k, tn), lambda j, i: (layer, 0, j))],
        out_specs=pl.BlockSpec((tm, tn), lambda j, i: (i, j)),
        out_shape=jax.ShapeDtypeStruct((m, n), out_dtype),
        compiler_params=_cp(("arbitrary", "arbitrary")),
        name="inproj",
    )(x, w_all)


def _dft_consts():
    def cs(n):
        k = np.arange(n, dtype=np.int64)
        ang = 2.0 * np.pi * ((k[:, None] * k[None, :]) % n).astype(np.float64) / n
        return np.cos(ang) / math.sqrt(n), np.sin(ang) / math.sqrt(n)

    c_l, s_l = cs(SEQ)
    c_c, s_c = cs(CTX)
    c_g, s_g = cs(FGD)
    cs_lat = np.concatenate([c_l, -s_l], axis=1).astype(BF16)
    cs_ctx = np.concatenate([c_c, -s_c], axis=1).astype(BF16)
    cc = np.concatenate([c_g, s_g], axis=1).astype(BF16)
    return cs_lat, cs_ctx, cc


def _fourier_kernel(a_ref, csl_ref, csc_ref, cc_ref, o_ref, rl_ref, rc_ref):
    j = pl.program_id(1)

    @pl.when(j == 0)
    def _():
        for g in range(FG):
            cols = slice(g * FGD, (g + 1) * FGD)
            t = _dot(a_ref[:, cols], cc_ref[...]).astype(BF16)
            rc_ref[0:CTX, cols] = t[0:CTX, 0:FGD]
            rc_ref[CTX:2 * CTX, cols] = t[0:CTX, FGD:2 * FGD]
            rl_ref[0:SEQ, cols] = t[CTX:LB, 0:FGD]
            rl_ref[SEQ:2 * SEQ, cols] = t[CTX:LB, FGD:2 * FGD]
        o_ref[...] = _dot(csc_ref[...], rc_ref[...]).astype(BF16)

    @pl.when(j > 0)
    def _():
        o_ref[...] = _dot(csl_ref[...], rl_ref[...]).astype(BF16)


def _fourier(pm, cs_lat, cs_ctx, cc):
    return pl.pallas_call(
        _fourier_kernel,
        grid=(NB, TPB),
        in_specs=[pl.BlockSpec((LB, FDIM), lambda b, j: (b, C_A // FDIM)),
                  pl.BlockSpec((TM, 2 * SEQ), lambda b, j: (jnp.maximum(j - 1, 0), 0)),
                  pl.BlockSpec((CTX, 2 * CTX), lambda b, j: (0, 0)),
                  pl.BlockSpec((FGD, 2 * FGD), lambda b, j: (0, 0))],
        out_specs=pl.BlockSpec((TM, FDIM), lambda b, j: (b * TPB + j, 0)),
        out_shape=jax.ShapeDtypeStruct((P, FDIM), BF16),
        scratch_shapes=[pltpu.VMEM((2 * SEQ, FDIM), BF16), pltpu.VMEM((2 * CTX, FDIM), BF16)],
        compiler_params=_cp(("arbitrary", "arbitrary")),
        name="fourier",
    )(pm, cs_lat, cs_ctx, cc)


SB = 256
SBC = SB // LC
NSB = LB // SB


def _gla_kernel(q_ref, k_ref, v_ref, lr_ref, wa_ref, ba_ref, o_ref,
                qd_ref, oacc_ref, ds_ref, gam_ref, sall_ref, sf_ref, sb_ref):
    ri = lax.broadcasted_iota(I32, (SB, SB), 0)
    ci = lax.broadcasted_iota(I32, (SB, SB), 1)
    same = (ri // LC) == (ci // LC)
    tri = (same & (ci <= ri)).astype(BF16)
    keep_f = same & (ci <= ri)
    keep_b = same & (ci > ri)
    rchunk = lax.broadcasted_iota(I32, (SB, LDK), 0) // LC
    scale = LDK ** -0.5
    wa = wa_ref[...]
    ba = ba_ref[...]

    def phase1(sb, carry):
        rows = pl.ds(pl.multiple_of(sb * SB, SB), SB)
        logits = _dot(lr_ref[rows, :].astype(BF16), wa) + ba
        g = (jnp.minimum(logits, 0.0) - jnp.log1p(jnp.exp(-jnp.abs(logits)))) * (1.0 / LTAU)
        g_hi, g_lo = _split(g)
        pre = _dot(tri, g_hi) + _dot(tri, g_lo)
        tot = jnp.concatenate(
            [jnp.broadcast_to(pre[c * LC + LC - 1:c * LC + LC, :], (LC, 2 * LDK)) for c in range(SBC)], axis=0)
        q = q_ref[rows, :].astype(F32) * scale
        k = k_ref[rows, :].astype(F32)
        v = v_ref[rows, :]
        s_sum = None
        kts = []
        for d in range(2):
            cols = slice(d * LDK, (d + 1) * LDK)
            t_d = tot[:, cols]
            b_d = pre[:, cols] if d == 0 else t_d - pre[:, cols] + g[:, cols]
            q_dec = (q * jnp.exp(b_d)).astype(BF16)
            k_inv = (k * jnp.exp(-b_d)).astype(BF16)
            k_tail = (k * jnp.exp(t_d - b_d)).astype(BF16)
            sc = jnp.where(keep_f if d == 0 else keep_b, _dot_t(q_dec, k_inv), 0.0)
            s_sum = sc if s_sum is None else s_sum + sc
            qd_ref[rows, cols] = q_dec
            zero = jnp.zeros_like(k_tail)
            kts += [jnp.where(rchunk == c, k_tail, zero) for c in range(SBC)]
            for c in range(SBC):
                gam_ref[d, sb * SBC + c] = jnp.exp(t_d[c * LC:c * LC + 1, :])
        oacc_ref[rows, :] = _dot(s_sum.astype(BF16), v)
        dst = _dot_lt(v, jnp.concatenate(kts, axis=1))
        for d in range(2):
            for c in range(SBC):
                j = d * SBC + c
                ds_ref[d, sb * SBC + c] = dst[:, j * LDK:(j + 1) * LDK]
        return carry

    lax.fori_loop(0, NSB, phase1, 0, unroll=3)

    sf_ref[...] = jnp.zeros_like(sf_ref)
    sb_ref[...] = jnp.zeros_like(sb_ref)

    def phase2(i, carry):
        nb = jnp.where(i < NCTXCH, NCTXCH - 1 - i, NCH + NCTXCH - 1 - i)
        s_f = sf_ref[...]
        s_b = sb_ref[...]
        sall_ref[i, :, 0:LDK] = s_f.astype(BF16)
        sall_ref[nb, :, LDK:2 * LDK] = s_b.astype(BF16)
        sf_ref[...] = s_f * gam_ref[0, i] + ds_ref[0, i]
        sb_ref[...] = s_b * gam_ref[1, nb] + ds_ref[1, nb]
        return carry

    lax.fori_loop(0, NCH, phase2, 0)

    def phase3(n, carry):
        rows = pl.ds(pl.multiple_of(n * LC, LC), LC)
        o_ref[rows, :] = (oacc_ref[rows, :] + _dot_t(qd_ref[rows, :], sall_ref[n])).astype(BF16)
        return carry

    lax.fori_loop(0, NCH, phase3, 0, unroll=4)


def _gla(pm, lr, wa_cat, ba_cat, layer):
    return pl.pallas_call(
        _gla_kernel,
        grid=(NB, LH),
        in_specs=[pl.BlockSpec((LB, LDK), lambda b, h: (b, C_Q // LDK + h)),
                  pl.BlockSpec((LB, LDK), lambda b, h: (b, C_K // LDK + h)),
                  pl.BlockSpec((LB, LDV), lambda b, h: (b, C_V // LDV + h)),
                  pl.BlockSpec((LB, LRW), lambda b, h: (b, 0)),
                  pl.BlockSpec((None, None, LRW, 2 * LDK), lambda b, h: (layer, h, 0, 0)),
                  pl.BlockSpec((None, None, 1, 2 * LDK), lambda b, h: (layer, h, 0, 0))],
        out_specs=pl.BlockSpec((LB, LDV), lambda b, h: (b, h)),
        out_shape=jax.ShapeDtypeStruct((P, LVD), BF16),
        scratch_shapes=[pltpu.VMEM((LB, 2 * LDK), BF16), pltpu.VMEM((LB, LDV), F32),
                        pltpu.VMEM((2, NCH, LDV, LDK), F32), pltpu.VMEM((2, NCH, 1, LDK), F32),
                        pltpu.VMEM((NCH, LDV, 2 * LDK), BF16),
                        pltpu.VMEM((LDV, LDK), F32), pltpu.VMEM((LDV, LDK), F32)],
        compiler_params=_cp(("arbitrary", "arbitrary")),
        name="gla",
    )(pm, pm, pm, lr, wa_cat, ba_cat)


def _merge_kernel(yf_ref, zu_ref, zv_ref, o_ref, r_ref, g0_ref, g1_ref, g2_ref, h_ref,
                  gt1_ref, sh2_ref, sc2_ref, n2g_ref, gng_ref, ws_ref, bs_ref, lng_ref,
                  pf_ref, pg_ref, pl_ref, wo_ref, wrc_ref, br_ref,
                  hmid_ref, xm_ref, lg_ref):
    y = _sigmoid(g0_ref[...].astype(F32)) * _dot(yf_ref[...], pf_ref[...])

    u = _gelu_tanh(zu_ref[...].astype(F32))
    v = _rms(_gelu_tanh(zv_ref[...].astype(F32)), gng_ref[...]).astype(BF16)
    chunks = []
    for ch in range(TM // GCH):
        rows = slice(ch * GCH, (ch + 1) * GCH)
        heads = [_dot(ws_ref[hd], v[rows, hd * GHD:(hd + 1) * GHD]) + bs_ref[hd] for hd in range(GH)]
        chunks.append(jnp.concatenate(heads, axis=1))
    s = jnp.concatenate(chunks, axis=0)
    y += _sigmoid(g1_ref[...].astype(F32)) * _dot((u * s).astype(BF16), pg_ref[...])

    o = o_ref[...].astype(F32)
    lng = lng_ref[...]
    heads = [_rms(o[:, hd * LDV:(hd + 1) * LDV], lng[:, hd * LDV:(hd + 1) * LDV]) for hd in range(LH)]
    ol = (jnp.concatenate(heads, axis=1) * _silu(r_ref[...].astype(F32))).astype(BF16)
    y += _sigmoid(g2_ref[...].astype(F32)) * _dot(ol, pl_ref[...])

    hmid = h_ref[...] + gt1_ref[0] * _dot(y.astype(BF16), wo_ref[...])
    hmid_ref[...] = hmid

    xm = _rms(hmid, n2g_ref[0]) * (1.0 + sc2_ref[0]) + sh2_ref[0]
    xm_ref[...] = xm
    xm_hi, xm_lo = _split(xm)
    hh_hl = _dot(xm_hi, wrc_ref[...])
    lg_ref[...] = hh_hl[:, 0:RW] + hh_hl[:, RW:2 * RW] + _dot(xm_lo, wrc_ref[:, 0:RW]) + br_ref[...]


def _merge(layer, yf, pm, o, h, mods3, norm2_g3, gng, ws, bsb, lng, pf, pg, plw, wo, wrc, br):
    row = _tile_row
    tile = lambda w, c: pl.BlockSpec((TM, w), lambda t: (t, c))
    lay3 = lambda a, b: pl.BlockSpec((None, a, b), lambda t: (layer, 0, 0), pipeline_mode=pl.Buffered(1))
    lay4 = lambda a, b, c: pl.BlockSpec((None, a, b, c), lambda t: (layer, 0, 0, 0))
    return pl.pallas_call(
        _merge_kernel,
        grid=(NT,),
        in_specs=[tile(FDIM, 0), tile(GDIM, C_ZU // GDIM), tile(GDIM, C_ZV // GDIM), tile(LVD, 0),
                  tile(LVD, C_R // LVD), tile(D, 0), tile(D, 1), tile(D, 2), tile(D, 0),
                  _mod_spec(layer, 2, row), _mod_spec(layer, 3, row), _mod_spec(layer, 4, row),
                  pl.BlockSpec((1, 1, D), lambda t: (layer, 0, 0)),
                  lay3(1, GDIM), lay4(GH, GCH, GCH), lay4(GH, GCH, GCH), lay3(1, LVD),
                  lay3(FDIM, D), lay3(GDIM, D), lay3(LVD, D), lay3(D, D),
                  lay3(D, 2 * RW), lay3(1, RW)],
        out_specs=[tile(D, 0), tile(D, 0), tile(RW, 0)],
        out_shape=[jax.ShapeDtypeStruct((P, D), F32), jax.ShapeDtypeStruct((P, D), F32),
                   jax.ShapeDtypeStruct((P, RW), F32)],
        compiler_params=_cp(("arbitrary",)),
        name="merge",
    )(yf, pm, pm, o, pm, pm, pm, pm, h, mods3, mods3, mods3, norm2_g3, gng, ws, bsb, lng,
      pf, pg, plw, wo, wrc, br)


def _route_kernel(lg_ref, meta_ref, wts_ref, cnt_ref):
    lg = lg_ref[...]
    lane = lax.broadcasted_iota(I32, (TM, RW), 1)
    lane_f = lane.astype(F32)
    ninf = jnp.float32(-jnp.inf)

    def first_max(x):
        m = jnp.max(x, axis=-1, keepdims=True)
        first = jnp.min(jnp.where(x == m, lane_f, float(RW)), axis=-1, keepdims=True)
        return m, first.astype(I32)

    is_g = (lane >= NEXP) & (lane < NEXP + NGRP)
    gmax, glane = first_max(jnp.where(is_g, lg, ninf))
    gsum = jnp.sum(jnp.where(is_g, jnp.exp(lg - gmax), 0.0), axis=-1, keepdims=True)
    g_w = 1.0 / gsum
    lo = (glane - NEXP) * EPG
    in_grp = (lane >= lo) & (lane < lo + EPG)
    el = jnp.where(in_grp, lg, ninf)
    v1, l1 = first_max(el)
    v2, l2 = first_max(jnp.where(lane == l1, ninf, el))
    e = jnp.exp(v2 - v1)
    w1 = g_w / (1.0 + e)
    w2 = g_w * e / (1.0 + e)

    hit1 = lane == l1
    hit2 = lane == l2
    m = (hit1 | hit2).astype(BF16)
    ri = lax.broadcasted_iota(I32, (TM, TM), 0)
    ci = lax.broadcasted_iota(I32, (TM, TM), 1)
    before = _dot((ci < ri).astype(BF16), m)
    r1 = jnp.sum(jnp.where(hit1, before, 0.0), axis=-1, keepdims=True).astype(I32)
    r2 = jnp.sum(jnp.where(hit2, before, 0.0), axis=-1, keepdims=True).astype(I32)
    total = jnp.sum(m.astype(F32), axis=0, keepdims=True)
    cnt_ref[...] = jnp.broadcast_to(total, cnt_ref.shape).astype(I32)
    meta_ref[...] = jnp.where(lane == 0, l1, jnp.where(lane == 1, l2, jnp.where(lane == 2, r1,
                              jnp.where(lane == 3, r2, 0))))
    wts_ref[...] = jnp.where(lane == 0, w1, jnp.where(lane == 1, w2, 0.0))


def _route(logits):
    return pl.pallas_call(
        _route_kernel,
        grid=(NT,),
        in_specs=[pl.BlockSpec((TM, RW), lambda t: (t, 0))],
        out_specs=[pl.BlockSpec((TM, RW), lambda t: (t, 0)), pl.BlockSpec((TM, RW), lambda t: (t, 0)),
                   pl.BlockSpec((8, RW), lambda t: (t, 0))],
        out_shape=[jax.ShapeDtypeStruct((P, RW), I32), jax.ShapeDtypeStruct((P, RW), F32),
                   jax.ShapeDtypeStruct((NT * 8, RW), I32)],
        compiler_params=_cp(("arbitrary",)),
        name="route",
    )(logits)


def _plan(cnt8):
    cnt = cnt8.reshape(NT, 8, RW)[:, 0, :NEXP]
    c8 = (cnt + (CH - 1)) // CH
    lend = jnp.cumsum(c8, axis=1)
    lstart = lend - c8
    nloc = lend[:, -1]
    reg = jnp.sum(c8, axis=0)
    rpad = (reg + (TCH - 1)) // TCH * TCH
    rend = jnp.cumsum(rpad)
    rstart = rend - rpad
    gbase = rstart[None, :] + jnp.cumsum(c8, axis=0) - c8
    j = jnp.arange(LCH, dtype=I32)
    owner = jnp.sum((lend[:, None, :] <= j[None, :, None]).astype(I32), axis=2)
    sel = (owner[:, :, None] == jnp.arange(NEXP, dtype=I32)[None, None, :]).astype(I32)
    dch = jnp.sum(sel * (gbase - lstart)[:, None, :], axis=2) + j[None, :]
    n_act = rend[-1] // TCH
    tid = jnp.minimum(jnp.arange(NTILE, dtype=I32), n_act - 1)
    tile_expert = jnp.minimum(jnp.sum((rend[None, :] <= (tid * TCH)[:, None]).astype(I32), axis=1), NEXP - 1)
    nxt_tile = jnp.sum((tile_expert[:, None] == jnp.arange(NEXP, dtype=I32)[None, :]).astype(I32)
                       * (rend // TCH)[None, :], axis=1)
    nxt_onehot = (nxt_tile[:, None] == jnp.arange(NTILE, dtype=I32)[None, :]).astype(I32)
    next_expert = jnp.where(nxt_tile < n_act, jnp.sum(nxt_onehot * tile_expert[None, :], axis=1), -1)
    lrow = jnp.pad((lstart * CH).astype(F32), ((0, 0), (0, RW - NEXP))).reshape(NT, 1, RW)
    return dict(nloc=nloc.astype(I32), dch=dch.astype(I32).reshape(NT, 1, LCH), lrow=lrow,
                pstart=(rstart + reg).astype(I32), npad=(rpad - reg).astype(I32),
                tile_expert=tile_expert.astype(I32), next_expert=next_expert.astype(I32),
                n_act=n_act.reshape(1).astype(I32))


def _local_pos(meta, lrow):
    lane = lax.broadcasted_iota(I32, (TM, RW), 1)
    meta_f = meta.astype(F32)
    col = lambda k: jnp.sum(jnp.where(lane == k, meta_f, 0.0), axis=-1, keepdims=True).astype(I32)
    l1, l2, r1, r2 = col(0), col(1), col(2), col(3)
    off = lambda l: jnp.sum(jnp.where(lane == l, lrow, 0.0), axis=-1, keepdims=True).astype(I32)
    return off(l1) + r1, off(l2) + r2


def _dispatch_kernel(nloc_ref, pstart_ref, npad_ref, na_ref, dch_ref, x_ref, meta_ref, lrow_ref, xs_hbm,
                     xloc, zbuf, sem, zsem):
    t = pl.program_id(0)
    slot = t % 2

    def chunk_copy(buf_slot, j, g):
        return pltpu.make_async_copy(xloc.at[buf_slot, pl.ds(pl.multiple_of(j * CH, CH), CH)],
                                     xs_hbm.at[pl.ds(pl.multiple_of(g * CH, CH), CH)], sem.at[buf_slot])

    @pl.when(t == 0)
    def _():
        zbuf[...] = jnp.zeros_like(zbuf)

        def zero_chunks(first, n):
            zcopy = lambda c: pltpu.make_async_copy(
                zbuf, xs_hbm.at[pl.ds(pl.multiple_of((first + c) * CH, CH), CH)], zsem.at[0])
            lax.fori_loop(0, n, lambda c, z: (zcopy(c).start(), z)[1], 0)
            lax.fori_loop(0, n, lambda c, z: (zcopy(0).wait(), z)[1], 0)

        for e in range(NEXP):
            zero_chunks(pstart_ref[e], npad_ref[e])
        zero_chunks(na_ref[0] * TCH, NTILE * TCH - na_ref[0] * TCH)

    p1, p2 = _local_pos(meta_ref[...], lrow_ref[0])
    pos = lax.broadcasted_iota(I32, (TM, LROWS), 1)
    sel = ((pos == p1) | (pos == p2)).astype(BF16)
    xloc[slot] = _pack_bf16_pair(_dot_lt(sel, x_ref[...].astype(BF16)))

    @pl.when(t > 0)
    def _():
        lax.fori_loop(0, nloc_ref[t - 1], lambda j, z: (chunk_copy(1 - slot, 0, 0).wait(), z)[1], 0)

    lax.fori_loop(0, nloc_ref[t], lambda j, z: (chunk_copy(slot, j, dch_ref[0, 0, j]).start(), z)[1], 0)

    @pl.when(t == NT - 1)
    def _():
        lax.fori_loop(0, nloc_ref[t], lambda j, z: (chunk_copy(slot, 0, 0).wait(), z)[1], 0)


def _dispatch(plan, xm, meta):
    grid_spec = pltpu.PrefetchScalarGridSpec(
        num_scalar_prefetch=4,
        grid=(NT,),
        in_specs=[pl.BlockSpec((1, 1, LCH), lambda t, *_: (t, 0, 0), memory_space=pltpu.SMEM),
                  pl.BlockSpec((TM, D), lambda t, *_: (t, 0)),
                  pl.BlockSpec((TM, RW), lambda t, *_: (t, 0)),
                  pl.BlockSpec((1, 1, RW), lambda t, *_: (t, 0, 0))],
        out_specs=pl.BlockSpec(memory_space=pl.ANY),
        scratch_shapes=[pltpu.VMEM((2, LROWS, DH), U32), pltpu.VMEM((CH, DH), U32),
                        pltpu.SemaphoreType.DMA((2,)), pltpu.SemaphoreType.DMA((1,))],
    )
    return pl.pallas_call(
        _dispatch_kernel,
        grid_spec=grid_spec,
        out_shape=jax.ShapeDtypeStruct((NSLOT, DH), U32),
        compiler_params=_cp(("arbitrary",)),
        name="dispatch",
    )(plan["nloc"], plan["pstart"], plan["npad"], plan["n_act"], plan["dch"], xm, meta, plan["lrow"])


def _experts_kernel(layer, te_ref, nx_ref, na_ref, x_ref, wg_hbm, wu_hbm, wd_hbm, y_ref,
                    wgf, wuf, wdf, wgb, wub, wdb, wsem):
    i = pl.program_id(0)

    def weight_copies(e):
        return (pltpu.make_async_copy(wg_hbm.at[layer, e], wgf, wsem.at[0]),
                pltpu.make_async_copy(wu_hbm.at[layer, e], wuf, wsem.at[1]),
                pltpu.make_async_copy(wd_hbm.at[layer, e], wdf, wsem.at[2]))

    @pl.when(i < na_ref[0])
    def _():
        e = te_ref[i]
        first = jnp.logical_or(i == 0, e != te_ref[jnp.maximum(i - 1, 0)])

        @pl.when(i == 0)
        def _():
            for cp in weight_copies(e):
                cp.start()

        @pl.when(first)
        def _():
            for cp in weight_copies(e):
                cp.wait()
            wgb[...] = wgf[...].astype(BF16)
            wub[...] = wuf[...].astype(BF16)
            wdb[...] = wdf[...].astype(BF16)

            @pl.when(nx_ref[i] >= 0)
            def _():
                for cp in weight_copies(nx_ref[i]):
                    cp.start()

        x_lo, x_hi = _unpack_bf16_pair(x_ref[...])
        hg = _dot(x_lo, wgb[0:DH, :]) + _dot(x_hi, wgb[DH:D, :])
        hu = _dot(x_lo, wub[0:DH, :]) + _dot(x_hi, wub[DH:D, :])
        act = (_silu(hg) * hu).astype(BF16)
        y = _dot(act, wdb[...])
        y_ref[...] = _pack_bf16_pair(y.astype(BF16).astype(F32))

    @pl.when(i >= na_ref[0])
    def _():
        y_ref[...] = jnp.zeros_like(y_ref)


def _experts(plan, xs, wg, wu, wd, layer):
    hbm = pl.BlockSpec(memory_space=pl.ANY)
    grid_spec = pltpu.PrefetchScalarGridSpec(
        num_scalar_prefetch=3,
        grid=(NTILE,),
        in_specs=[pl.BlockSpec((TR, DH), lambda i, te, nx, na: (jnp.minimum(i, na[0] - 1), 0)), hbm, hbm, hbm],
        out_specs=pl.BlockSpec((TR, DH), lambda i, te, nx, na: (i, 0)),
        scratch_shapes=[pltpu.VMEM((D, DEXP), F32), pltpu.VMEM((D, DEXP), F32), pltpu.VMEM((DEXP, D), F32),
                        pltpu.VMEM((D, DEXP), BF16), pltpu.VMEM((D, DEXP), BF16), pltpu.VMEM((DEXP, D), BF16),
                        pltpu.SemaphoreType.DMA((3,))],
    )
    return pl.pallas_call(
        functools.partial(_experts_kernel, layer),
        grid_spec=grid_spec,
        out_shape=jax.ShapeDtypeStruct((NSLOT, DH), U32),
        compiler_params=_cp(("arbitrary",)),
        name="experts",
    )(plan["tile_expert"], plan["next_expert"], plan["n_act"], xs, wg, wu, wd)


def _combine_kernel(last, nloc_ref, dch_ref, dchn_ref, meta_ref, w_ref, lrow_ref, h_ref, gt2_ref, g_ref,
                    sh_ref, sc_ref, ys_hbm, *rest):
    if last:
        out_ref, yloc, sem = rest
    else:
        h_out, xn_out, yloc, sem = rest
    t = pl.program_id(0)
    slot = t % 2

    def fetch(idx_ref, buf_slot, n):
        def body(j, z):
            g = idx_ref[0, 0, j]
            pltpu.make_async_copy(ys_hbm.at[pl.ds(pl.multiple_of(g * CH, CH), CH)],
                                  yloc.at[buf_slot, pl.ds(pl.multiple_of(j * CH, CH), CH)],
                                  sem.at[buf_slot]).start()
            return z
        lax.fori_loop(0, n, body, 0)

    @pl.when(t == 0)
    def _():
        yloc[...] = jnp.zeros_like(yloc)
        fetch(dch_ref, 0, nloc_ref[0])

    @pl.when(t + 1 < NT)
    def _():
        fetch(dchn_ref, 1 - slot, nloc_ref[jnp.minimum(t + 1, NT - 1)])

    def wait_one(j, z):
        pltpu.make_async_copy(ys_hbm.at[pl.ds(0, CH)], yloc.at[slot, pl.ds(0, CH)], sem.at[slot]).wait()
        return z
    lax.fori_loop(0, nloc_ref[t], wait_one, 0)

    p1, p2 = _local_pos(meta_ref[...], lrow_ref[0])
    pos = lax.broadcasted_iota(I32, (TM, LROWS), 1)
    y_lo, y_hi = _unpack_bf16_pair(yloc[slot])
    w = w_ref[...]
    q1 = (pos == p1).astype(BF16)
    q2 = (pos == p2).astype(BF16)
    pick = lambda y: w[:, 0:1] * _dot(q1, y) + w[:, 1:2] * _dot(q2, y)
    moe = jnp.concatenate([pick(y_lo), pick(y_hi)], axis=1)
    h = h_ref[...] + gt2_ref[0] * moe
    if last:
        @pl.when(t % TPB > 0)
        def _():
            out_ref[0] = _rms(h, g_ref[0])
    else:
        h_out[...] = h
        xn_out[...] = (_rms(h, g_ref[0]) * (1.0 + sc_ref[0]) + sh_ref[0]).astype(BF16)


def _combine(layer, plan, ys, meta, wts, hmid, mods3, norm_g3):
    last = layer == NLAYER - 1
    row = lambda t, *_: _tile_row(t)
    nxt = 0 if last else layer + 1
    tile = lambda w: pl.BlockSpec((TM, w), lambda t, *_: (t, 0))
    if last:
        out_specs = pl.BlockSpec((1, TM, D), lambda t, *_: (t // TPB, jnp.maximum(t % TPB - 1, 0), 0))
        out_shape = jax.ShapeDtypeStruct((NB, SEQ, D), F32)
    else:
        out_specs = [tile(D), tile(D)]
        out_shape = [jax.ShapeDtypeStruct((P, D), F32), jax.ShapeDtypeStruct((P, D), BF16)]
    grid_spec = pltpu.PrefetchScalarGridSpec(
        num_scalar_prefetch=1,
        grid=(NT,),
        in_specs=[pl.BlockSpec((1, 1, LCH), lambda t, *_: (t, 0, 0), memory_space=pltpu.SMEM),
                  pl.BlockSpec((1, 1, LCH), lambda t, *_: (jnp.minimum(t + 1, NT - 1), 0, 0),
                               memory_space=pltpu.SMEM),
                  tile(RW), tile(RW), pl.BlockSpec((1, 1, RW), lambda t, *_: (t, 0, 0)), tile(D),
                  _mod_spec(layer, 5, row),
                  pl.BlockSpec((1, 1, D), lambda t, *_: (NLAYER if last else nxt, 0, 0)),
                  _mod_spec(nxt, 0, row), _mod_spec(nxt, 1, row),
                  pl.BlockSpec(memory_space=pl.ANY)],
        out_specs=out_specs,
        scratch_shapes=[pltpu.VMEM((2, LROWS, DH), U32), pltpu.SemaphoreType.DMA((2,))],
    )
    return pl.pallas_call(
        functools.partial(_combine_kernel, last),
        grid_spec=grid_spec,
        out_shape=out_shape,
        compiler_params=_cp(("arbitrary",)),
        name="final" if last else "combine",
    )(plan["nloc"], plan["dch"], plan["dch"], meta, wts, plan["lrow"], hmid, mods3, norm_g3, mods3, mods3, ys)


def kernel(x, c, ctx, c_ctx, w_mod, b_mod, norm1_g, norm2_g, w_in, p_fourier, gmlp_norm_g, gmlp_ws,
           gmlp_bs, p_gmlp, gla_w_a2, gla_b_a, gla_norm_g, p_gla, w_out, router_group_w, router_group_b,
           router_expert_w, router_expert_b, expert_w_gate, expert_w_up, expert_w_down, final_norm_g):
    cvec = jnp.concatenate([c, c_ctx[None, :], jnp.zeros((3, D), F32)], axis=0)
    a_, z_, q_, k_, v_, r_, lr_, gt_ = (w_in[..., 0:512], w_in[..., 512:1536], w_in[..., 1536:2048],
                                        w_in[..., 2048:2560], w_in[..., 2560:3584], w_in[..., 3584:4608],
                                        w_in[..., 4608:4640], w_in[..., 4640:])
    w_main = jnp.concatenate([gt_, v_, r_, z_, a_, q_, k_], axis=-1).astype(BF16)
    w_lr = jnp.pad(lr_, ((0, 0), (0, 0), (0, LRW - 2 * LRANK))).astype(BF16)
    wa_pad = jnp.stack([jnp.pad(gla_w_a2[:, 0], ((0, 0), (0, LRW - LRANK), (0, 0))),
                        jnp.pad(gla_w_a2[:, 1], ((0, 0), (LRANK, LRW - 2 * LRANK), (0, 0)))], axis=1)
    wa_cat = wa_pad.reshape(NLAYER, 2, LRW, LH, LDK).transpose(0, 3, 2, 1, 4).reshape(NLAYER, LH, LRW, 2 * LDK)
    wa_cat = wa_cat.astype(BF16)
    ba_cat = gla_b_a.reshape(NLAYER, 2, LH, LDK).transpose(0, 2, 1, 3).reshape(NLAYER, LH, 1, 2 * LDK)
    norm1_g3 = jnp.concatenate([norm1_g, final_norm_g[None, :]], axis=0).reshape(NLAYER + 1, 1, D)
    norm2_g3 = norm2_g.reshape(NLAYER, 1, D)
    gng = gmlp_norm_g.reshape(NLAYER, 1, GDIM)
    ws = gmlp_ws.astype(BF16)
    bsb = jnp.broadcast_to(gmlp_bs[:, :, :, None], (NLAYER, GH, GCH, GCH))
    lng = gla_norm_g.reshape(NLAYER, 1, LVD)
    pf, pg, plw, wo = (p_fourier.astype(BF16), p_gmlp.astype(BF16), p_gla.astype(BF16), w_out.astype(BF16))
    w_r = jnp.pad(jnp.concatenate([router_expert_w, router_group_w], axis=-1),
                  ((0, 0), (0, 0), (0, RW - NEXP - NGRP)))
    wrh = w_r.astype(BF16)
    wrc = jnp.concatenate([wrh, (w_r - wrh.astype(F32)).astype(BF16)], axis=-1)
    br =jnp.pad(jnp.concatenate([router_expert_b, router_group_b], axis=-1),
                 ((0, 0), (0, RW - NEXP - NGRP))).reshape(NLAYER, 1, RW)
    cs_lat, cs_ctx, cc = _dft_consts()

    mods3 = _mods(cvec, w_mod, b_mod).reshape(NLAYER * 8 * 6, 1, D)

    h, xn = _init(x, ctx, norm1_g3, mods3)
    res = None
    for layer in range(NLAYER):
        pm = _matmul(xn, w_main, layer, 1024, 1536, BF16)
        lr = _matmul(xn, w_lr, layer, 1024, LRW, F32)
        o = _gla(pm, lr, wa_cat, ba_cat, layer)
        yf = _fourier(pm, cs_lat, cs_ctx, cc)
        hmid, xm, logits = _merge(layer, yf, pm, o, h, mods3, norm2_g3, gng, ws, bsb, lng,
                                  pf, pg, plw, wo, wrc, br)
        meta, wts, cnt8 = _route(logits)
        plan = _plan(cnt8)
        xs = _dispatch(plan, xm, meta)
        ys = _experts(plan, xs, expert_w_gate, expert_w_up, expert_w_down, layer)
        res = _combine(layer, plan, ys, meta, wts, hmid, mods3, norm1_g3)
        if layer + 1 < NLAYER:
            h, xn = res
    return res
```

```python
import functools
import math

import numpy as np
import jax
import jax.numpy as jnp
from jax import lax
from jax.experimental import pallas as pl
from jax.experimental.pallas import tpu as pltpu

F32 = jnp.float32
BF16 = jnp.bfloat16
I32 = jnp.int32

D = 2048
NB = 4
SEQ = 2048
NLAYER = 4
CTX = 256
EPS = 1e-6
LB = CTX + SEQ
P = NB * LB
TM = 256
TPB = LB // TM
NT = P // TM

FG, FGD = 4, 128
FDIM = FG * FGD
GH, GHD, GCH = 4, 128, 128
GDIM = GH * GHD
LH, LDK, LDV, LRANK, LTAU, LC = 4, 128, 256, 16, 16.0, 64
LKD, LVD = LH * LDK, LH * LDV
NCH = LB // LC
NCTXCH = CTX // LC

W_ALIGNED = 4608
W_GATE0 = W_ALIGNED + 2 * 16
TN = 1536
NBLK_AL, NBLK_GT = W_ALIGNED // TN, 3 * D // TN
C_G0, C_A, C_ZU, C_ZV, C_Q, C_K, C_V, C_R = 0, 6144, 6656, 7168, 7680, 8192, 8704, 9728
NMAIN = 10752
LRW = 128

NGRP, EPG, NEXP, DEXP = 4, 8, 32, 512
TR = 256
CH = 8
LCH = (2 * TM + NEXP * (CH - 1)) // CH
LROWS = 768
TCH = TR // CH
NCHUNK = 2 * P // CH + NT * NEXP * (CH - 1) // CH + NEXP * (TCH - 1)
NTILE = NCHUNK // TCH + 1
NSLOT = NTILE * TR
RW = 128

VMEM_LIMIT = 56 * 1024 * 1024


def _cp(sem, vmem=VMEM_LIMIT):
    return pltpu.CompilerParams(dimension_semantics=sem, vmem_limit_bytes=vmem)


def _dot(a, b):
    return jnp.dot(a, b, preferred_element_type=F32)


def _dot_t(a, b):
    return lax.dot_general(a, b, (((1,), (1,)), ((), ())), preferred_element_type=F32)


def _dot_lt(a, b):
    return lax.dot_general(a, b, (((0,), (0,)), ((), ())), preferred_element_type=F32)


def _split(x):
    hi = x.astype(BF16)
    lo = (x - hi.astype(F32)).astype(BF16)
    return hi, lo


U32 = jnp.uint32
DH = D // 2


def _pack_bf16_pair(x):
    lo = lax.bitcast_convert_type(x[:, :DH], U32)
    hi = lax.bitcast_convert_type(x[:, DH:], U32)
    return lax.shift_right_logical(lo, jnp.uint32(16)) | (hi & jnp.uint32(0xFFFF0000))


def _unpack_bf16_pair(u):
    lo = lax.bitcast_convert_type(lax.shift_left(u, jnp.uint32(16)), F32)
    hi = lax.bitcast_convert_type(u & jnp.uint32(0xFFFF0000), F32)
    return lo.astype(BF16), hi.astype(BF16)


def _sigmoid(x):
    return 1.0 / (1.0 + jnp.exp(-x))


def _silu(x):
    return x * _sigmoid(x)


def _gelu_tanh(x):
    return 0.5 * x * (1.0 + jnp.tanh(math.sqrt(2.0 / math.pi) * (x + 0.044715 * (x * x * x))))


def _rms(x, g):
    return x * lax.rsqrt(jnp.mean(x * x, axis=-1, keepdims=True) + EPS) * g


def _tile_row(t):
    return jnp.where(t % TPB == 0, 4, t // TPB)


def _mod_spec(layer, comp, row_fn):
    return pl.BlockSpec((1, 1, D), lambda *g: ((layer * 8 + row_fn(*g)) * 6 + comp, 0, 0))


def _mods_kernel(c_ref, w_ref, b_ref, o_ref):
    c = c_ref[...]
    s = _silu(c).astype(BF16)
    o_ref[0] = _dot(s, w_ref[0].astype(BF16)) + b_ref[0]


def _mods(cvec, w_mod, b_mod):
    tn = 1024
    return pl.pallas_call(
        _mods_kernel,
        grid=(NLAYER, 6 * D // tn),
        in_specs=[pl.BlockSpec((8, D), lambda l, j: (0, 0)),
                  pl.BlockSpec((1, D, tn), lambda l, j: (l, 0, j)),
                  pl.BlockSpec((1, 1, tn), lambda l, j: (l, 0, j))],
        out_specs=pl.BlockSpec((1, 8, tn), lambda l, j: (l, 0, j)),
        out_shape=jax.ShapeDtypeStruct((NLAYER, 8, 6 * D), F32),
        compiler_params=_cp(("arbitrary", "arbitrary")),
        name="mods",
    )(cvec, w_mod, b_mod.reshape(NLAYER, 1, 6 * D))


def _init_kernel(x_ref, c_ref, g_ref, sh_ref, sc_ref, h_ref, xn_ref):
    j = pl.program_id(1)

    def emit(v):
        h_ref[...] = v
        xn_ref[...] = (_rms(v, g_ref[0]) * (1.0 + sc_ref[0]) + sh_ref[0]).astype(BF16)

    @pl.when(j == 0)
    def _():
        emit(c_ref[0])

    @pl.when(j > 0)
    def _():
        emit(x_ref[0])


def _init(x, ctx, norm1_g3, mods3):
    row = lambda b, j: jnp.where(j == 0, 4, b)
    return pl.pallas_call(
        _init_kernel,
        grid=(NB, TPB),
        in_specs=[pl.BlockSpec((1, TM, D), lambda b, j: (b, jnp.maximum(j - 1, 0), 0)),
                  pl.BlockSpec((1, CTX, D), lambda b, j: (b, 0, 0)),
                  pl.BlockSpec((1, 1, D), lambda b, j: (0, 0, 0)),
                  _mod_spec(0, 0, row), _mod_spec(0, 1, row)],
        out_specs=[pl.BlockSpec((TM, D), lambda b, j: (b * TPB + j, 0)),
                   pl.BlockSpec((TM, D), lambda b, j: (b * TPB + j, 0))],
        out_shape=[jax.ShapeDtypeStruct((P, D), F32), jax.ShapeDtypeStruct((P, D), BF16)],
        compiler_params=_cp(("arbitrary", "arbitrary")),
        name="init",
    )(x, ctx, norm1_g3, mods3, mods3)


TMP = 1024


def _inproj_kernel(x_ref, w_ref, wt_ref, o_ref, wb_ref):
    j = pl.program_id(0)

    @pl.when((pl.program_id(1) == 0) & (j < NBLK_AL))
    def _():
        wb_ref[...] = w_ref[0].astype(BF16)

    @pl.when((pl.program_id(1) == 0) & (j >= NBLK_AL))
    def _():
        off = W_GATE0 - W_ALIGNED
        wide = jnp.concatenate([w_ref[0], wt_ref[0]], axis=1)
        wb_ref[...] = wide[:, off:off + TN].astype(BF16)

    o_ref[...] = _dot(x_ref[...], wb_ref[...]).astype(BF16)


def _inproj(xn, w_in, layer):
    nblk = NBLK_AL + NBLK_GT
    tail = lambda j, i: (layer, 0, jnp.where(j >= NBLK_AL, (j + 1) * (TN // LRW), (NBLK_AL + 1) * (TN // LRW)))
    return pl.pallas_call(
        _inproj_kernel,
        grid=(nblk, P // TMP),
        in_specs=[pl.BlockSpec((TMP, D), lambda j, i: (i, 0)),
                  pl.BlockSpec((1, D, TN), lambda j, i: (layer, 0, j)),
                  pl.BlockSpec((1, D, LRW), tail)],
        out_specs=pl.BlockSpec((TMP, TN), lambda j, i: (i, (j + NBLK_GT) % nblk)),
        out_shape=jax.ShapeDtypeStruct((P, NMAIN), BF16),
        scratch_shapes=[pltpu.VMEM((D, TN), BF16)],
        compiler_params=_cp(("arbitrary", "arbitrary")),
        name="inproj",
    )(xn, w_in, w_in)


def _lrproj_kernel(x_ref, w_ref, o_ref):
    o_ref[...] = _dot(x_ref[...], w_ref[0].astype(BF16))


def _lrproj(xn, w_in, layer):
    tm = 1024
    return pl.pallas_call(
        _lrproj_kernel,
        grid=(P // tm,),
        in_specs=[pl.BlockSpec((tm, D), lambda i: (i, 0)),
                  pl.BlockSpec((1, D, LRW), lambda i: (layer, 0, W_ALIGNED // LRW))],
        out_specs=pl.BlockSpec((tm, LRW), lambda i: (i, 0)),
        out_shape=jax.ShapeDtypeStruct((P, LRW), F32),
        compiler_params=_cp(("arbitrary",)),
        name="lrproj",
    )(xn, w_in)


def _dft_consts():
    def cs(n):
        k = np.arange(n, dtype=np.int64)
        ang = 2.0 * np.pi * ((k[:, None] * k[None, :]) % n).astype(np.float64) / n
        return np.cos(ang) / math.sqrt(n), np.sin(ang) / math.sqrt(n)

    c_l, s_l = cs(SEQ)
    c_c, s_c = cs(CTX)
    c_g, s_g = cs(FGD)
    cs_lat = np.concatenate([c_l, -s_l], axis=1).astype(BF16)
    cs_ctx = np.concatenate([c_c, -s_c], axis=1).astype(BF16)
    cc = np.concatenate([c_g, s_g], axis=1).astype(BF16)
    return cs_lat, cs_ctx, cc


def _fourier_kernel(a_ref, csl_ref, csc_ref, cc_ref, o_ref, rl_ref, rc_ref):
    j = pl.program_id(1)

    @pl.when(j == 0)
    def _():
        for g in range(FG):
            cols = slice(g * FGD, (g + 1) * FGD)
            t = _dot(a_ref[:, cols], cc_ref[...]).astype(BF16)
            rc_ref[0:CTX, cols] = t[0:CTX, 0:FGD]
            rc_ref[CTX:2 * CTX, cols] = t[0:CTX, FGD:2 * FGD]
            rl_ref[0:SEQ, cols] = t[CTX:LB, 0:FGD]
            rl_ref[SEQ:2 * SEQ, cols] = t[CTX:LB, FGD:2 * FGD]
        o_ref[...] = _dot(csc_ref[...], rc_ref[...]).astype(BF16)

    @pl.when(j > 0)
    def _():
        o_ref[...] = _dot(csl_ref[...], rl_ref[...]).astype(BF16)


def _fourier(pm, cs_lat, cs_ctx, cc):
    return pl.pallas_call(
        _fourier_kernel,
        grid=(NB, TPB),
        in_specs=[pl.BlockSpec((LB, FDIM), lambda b, j: (b, C_A // FDIM)),
                  pl.BlockSpec((TM, 2 * SEQ), lambda b, j: (jnp.maximum(j - 1, 0), 0)),
                  pl.BlockSpec((CTX, 2 * CTX), lambda b, j: (0, 0)),
                  pl.BlockSpec((FGD, 2 * FGD), lambda b, j: (0, 0))],
        out_specs=pl.BlockSpec((TM, FDIM), lambda b, j: (b * TPB + j, 0)),
        out_shape=jax.ShapeDtypeStruct((P, FDIM), BF16),
        scratch_shapes=[pltpu.VMEM((2 * SEQ, FDIM), BF16), pltpu.VMEM((2 * CTX, FDIM), BF16)],
        compiler_params=_cp(("arbitrary", "arbitrary")),
        name="fourier",
    )(pm, cs_lat, cs_ctx, cc)


SB = 256
SBC = SB // LC
NSB = LB // SB


def _gla_kernel(q_ref, k_ref, v_ref, lr_ref, wa_ref, ba_ref, o_ref,
                qd_ref, oacc_ref, ds_ref, gam_ref, sall_ref, sf_ref, sb_ref):
    ri = lax.broadcasted_iota(I32, (SB, SB), 0)
    ci = lax.broadcasted_iota(I32, (SB, SB), 1)
    same = (ri // LC) == (ci // LC)
    tri = (same & (ci <= ri)).astype(BF16)
    keep_f = same & (ci <= ri)
    keep_b = same & (ci > ri)
    rchunk = lax.broadcasted_iota(I32, (SB, LDK), 0) // LC
    scale = LDK ** -0.5
    wa = wa_ref[...]
    ba = ba_ref[...]

    def phase1(sb, carry):
        rows = pl.ds(pl.multiple_of(sb * SB, SB), SB)
        logits = _dot(lr_ref[rows, :].astype(BF16), wa) + ba
        g = (jnp.minimum(logits, 0.0) - jnp.log1p(jnp.exp(-jnp.abs(logits)))) * (1.0 / LTAU)
        g_hi, g_lo = _split(g)
        pre = _dot(tri, g_hi) + _dot(tri, g_lo)
        tot = jnp.concatenate(
            [jnp.broadcast_to(pre[c * LC + LC - 1:c * LC + LC, :], (LC, 2 * LDK)) for c in range(SBC)], axis=0)
        q = q_ref[rows, :].astype(F32) * scale
        k = k_ref[rows, :].astype(F32)
        v = v_ref[rows, :]
        s_sum = None
        kts = []
        for d in range(2):
            cols = slice(d * LDK, (d + 1) * LDK)
            t_d = tot[:, cols]
            b_d = pre[:, cols] if d == 0 else t_d - pre[:, cols] + g[:, cols]
            q_dec = (q * jnp.exp(b_d)).astype(BF16)
            k_inv = (k * jnp.exp(-b_d)).astype(BF16)
            k_tail = (k * jnp.exp(t_d - b_d)).astype(BF16)
            sc = jnp.where(keep_f if d == 0 else keep_b, _dot_t(q_dec, k_inv), 0.0)
            s_sum = sc if s_sum is None else s_sum + sc
            qd_ref[rows, cols] = q_dec
            zero = jnp.zeros_like(k_tail)
            kts += [jnp.where(rchunk == c, k_tail, zero) for c in range(SBC)]
            for c in range(SBC):
                gam_ref[d, sb * SBC + c] = jnp.exp(t_d[c * LC:c * LC + 1, :])
        oacc_ref[rows, :] = _dot(s_sum.astype(BF16), v)
        dst = _dot_lt(v, jnp.concatenate(kts, axis=1))
        for d in range(2):
            for c in range(SBC):
                j = d * SBC + c
                ds_ref[d, sb * SBC + c] = dst[:, j * LDK:(j + 1) * LDK]
        return carry

    lax.fori_loop(0, NSB, phase1, 0, unroll=3)

    sf_ref[...] = jnp.zeros_like(sf_ref)
    sb_ref[...] = jnp.zeros_like(sb_ref)

    def phase2(i, carry):
        nb = jnp.where(i < NCTXCH, NCTXCH - 1 - i, NCH + NCTXCH - 1 - i)
        s_f = sf_ref[...]
        s_b = sb_ref[...]
        sall_ref[i, :, 0:LDK] = s_f.astype(BF16)
        sall_ref[nb, :, LDK:2 * LDK] = s_b.astype(BF16)
        sf_ref[...] = s_f * gam_ref[0, i] + ds_ref[0, i]
        sb_ref[...] = s_b * gam_ref[1, nb] + ds_ref[1, nb]
        return carry

    lax.fori_loop(0, NCH, phase2, 0)

    def phase3(n, carry):
        rows = pl.ds(pl.multiple_of(n * LC, LC), LC)
        o_ref[rows, :] = (oacc_ref[rows, :] + _dot_t(qd_ref[rows, :], sall_ref[n])).astype(BF16)
        return carry

    lax.fori_loop(0, NCH, phase3, 0, unroll=4)


def _gla(pm, lr, wa_cat, ba_cat, layer):
    return pl.pallas_call(
        _gla_kernel,
        grid=(NB, LH),
        in_specs=[pl.BlockSpec((LB, LDK), lambda b, h: (b, C_Q // LDK + h)),
                  pl.BlockSpec((LB, LDK), lambda b, h: (b, C_K // LDK + h)),
                  pl.BlockSpec((LB, LDV), lambda b, h: (b, C_V // LDV + h)),
                  pl.BlockSpec((LB, LRW), lambda b, h: (b, 0)),
                  pl.BlockSpec((None, None, LRW, 2 * LDK), lambda b, h: (layer, h, 0, 0)),
                  pl.BlockSpec((None, None, 1, 2 * LDK), lambda b, h: (layer, h, 0, 0))],
        out_specs=pl.BlockSpec((LB, LDV), lambda b, h: (b, h)),
        out_shape=jax.ShapeDtypeStruct((P, LVD), BF16),
        scratch_shapes=[pltpu.VMEM((LB, 2 * LDK), BF16), pltpu.VMEM((LB, LDV), F32),
                        pltpu.VMEM((2, NCH, LDV, LDK), F32), pltpu.VMEM((2, NCH, 1, LDK), F32),
                        pltpu.VMEM((NCH, LDV, 2 * LDK), BF16),
                        pltpu.VMEM((LDV, LDK), F32), pltpu.VMEM((LDV, LDK), F32)],
        compiler_params=_cp(("arbitrary", "arbitrary")),
        name="gla",
    )(pm, pm, pm, lr, wa_cat, ba_cat)


def _merge_kernel(yf_ref, zu_ref, zv_ref, o_ref, ra_ref, rb_ref, g0_ref, g1_ref, g2_ref, h_ref,
                  gt1_ref, sh2_ref, sc2_ref, n2g_ref, gng_ref, ws_ref, bs_ref, lng_ref,
                  pf_ref, pg_ref, pl_ref, wo_ref, wrc_ref, br_ref,
                  hmid_ref, xm_ref, lg_ref):
    y = _sigmoid(g0_ref[...].astype(F32)) * _dot(yf_ref[...], pf_ref[...])

    u = _gelu_tanh(zu_ref[...].astype(F32))
    v = _rms(_gelu_tanh(zv_ref[...].astype(F32)), gng_ref[...]).astype(BF16)
    chunks = []
    for ch in range(TM // GCH):
        rows = slice(ch * GCH, (ch + 1) * GCH)
        heads = [_dot(ws_ref[hd], v[rows, hd * GHD:(hd + 1) * GHD]) + bs_ref[hd] for hd in range(GH)]
        chunks.append(jnp.concatenate(heads, axis=1))
    s = jnp.concatenate(chunks, axis=0)
    y += _sigmoid(g1_ref[...].astype(F32)) * _dot((u * s).astype(BF16), pg_ref[...])

    o = o_ref[...].astype(F32)
    lng = lng_ref[...]
    heads = [_rms(o[:, hd * LDV:(hd + 1) * LDV], lng[:, hd * LDV:(hd + 1) * LDV]) for hd in range(LH)]
    r = jnp.concatenate([ra_ref[...], rb_ref[...]], axis=1).astype(F32)
    ol = (jnp.concatenate(heads, axis=1) * _silu(r)).astype(BF16)
    y += _sigmoid(g2_ref[...].astype(F32)) * _dot(ol, pl_ref[...])

    hmid = h_ref[...] + gt1_ref[0] * _dot(y.astype(BF16), wo_ref[...])
    hmid_ref[...] = hmid

    xm = _rms(hmid, n2g_ref[0]) * (1.0 + sc2_ref[0]) + sh2_ref[0]
    xm_ref[...] = xm
    xm_hi, xm_lo = _split(xm)
    hh_hl = _dot(xm_hi, wrc_ref[...])
    lg_ref[...] = hh_hl[:, 0:RW] + hh_hl[:, RW:2 * RW] + _dot(xm_lo, wrc_ref[:, 0:RW]) + br_ref[...]


def _merge(layer, yf, pm, o, h, mods3, norm2_g3, gng, ws, bsb, lng, pf, pg, plw, wo, wrc, br):
    row = _tile_row
    tile = lambda w, c: pl.BlockSpec((TM, w), lambda t: (t, c))
    lay3 = lambda a, b: pl.BlockSpec((None, a, b), lambda t: (layer, 0, 0), pipeline_mode=pl.Buffered(1))
    lay4 = lambda a, b, c: pl.BlockSpec((None, a, b, c), lambda t: (layer, 0, 0, 0))
    return pl.pallas_call(
        _merge_kernel,
        grid=(NT,),
        in_specs=[tile(FDIM, 0), tile(GDIM, C_ZU // GDIM), tile(GDIM, C_ZV // GDIM), tile(LVD, 0),
                  tile(LVD // 2, C_R // (LVD // 2)), tile(LVD // 2, C_R // (LVD // 2) + 1),
                  tile(D, 0), tile(D, 1), tile(D, 2), tile(D, 0),
                  _mod_spec(layer, 2, row), _mod_spec(layer, 3, row), _mod_spec(layer, 4, row),
                  pl.BlockSpec((1, 1, D), lambda t: (layer, 0, 0)),
                  lay3(1, GDIM), lay4(GH, GCH, GCH), lay4(GH, GCH, GCH), lay3(1, LVD),
                  lay3(FDIM, D), lay3(GDIM, D), lay3(LVD, D), lay3(D, D),
                  lay3(D, 2 * RW), lay3(1, RW)],
        out_specs=[tile(D, 0), tile(D, 0), tile(RW, 0)],
        out_shape=[jax.ShapeDtypeStruct((P, D), F32), jax.ShapeDtypeStruct((P, D), F32),
                   jax.ShapeDtypeStruct((P, RW), F32)],
        compiler_params=_cp(("arbitrary",)),
        name="merge",
    )(yf, pm, pm, o, pm, pm, pm, pm, pm, h, mods3, mods3, mods3, norm2_g3, gng, ws, bsb, lng,
      pf, pg, plw, wo, wrc, br)


def _route_kernel(lg_ref, meta_ref, wts_ref, cnt_ref):
    lg = lg_ref[...]
    lane = lax.broadcasted_iota(I32, (TM, RW), 1)
    lane_f = lane.astype(F32)
    ninf = jnp.float32(-jnp.inf)

    def first_max(x):
        m = jnp.max(x, axis=-1, keepdims=True)
        first = jnp.min(jnp.where(x == m, lane_f, float(RW)), axis=-1, keepdims=True)
        return m, first.astype(I32)

    is_g = (lane >= NEXP) & (lane < NEXP + NGRP)
    gmax, glane = first_max(jnp.where(is_g, lg, ninf))
    gsum = jnp.sum(jnp.where(is_g, jnp.exp(lg - gmax), 0.0), axis=-1, keepdims=True)
    g_w = 1.0 / gsum
    lo = (glane - NEXP) * EPG
    in_grp = (lane >= lo) & (lane < lo + EPG)
    el = jnp.where(in_grp, lg, ninf)
    v1, l1 = first_max(el)
    v2, l2 = first_max(jnp.where(lane == l1, ninf, el))
    e = jnp.exp(v2 - v1)
    w1 = g_w / (1.0 + e)
    w2 = g_w * e / (1.0 + e)

    hit1 = lane == l1
    hit2 = lane == l2
    m = (hit1 | hit2).astype(BF16)
    ri = lax.broadcasted_iota(I32, (TM, TM), 0)
    ci = lax.broadcasted_iota(I32, (TM, TM), 1)
    before = _dot((ci < ri).astype(BF16), m)
    r1 = jnp.sum(jnp.where(hit1, before, 0.0), axis=-1, keepdims=True).astype(I32)
    r2 = jnp.sum(jnp.where(hit2, before, 0.0), axis=-1, keepdims=True).astype(I32)
    total = jnp.sum(m.astype(F32), axis=0, keepdims=True)
    cnt_ref[...] = jnp.broadcast_to(total, cnt_ref.shape).astype(I32)
    meta_ref[...] = jnp.where(lane == 0, l1, jnp.where(lane == 1, l2, jnp.where(lane == 2, r1,
                              jnp.where(lane == 3, r2, 0))))
    wts_ref[...] = jnp.where(lane == 0, w1, jnp.where(lane == 1, w2, 0.0))


def _route(logits):
    return pl.pallas_call(
        _route_kernel,
        grid=(NT,),
        in_specs=[pl.BlockSpec((TM, RW), lambda t: (t, 0))],
        out_specs=[pl.BlockSpec((TM, RW), lambda t: (t, 0)), pl.BlockSpec((TM, RW), lambda t: (t, 0)),
                   pl.BlockSpec((8, RW), lambda t: (t, 0))],
        out_shape=[jax.ShapeDtypeStruct((P, RW), I32), jax.ShapeDtypeStruct((P, RW), F32),
                   jax.ShapeDtypeStruct((NT * 8, RW), I32)],
        compiler_params=_cp(("arbitrary",)),
        name="route",
    )(logits)


def _plan(cnt8):
    cnt = cnt8.reshape(NT, 8, RW)[:, 0, :NEXP]
    c8 = (cnt + (CH - 1)) // CH
    lend = jnp.cumsum(c8, axis=1)
    lstart = lend - c8
    nloc = lend[:, -1]
    reg = jnp.sum(c8, axis=0)
    rpad = (reg + (TCH - 1)) // TCH * TCH
    rend = jnp.cumsum(rpad)
    rstart = rend - rpad
    gbase = rstart[None, :] + jnp.cumsum(c8, axis=0) - c8
    j = jnp.arange(LCH, dtype=I32)
    owner = jnp.sum((lend[:, None, :] <= j[None, :, None]).astype(I32), axis=2)
    sel = (owner[:, :, None] == jnp.arange(NEXP, dtype=I32)[None, None, :]).astype(I32)
    dch = jnp.sum(sel * (gbase - lstart)[:, None, :], axis=2) + j[None, :]
    n_act = rend[-1] // TCH
    tid = jnp.minimum(jnp.arange(NTILE, dtype=I32), n_act - 1)
    tile_expert = jnp.minimum(jnp.sum((rend[None, :] <= (tid * TCH)[:, None]).astype(I32), axis=1), NEXP - 1)
    nxt_tile = jnp.sum((tile_expert[:, None] == jnp.arange(NEXP, dtype=I32)[None, :]).astype(I32)
                       * (rend // TCH)[None, :], axis=1)
    nxt_onehot = (nxt_tile[:, None] == jnp.arange(NTILE, dtype=I32)[None, :]).astype(I32)
    next_expert = jnp.where(nxt_tile < n_act, jnp.sum(nxt_onehot * tile_expert[None, :], axis=1), -1)
    lrow = jnp.pad((lstart * CH).astype(F32), ((0, 0), (0, RW - NEXP))).reshape(NT, 1, RW)
    return dict(nloc=nloc.astype(I32), dch=dch.astype(I32).reshape(NT, 1, LCH), lrow=lrow,
                pstart=(rstart + reg).astype(I32), npad=(rpad - reg).astype(I32),
                tile_expert=tile_expert.astype(I32), next_expert=next_expert.astype(I32),
                n_act=n_act.reshape(1).astype(I32))


def _local_pos(meta, lrow):
    lane = lax.broadcasted_iota(I32, (TM, RW), 1)
    meta_f = meta.astype(F32)
    col = lambda k: jnp.sum(jnp.where(lane == k, meta_f, 0.0), axis=-1, keepdims=True).astype(I32)
    l1, l2, r1, r2 = col(0), col(1), col(2), col(3)
    off = lambda l: jnp.sum(jnp.where(lane == l, lrow, 0.0), axis=-1, keepdims=True).astype(I32)
    return off(l1) + r1, off(l2) + r2


def _dispatch_kernel(nloc_ref, pstart_ref, npad_ref, na_ref, dch_ref, x_ref, meta_ref, lrow_ref, xs_hbm,
                     xloc, zbuf, sem, zsem):
    t = pl.program_id(0)
    slot = t % 2

    def chunk_copy(buf_slot, j, g):
        return pltpu.make_async_copy(xloc.at[buf_slot, pl.ds(pl.multiple_of(j * CH, CH), CH)],
                                     xs_hbm.at[pl.ds(pl.multiple_of(g * CH, CH), CH)], sem.at[buf_slot])

    @pl.when(t == 0)
    def _():
        zbuf[...] = jnp.zeros_like(zbuf)

        def zero_chunks(first, n):
            zcopy = lambda c: pltpu.make_async_copy(
                zbuf, xs_hbm.at[pl.ds(pl.multiple_of((first + c) * CH, CH), CH)], zsem.at[0])
            lax.fori_loop(0, n, lambda c, z: (zcopy(c).start(), z)[1], 0)
            lax.fori_loop(0, n, lambda c, z: (zcopy(0).wait(), z)[1], 0)

        for e in range(NEXP):
            zero_chunks(pstart_ref[e], npad_ref[e])
        zero_chunks(na_ref[0] * TCH, NTILE * TCH - na_ref[0] * TCH)

    p1, p2 = _local_pos(meta_ref[...], lrow_ref[0])
    pos = lax.broadcasted_iota(I32, (TM, LROWS), 1)
    sel = ((pos == p1) | (pos == p2)).astype(BF16)
    xloc[slot] = _pack_bf16_pair(_dot_lt(sel, x_ref[...].astype(BF16)))

    @pl.when(t > 0)
    def _():
        lax.fori_loop(0, nloc_ref[t - 1], lambda j, z: (chunk_copy(1 - slot, 0, 0).wait(), z)[1], 0)

    lax.fori_loop(0, nloc_ref[t], lambda j, z: (chunk_copy(slot, j, dch_ref[0, 0, j]).start(), z)[1], 0)

    @pl.when(t == NT - 1)
    def _():
        lax.fori_loop(0, nloc_ref[t], lambda j, z: (chunk_copy(slot, 0, 0).wait(), z)[1], 0)


def _dispatch(plan, xm, meta):
    grid_spec = pltpu.PrefetchScalarGridSpec(
        num_scalar_prefetch=4,
        grid=(NT,),
        in_specs=[pl.BlockSpec((1, 1, LCH), lambda t, *_: (t, 0, 0), memory_space=pltpu.SMEM),
                  pl.BlockSpec((TM, D), lambda t, *_: (t, 0)),
                  pl.BlockSpec((TM, RW), lambda t, *_: (t, 0)),
                  pl.BlockSpec((1, 1, RW), lambda t, *_: (t, 0, 0))],
        out_specs=pl.BlockSpec(memory_space=pl.ANY),
        scratch_shapes=[pltpu.VMEM((2, LROWS, DH), U32), pltpu.VMEM((CH, DH), U32),
                        pltpu.SemaphoreType.DMA((2,)), pltpu.SemaphoreType.DMA((1,))],
    )
    return pl.pallas_call(
        _dispatch_kernel,
        grid_spec=grid_spec,
        out_shape=jax.ShapeDtypeStruct((NSLOT, DH), U32),
        compiler_params=_cp(("arbitrary",)),
        name="dispatch",
    )(plan["nloc"], plan["pstart"], plan["npad"], plan["n_act"], plan["dch"], xm, meta, plan["lrow"])


def _experts_kernel(layer, te_ref, nx_ref, na_ref, x_ref, wg_hbm, wu_hbm, wd_hbm, y_ref,
                    wgf, wuf, wdf, wgb, wub, wdb, wsem):
    i = pl.program_id(0)

    def weight_copies(e):
        return (pltpu.make_async_copy(wg_hbm.at[layer, e], wgf, wsem.at[0]),
                pltpu.make_async_copy(wu_hbm.at[layer, e], wuf, wsem.at[1]),
                pltpu.make_async_copy(wd_hbm.at[layer, e], wdf, wsem.at[2]))

    @pl.when(i < na_ref[0])
    def _():
        e = te_ref[i]
        first = jnp.logical_or(i == 0, e != te_ref[jnp.maximum(i - 1, 0)])

        @pl.when(i == 0)
        def _():
            for cp in weight_copies(e):
                cp.start()

        @pl.when(first)
        def _():
            for cp in weight_copies(e):
                cp.wait()
            wgb[...] = wgf[...].astype(BF16)
            wub[...] = wuf[...].astype(BF16)
            wdb[...] = wdf[...].astype(BF16)

            @pl.when(nx_ref[i] >= 0)
            def _():
                for cp in weight_copies(nx_ref[i]):
                    cp.start()

        x_lo, x_hi = _unpack_bf16_pair(x_ref[...])
        hg = _dot(x_lo, wgb[0:DH, :]) + _dot(x_hi, wgb[DH:D, :])
        hu = _dot(x_lo, wub[0:DH, :]) + _dot(x_hi, wub[DH:D, :])
        act = (_silu(hg) * hu).astype(BF16)
        y = _dot(act, wdb[...])
        y_ref[...] = _pack_bf16_pair(y.astype(BF16).astype(F32))

    @pl.when(i >= na_ref[0])
    def _():
        y_ref[...] = jnp.zeros_like(y_ref)


def _experts(plan, xs, wg, wu, wd, layer):
    hbm = pl.BlockSpec(memory_space=pl.ANY)
    grid_spec = pltpu.PrefetchScalarGridSpec(
        num_scalar_prefetch=3,
        grid=(NTILE,),
        in_specs=[pl.BlockSpec((TR, DH), lambda i, te, nx, na: (jnp.minimum(i, na[0] - 1), 0)), hbm, hbm, hbm],
        out_specs=pl.BlockSpec((TR, DH), lambda i, te, nx, na: (i, 0)),
        scratch_shapes=[pltpu.VMEM((D, DEXP), F32), pltpu.VMEM((D, DEXP), F32), pltpu.VMEM((DEXP, D), F32),
                        pltpu.VMEM((D, DEXP), BF16), pltpu.VMEM((D, DEXP), BF16), pltpu.VMEM((DEXP, D), BF16),
                        pltpu.SemaphoreType.DMA((3,))],
    )
    return pl.pallas_call(
        functools.partial(_experts_kernel, layer),
        grid_spec=grid_spec,
        out_shape=jax.ShapeDtypeStruct((NSLOT, DH), U32),
        compiler_params=_cp(("arbitrary",)),
        name="experts",
    )(plan["tile_expert"], plan["next_expert"], plan["n_act"], xs, wg, wu, wd)


def _combine_kernel(last, nloc_ref, dch_ref, dchn_ref, meta_ref, w_ref, lrow_ref, h_ref, gt2_ref, g_ref,
                    sh_ref, sc_ref, ys_hbm, *rest):
    if last:
        out_ref, yloc, sem = rest
    else:
        h_out, xn_out, yloc, sem = rest
    t = pl.program_id(0)
    slot = t % 2

    def fetch(idx_ref, buf_slot, n):
        def body(j, z):
            g = idx_ref[0, 0, j]
            pltpu.make_async_copy(ys_hbm.at[pl.ds(pl.multiple_of(g * CH, CH), CH)],
                                  yloc.at[buf_slot, pl.ds(pl.multiple_of(j * CH, CH), CH)],
                                  sem.at[buf_slot]).start()
            return z
        lax.fori_loop(0, n, body, 0)

    @pl.when(t == 0)
    def _():
        yloc[...] = jnp.zeros_like(yloc)
        fetch(dch_ref, 0, nloc_ref[0])

    @pl.when(t + 1 < NT)
    def _():
        fetch(dchn_ref, 1 - slot, nloc_ref[jnp.minimum(t + 1, NT - 1)])

    def wait_one(j, z):
        pltpu.make_async_copy(ys_hbm.at[pl.ds(0, CH)], yloc.at[slot, pl.ds(0, CH)], sem.at[slot]).wait()
        return z
    lax.fori_loop(0, nloc_ref[t], wait_one, 0)

    p1, p2 = _local_pos(meta_ref[...], lrow_ref[0])
    pos = lax.broadcasted_iota(I32, (TM, LROWS), 1)
    y_lo, y_hi = _unpack_bf16_pair(yloc[slot])
    w = w_ref[...]
    q1 = (pos == p1).astype(BF16)
    q2 = (pos == p2).astype(BF16)
    pick = lambda y: w[:, 0:1] * _dot(q1, y) + w[:, 1:2] * _dot(q2, y)
    moe = jnp.concatenate([pick(y_lo), pick(y_hi)], axis=1)
    h = h_ref[...] + gt2_ref[0] * moe
    if last:
        @pl.when(t % TPB > 0)
        def _():
            out_ref[0] = _rms(h, g_ref[0])
    else:
        h_out[...] = h
        xn_out[...] = (_rms(h, g_ref[0]) * (1.0 + sc_ref[0]) + sh_ref[0]).astype(BF16)


def _combine(layer, plan, ys, meta, wts, hmid, mods3, norm_g3):
    last = layer == NLAYER - 1
    row = lambda t, *_: _tile_row(t)
    nxt = 0 if last else layer + 1
    tile = lambda w: pl.BlockSpec((TM, w), lambda t, *_: (t, 0))
    if last:
        out_specs = pl.BlockSpec((1, TM, D), lambda t, *_: (t // TPB, jnp.maximum(t % TPB - 1, 0), 0))
        out_shape = jax.ShapeDtypeStruct((NB, SEQ, D), F32)
    else:
        out_specs = [tile(D), tile(D)]
        out_shape = [jax.ShapeDtypeStruct((P, D), F32), jax.ShapeDtypeStruct((P, D), BF16)]
    grid_spec = pltpu.PrefetchScalarGridSpec(
        num_scalar_prefetch=1,
        grid=(NT,),
        in_specs=[pl.BlockSpec((1, 1, LCH), lambda t, *_: (t, 0, 0), memory_space=pltpu.SMEM),
                  pl.BlockSpec((1, 1, LCH), lambda t, *_: (jnp.minimum(t + 1, NT - 1), 0, 0),
                               memory_space=pltpu.SMEM),
                  tile(RW), tile(RW), pl.BlockSpec((1, 1, RW), lambda t, *_: (t, 0, 0)), tile(D),
                  _mod_spec(layer, 5, row),
                  pl.BlockSpec((1, 1, D), lambda t, *_: (NLAYER if last else nxt, 0, 0)),
                  _mod_spec(nxt, 0, row), _mod_spec(nxt, 1, row),
                  pl.BlockSpec(memory_space=pl.ANY)],
        out_specs=out_specs,
        scratch_shapes=[pltpu.VMEM((2, LROWS, DH), U32), pltpu.SemaphoreType.DMA((2,))],
    )
    return pl.pallas_call(
        functools.partial(_combine_kernel, last),
        grid_spec=grid_spec,
        out_shape=out_shape,
        compiler_params=_cp(("arbitrary",)),
        name="final" if last else "combine",
    )(plan["nloc"], plan["dch"], plan["dch"], meta, wts, plan["lrow"], hmid, mods3, norm_g3, mods3, mods3, ys)


def kernel(x, c, ctx, c_ctx, w_mod, b_mod, norm1_g, norm2_g, w_in, p_fourier, gmlp_norm_g, gmlp_ws,
           gmlp_bs, p_gmlp, gla_w_a2, gla_b_a, gla_norm_g, p_gla, w_out, router_group_w, router_group_b,
           router_expert_w, router_expert_b, expert_w_gate, expert_w_up, expert_w_down, final_norm_g):
    cvec = jnp.concatenate([c, c_ctx[None, :], jnp.zeros((3, D), F32)], axis=0)
    wa_pad = jnp.stack([jnp.pad(gla_w_a2[:, 0], ((0, 0), (0, LRW - LRANK), (0, 0))),
                        jnp.pad(gla_w_a2[:, 1], ((0, 0), (LRANK, LRW - 2 * LRANK), (0, 0)))], axis=1)
    wa_cat = wa_pad.reshape(NLAYER, 2, LRW, LH, LDK).transpose(0, 3, 2, 1, 4).reshape(NLAYER, LH, LRW, 2 * LDK)
    wa_cat = wa_cat.astype(BF16)
    ba_cat = gla_b_a.reshape(NLAYER, 2, LH, LDK).transpose(0, 2, 1, 3).reshape(NLAYER, LH, 1, 2 * LDK)
    norm1_g3 = jnp.concatenate([norm1_g, final_norm_g[None, :]], axis=0).reshape(NLAYER + 1, 1, D)
    norm2_g3 = norm2_g.reshape(NLAYER, 1, D)
    gng = gmlp_norm_g.reshape(NLAYER, 1, GDIM)
    ws = gmlp_ws.astype(BF16)
    bsb = jnp.broadcast_to(gmlp_bs[:, :, :, None], (NLAYER, GH, GCH, GCH))
    lng = gla_norm_g.reshape(NLAYER, 1, LVD)
    pf, pg, plw, wo = (p_fourier.astype(BF16), p_gmlp.astype(BF16), p_gla.astype(BF16), w_out.astype(BF16))
    w_r = jnp.pad(jnp.concatenate([router_expert_w, router_group_w], axis=-1),
                  ((0, 0), (0, 0), (0, RW - NEXP - NGRP)))
    wrh = w_r.astype(BF16)
    wrc = jnp.concatenate([wrh, (w_r - wrh.astype(F32)).astype(BF16)], axis=-1)
    br =jnp.pad(jnp.concatenate([router_expert_b, router_group_b], axis=-1),
                 ((0, 0), (0, RW - NEXP - NGRP))).reshape(NLAYER, 1, RW)
    cs_lat, cs_ctx, cc = _dft_consts()

    mods3 = _mods(cvec, w_mod, b_mod).reshape(NLAYER * 8 * 6, 1, D)

    h, xn = _init(x, ctx, norm1_g3, mods3)
    res = None
    for layer in range(NLAYER):
        pm = _inproj(xn, w_in, layer)
        lr = _lrproj(xn, w_in, layer)
        o = _gla(pm, lr, wa_cat, ba_cat, layer)
        yf = _fourier(pm, cs_lat, cs_ctx, cc)
        hmid, xm, logits = _merge(layer, yf, pm, o, h, mods3, norm2_g3, gng, ws, bsb, lng,
                                  pf, pg, plw, wo, wrc, br)
        meta, wts, cnt8 = _route(logits)
        plan = _plan(cnt8)
        xs = _dispatch(plan, xm, meta)
        ys = _experts(plan, xs, expert_w_gate, expert_w_up, expert_w_down, layer)
        res = _combine(layer, plan, ys, meta, wts, hmid, mods3, norm1_g3)
        if layer + 1 < NLAYER:
            h, xn = res
    return res
```

```python
import functools
import math

import numpy as np
import jax
import jax.numpy as jnp
from jax import lax
from jax.experimental import pallas as pl
from jax.experimental.pallas import tpu as pltpu

F32 = jnp.float32
BF16 = jnp.bfloat16
I32 = jnp.int32

D = 2048
NB = 4
SEQ = 2048
NLAYER = 4
CTX = 256
EPS = 1e-6
LB = CTX + SEQ
P = NB * LB
TM = 256
TPB = LB // TM
NT = P // TM

FG, FGD = 4, 128
FDIM = FG * FGD
GH, GHD, GCH = 4, 128, 128
GDIM = GH * GHD
LH, LDK, LDV, LRANK, LTAU, LC = 4, 128, 256, 16, 16.0, 64
LKD, LVD = LH * LDK, LH * LDV
NCH = LB // LC
NCTXCH = CTX // LC

W_ALIGNED = 4608
W_GATE0 = W_ALIGNED + 2 * 16
TN = 1536
NBLK_AL, NBLK_GT = W_ALIGNED // TN, 3 * D // TN
C_G0, C_A, C_ZU, C_ZV, C_Q, C_K, C_V, C_R = 0, 6144, 6656, 7168, 7680, 8192, 8704, 9728
NMAIN = 10752
LRW = 128

NGRP, EPG, NEXP, DEXP = 4, 8, 32, 512
TR = 256
CH = 8
LCH = (2 * TM + NEXP * (CH - 1)) // CH
LROWS = 768
TCH = TR // CH
NCHUNK = 2 * P // CH + NT * NEXP * (CH - 1) // CH + NEXP * (TCH - 1)
NTILE = NCHUNK // TCH + 1
NSLOT = NTILE * TR
RW = 128

VMEM_LIMIT = 56 * 1024 * 1024


def _cp(sem, vmem=VMEM_LIMIT):
    return pltpu.CompilerParams(dimension_semantics=sem, vmem_limit_bytes=vmem)


def _dot(a, b):
    return jnp.dot(a, b, preferred_element_type=F32)


def _dot_t(a, b):
    return lax.dot_general(a, b, (((1,), (1,)), ((), ())), preferred_element_type=F32)


def _dot_lt(a, b):
    return lax.dot_general(a, b, (((0,), (0,)), ((), ())), preferred_element_type=F32)


def _split(x):
    hi = x.astype(BF16)
    lo = (x - hi.astype(F32)).astype(BF16)
    return hi, lo


U32 = jnp.uint32
DH = D // 2


def _pack_bf16_pair(x):
    lo = lax.bitcast_convert_type(x[:, :DH], U32)
    hi = lax.bitcast_convert_type(x[:, DH:], U32)
    return lax.shift_right_logical(lo, jnp.uint32(16)) | (hi & jnp.uint32(0xFFFF0000))


def _unpack_bf16_pair(u):
    lo = lax.bitcast_convert_type(lax.shift_left(u, jnp.uint32(16)), F32)
    hi = lax.bitcast_convert_type(u & jnp.uint32(0xFFFF0000), F32)
    return lo.astype(BF16), hi.astype(BF16)


def _sigmoid(x):
    return 1.0 / (1.0 + jnp.exp(-x))


def _silu(x):
    return x * _sigmoid(x)


def _gelu_tanh(x):
    return 0.5 * x * (1.0 + jnp.tanh(math.sqrt(2.0 / math.pi) * (x + 0.044715 * (x * x * x))))


def _rms(x, g):
    return x * lax.rsqrt(jnp.mean(x * x, axis=-1, keepdims=True) + EPS) * g


def _tile_row(t):
    return jnp.where(t % TPB == 0, 4, t // TPB)


def _mod_spec(layer, comp, row_fn):
    return pl.BlockSpec((1, 1, D), lambda *g: ((layer * 8 + row_fn(*g)) * 6 + comp, 0, 0))


def _mods_kernel(c_ref, w_ref, b_ref, o_ref):
    c = c_ref[...]
    s = _silu(c).astype(BF16)
    o_ref[0] = _dot(s, w_ref[0].astype(BF16)) + b_ref[0]


def _mods(cvec, w_mod, b_mod):
    tn = 1024
    return pl.pallas_call(
        _mods_kernel,
        grid=(NLAYER, 6 * D // tn),
        in_specs=[pl.BlockSpec((8, D), lambda l, j: (0, 0)),
                  pl.BlockSpec((1, D, tn), lambda l, j: (l, 0, j)),
                  pl.BlockSpec((1, 1, tn), lambda l, j: (l, 0, j))],
        out_specs=pl.BlockSpec((1, 8, tn), lambda l, j: (l, 0, j)),
        out_shape=jax.ShapeDtypeStruct((NLAYER, 8, 6 * D), F32),
        compiler_params=_cp(("arbitrary", "arbitrary")),
        name="mods",
    )(cvec, w_mod, b_mod.reshape(NLAYER, 1, 6 * D))


def _init_kernel(x_ref, c_ref, g_ref, sh_ref, sc_ref, h_ref, xn_ref):
    j = pl.program_id(1)

    def emit(v):
        h_ref[...] = v
        xn_ref[...] = (_rms(v, g_ref[0]) * (1.0 + sc_ref[0]) + sh_ref[0]).astype(BF16)

    @pl.when(j == 0)
    def _():
        emit(c_ref[0])

    @pl.when(j > 0)
    def _():
        emit(x_ref[0])


def _init(x, ctx, norm1_g3, mods3):
    row = lambda b, j: jnp.where(j == 0, 4, b)
    return pl.pallas_call(
        _init_kernel,
        grid=(NB, TPB),
        in_specs=[pl.BlockSpec((1, TM, D), lambda b, j: (b, jnp.maximum(j - 1, 0), 0)),
                  pl.BlockSpec((1, CTX, D), lambda b, j: (b, 0, 0)),
                  pl.BlockSpec((1, 1, D), lambda b, j: (0, 0, 0)),
                  _mod_spec(0, 0, row), _mod_spec(0, 1, row)],
        out_specs=[pl.BlockSpec((TM, D), lambda b, j: (b * TPB + j, 0)),
                   pl.BlockSpec((TM, D), lambda b, j: (b * TPB + j, 0))],
        out_shape=[jax.ShapeDtypeStruct((P, D), F32), jax.ShapeDtypeStruct((P, D), BF16)],
        compiler_params=_cp(("arbitrary", "arbitrary")),
        name="init",
    )(x, ctx, norm1_g3, mods3, mods3)


TMP = 1024


GOFF = W_GATE0 - W_ALIGNED


def _inproj_kernel(x_ref, w_ref, wt_ref, o_ref, wb_ref):
    j = pl.program_id(0)

    @pl.when((pl.program_id(1) == 0) & (j < NBLK_AL))
    def _():
        wb_ref[...] = w_ref[0].astype(BF16)

    @pl.when((pl.program_id(1) == 0) & (j >= NBLK_AL))
    def _():
        wide = jnp.concatenate([w_ref[0], wt_ref[0]], axis=0)
        wb_ref[...] = wide[GOFF:GOFF + TN, :].astype(BF16)

    o_ref[...] = _dot_t(x_ref[...], wb_ref[...]).astype(BF16)


def _inproj(xn, w_int, layer):
    nblk = NBLK_AL + NBLK_GT
    tail = lambda j, i: (layer, jnp.where(j >= NBLK_AL, (j + 1) * (TN // GOFF), (NBLK_AL + 1) * (TN // GOFF)), 0)
    return pl.pallas_call(
        _inproj_kernel,
        grid=(nblk, P // TMP),
        in_specs=[pl.BlockSpec((TMP, D), lambda j, i: (i, 0)),
                  pl.BlockSpec((1, TN, D), lambda j, i: (layer, j, 0)),
                  pl.BlockSpec((1, GOFF, D), tail)],
        out_specs=pl.BlockSpec((TMP, TN), lambda j, i: (i, (j + NBLK_GT) % nblk)),
        out_shape=jax.ShapeDtypeStruct((P, NMAIN), BF16),
        scratch_shapes=[pltpu.VMEM((TN, D), BF16)],
        compiler_params=_cp(("arbitrary", "arbitrary")),
        name="inproj",
    )(xn, w_int, w_int)


def _lrproj_kernel(x_ref, w_ref, o_ref):
    o_ref[...] = _dot_t(x_ref[...], w_ref[0].astype(BF16))


def _lrproj(xn, w_int, layer):
    tm = 1024
    return pl.pallas_call(
        _lrproj_kernel,
        grid=(P // tm,),
        in_specs=[pl.BlockSpec((tm, D), lambda i: (i, 0)),
                  pl.BlockSpec((1, LRW, D), lambda i: (layer, W_ALIGNED // LRW, 0))],
        out_specs=pl.BlockSpec((tm, LRW), lambda i: (i, 0)),
        out_shape=jax.ShapeDtypeStruct((P, LRW), F32),
        compiler_params=_cp(("arbitrary",)),
        name="lrproj",
    )(xn, w_int)


def _dft_consts():
    def cs(n):
        k = np.arange(n, dtype=np.int64)
        ang = 2.0 * np.pi * ((k[:, None] * k[None, :]) % n).astype(np.float64) / n
        return np.cos(ang) / math.sqrt(n), np.sin(ang) / math.sqrt(n)

    c_l, s_l = cs(SEQ)
    c_c, s_c = cs(CTX)
    c_g, s_g = cs(FGD)
    cs_lat = np.concatenate([c_l, -s_l], axis=1).astype(BF16)
    cs_ctx = np.concatenate([c_c, -s_c], axis=1).astype(BF16)
    cc = np.concatenate([c_g, s_g], axis=1).astype(BF16)
    return cs_lat, cs_ctx, cc


def _fourier_kernel(a_ref, csl_ref, csc_ref, cc_ref, o_ref, rl_ref, rc_ref):
    j = pl.program_id(1)

    @pl.when(j == 0)
    def _():
        for g in range(FG):
            cols = slice(g * FGD, (g + 1) * FGD)
            t = _dot(a_ref[:, cols], cc_ref[...]).astype(BF16)
            rc_ref[0:CTX, cols] = t[0:CTX, 0:FGD]
            rc_ref[CTX:2 * CTX, cols] = t[0:CTX, FGD:2 * FGD]
            rl_ref[0:SEQ, cols] = t[CTX:LB, 0:FGD]
            rl_ref[SEQ:2 * SEQ, cols] = t[CTX:LB, FGD:2 * FGD]
        o_ref[...] = _dot(csc_ref[...], rc_ref[...]).astype(BF16)

    @pl.when(j > 0)
    def _():
        o_ref[...] = _dot(csl_ref[...], rl_ref[...]).astype(BF16)


def _fourier(pm, cs_lat, cs_ctx, cc):
    return pl.pallas_call(
        _fourier_kernel,
        grid=(NB, TPB),
        in_specs=[pl.BlockSpec((LB, FDIM), lambda b, j: (b, C_A // FDIM)),
                  pl.BlockSpec((TM, 2 * SEQ), lambda b, j: (jnp.maximum(j - 1, 0), 0)),
                  pl.BlockSpec((CTX, 2 * CTX), lambda b, j: (0, 0)),
                  pl.BlockSpec((FGD, 2 * FGD), lambda b, j: (0, 0))],
        out_specs=pl.BlockSpec((TM, FDIM), lambda b, j: (b * TPB + j, 0)),
        out_shape=jax.ShapeDtypeStruct((P, FDIM), BF16),
        scratch_shapes=[pltpu.VMEM((2 * SEQ, FDIM), BF16), pltpu.VMEM((2 * CTX, FDIM), BF16)],
        compiler_params=_cp(("arbitrary", "arbitrary")),
        name="fourier",
    )(pm, cs_lat, cs_ctx, cc)


SB = 256
SBC = SB // LC
NSB = LB // SB


def _gla_kernel(q_ref, k_ref, v_ref, lr_ref, wa_ref, ba_ref, o_ref,
                qd_ref, oacc_ref, ds_ref, gam_ref, sall_ref, sf_ref, sb_ref):
    ri = lax.broadcasted_iota(I32, (SB, SB), 0)
    ci = lax.broadcasted_iota(I32, (SB, SB), 1)
    same = (ri // LC) == (ci // LC)
    tri = (same & (ci <= ri)).astype(BF16)
    keep_f = same & (ci <= ri)
    keep_b = same & (ci > ri)
    rchunk = lax.broadcasted_iota(I32, (SB, LDK), 0) // LC
    scale = LDK ** -0.5
    wa = wa_ref[...]
    ba = ba_ref[...]

    def phase1(sb, carry):
        rows = pl.ds(pl.multiple_of(sb * SB, SB), SB)
        logits = _dot(lr_ref[rows, :].astype(BF16), wa) + ba
        g = (jnp.minimum(logits, 0.0) - jnp.log1p(jnp.exp(-jnp.abs(logits)))) * (1.0 / LTAU)
        g_hi, g_lo = _split(g)
        pre = _dot(tri, g_hi) + _dot(tri, g_lo)
        tot = jnp.concatenate(
            [jnp.broadcast_to(pre[c * LC + LC - 1:c * LC + LC, :], (LC, 2 * LDK)) for c in range(SBC)], axis=0)
        q = q_ref[rows, :].astype(F32) * scale
        k = k_ref[rows, :].astype(F32)
        v = v_ref[rows, :]
        s_sum = None
        kts = []
        for d in range(2):
            cols = slice(d * LDK, (d + 1) * LDK)
            t_d = tot[:, cols]
            b_d = pre[:, cols] if d == 0 else t_d - pre[:, cols] + g[:, cols]
            q_dec = (q * jnp.exp(b_d)).astype(BF16)
            k_inv = (k * jnp.exp(-b_d)).astype(BF16)
            k_tail = (k * jnp.exp(t_d - b_d)).astype(BF16)
            sc = jnp.where(keep_f if d == 0 else keep_b, _dot_t(q_dec, k_inv), 0.0)
            s_sum = sc if s_sum is None else s_sum + sc
            qd_ref[rows, cols] = q_dec
            zero = jnp.zeros_like(k_tail)
            kts += [jnp.where(rchunk == c, k_tail, zero) for c in range(SBC)]
            for c in range(SBC):
                gam_ref[d, sb * SBC + c] = jnp.exp(t_d[c * LC:c * LC + 1, :])
        oacc_ref[rows, :] = _dot(s_sum.astype(BF16), v)
        dst = _dot_lt(v, jnp.concatenate(kts, axis=1))
        for d in range(2):
            for c in range(SBC):
                j = d * SBC + c
                ds_ref[d, sb * SBC + c] = dst[:, j * LDK:(j + 1) * LDK]
        return carry

    lax.fori_loop(0, NSB, phase1, 0, unroll=3)

    sf_ref[...] = jnp.zeros_like(sf_ref)
    sb_ref[...] = jnp.zeros_like(sb_ref)

    def phase2(i, carry):
        nb = jnp.where(i < NCTXCH, NCTXCH - 1 - i, NCH + NCTXCH - 1 - i)
        s_f = sf_ref[...]
        s_b = sb_ref[...]
        sall_ref[i, :, 0:LDK] = s_f.astype(BF16)
        sall_ref[nb, :, LDK:2 * LDK] = s_b.astype(BF16)
        sf_ref[...] = s_f * gam_ref[0, i] + ds_ref[0, i]
        sb_ref[...] = s_b * gam_ref[1, nb] + ds_ref[1, nb]
        return carry

    lax.fori_loop(0, NCH, phase2, 0)

    def phase3(n, carry):
        rows = pl.ds(pl.multiple_of(n * LC, LC), LC)
        o_ref[rows, :] = (oacc_ref[rows, :] + _dot_t(qd_ref[rows, :], sall_ref[n])).astype(BF16)
        return carry

    lax.fori_loop(0, NCH, phase3, 0, unroll=4)


def _gla(pm, lr, wa_cat, ba_cat, layer):
    return pl.pallas_call(
        _gla_kernel,
        grid=(NB, LH),
        in_specs=[pl.BlockSpec((LB, LDK), lambda b, h: (b, C_Q // LDK + h)),
                  pl.BlockSpec((LB, LDK), lambda b, h: (b, C_K // LDK + h)),
                  pl.BlockSpec((LB, LDV), lambda b, h: (b, C_V // LDV + h)),
                  pl.BlockSpec((LB, LRW), lambda b, h: (b, 0)),
                  pl.BlockSpec((None, None, LRW, 2 * LDK), lambda b, h: (layer, h, 0, 0)),
                  pl.BlockSpec((None, None, 1, 2 * LDK), lambda b, h: (layer, h, 0, 0))],
        out_specs=pl.BlockSpec((LB, LDV), lambda b, h: (b, h)),
        out_shape=jax.ShapeDtypeStruct((P, LVD), BF16),
        scratch_shapes=[pltpu.VMEM((LB, 2 * LDK), BF16), pltpu.VMEM((LB, LDV), F32),
                        pltpu.VMEM((2, NCH, LDV, LDK), F32), pltpu.VMEM((2, NCH, 1, LDK), F32),
                        pltpu.VMEM((NCH, LDV, 2 * LDK), BF16),
                        pltpu.VMEM((LDV, LDK), F32), pltpu.VMEM((LDV, LDK), F32)],
        compiler_params=_cp(("arbitrary", "arbitrary")),
        name="gla",
    )(pm, pm, pm, lr, wa_cat, ba_cat)


def _merge_kernel(yf_ref, zu_ref, zv_ref, o_ref, ra_ref, rb_ref, g0_ref, g1_ref, g2_ref, h_ref,
                  gt1_ref, sh2_ref, sc2_ref, n2g_ref, gng_ref, ws_ref, bs_ref, lng_ref,
                  pf_ref, pg_ref, pl_ref, wo_ref, wrc_ref, br_ref,
                  hmid_ref, xm_ref, lg_ref):
    y = _sigmoid(g0_ref[...].astype(F32)) * _dot(yf_ref[...], pf_ref[...])

    u = _gelu_tanh(zu_ref[...].astype(F32))
    v = _rms(_gelu_tanh(zv_ref[...].astype(F32)), gng_ref[...]).astype(BF16)
    chunks = []
    for ch in range(TM // GCH):
        rows = slice(ch * GCH, (ch + 1) * GCH)
        heads = [_dot(ws_ref[hd], v[rows, hd * GHD:(hd + 1) * GHD]) + bs_ref[hd] for hd in range(GH)]
        chunks.append(jnp.concatenate(heads, axis=1))
    s = jnp.concatenate(chunks, axis=0)
    y += _sigmoid(g1_ref[...].astype(F32)) * _dot((u * s).astype(BF16), pg_ref[...])

    o = o_ref[...].astype(F32)
    lng = lng_ref[...]
    heads = [_rms(o[:, hd * LDV:(hd + 1) * LDV], lng[:, hd * LDV:(hd + 1) * LDV]) for hd in range(LH)]
    r = jnp.concatenate([ra_ref[...], rb_ref[...]], axis=1).astype(F32)
    ol = (jnp.concatenate(heads, axis=1) * _silu(r)).astype(BF16)
    y += _sigmoid(g2_ref[...].astype(F32)) * _dot(ol, pl_ref[...])

    hmid = h_ref[...] + gt1_ref[0] * _dot(y.astype(BF16), wo_ref[...])
    hmid_ref[...] = hmid

    xm = _rms(hmid, n2g_ref[0]) * (1.0 + sc2_ref[0]) + sh2_ref[0]
    xm_ref[...] = xm
    xm_hi, xm_lo = _split(xm)
    hh_hl = _dot(xm_hi, wrc_ref[...])
    lg_ref[...] = hh_hl[:, 0:RW] + hh_hl[:, RW:2 * RW] + _dot(xm_lo, wrc_ref[:, 0:RW]) + br_ref[...]


def _merge(layer, yf, pm, o, h, mods3, norm2_g3, gng, ws, bsb, lng, pf, pg, plw, wo, wrc, br):
    row = _tile_row
    tile = lambda w, c: pl.BlockSpec((TM, w), lambda t: (t, c))
    lay3 = lambda a, b: pl.BlockSpec((None, a, b), lambda t: (layer, 0, 0), pipeline_mode=pl.Buffered(1))
    lay4 = lambda a, b, c: pl.BlockSpec((None, a, b, c), lambda t: (layer, 0, 0, 0))
    return pl.pallas_call(
        _merge_kernel,
        grid=(NT,),
        in_specs=[tile(FDIM, 0), tile(GDIM, C_ZU // GDIM), tile(GDIM, C_ZV // GDIM), tile(LVD, 0),
                  tile(LVD // 2, C_R // (LVD // 2)), tile(LVD // 2, C_R // (LVD // 2) + 1),
                  tile(D, 0), tile(D, 1), tile(D, 2), tile(D, 0),
                  _mod_spec(layer, 2, row), _mod_spec(layer, 3, row), _mod_spec(layer, 4, row),
                  pl.BlockSpec((1, 1, D), lambda t: (layer, 0, 0)),
                  lay3(1, GDIM), lay4(GH, GCH, GCH), lay4(GH, GCH, GCH), lay3(1, LVD),
                  lay3(FDIM, D), lay3(GDIM, D), lay3(LVD, D), lay3(D, D),
                  lay3(D, 2 * RW), lay3(1, RW)],
        out_specs=[tile(D, 0), tile(D, 0), tile(RW, 0)],
        out_shape=[jax.ShapeDtypeStruct((P, D), F32), jax.ShapeDtypeStruct((P, D), F32),
                   jax.ShapeDtypeStruct((P, RW), F32)],
        compiler_params=_cp(("arbitrary",)),
        name="merge",
    )(yf, pm, pm, o, pm, pm, pm, pm, pm, h, mods3, mods3, mods3, norm2_g3, gng, ws, bsb, lng,
      pf, pg, plw, wo, wrc, br)


def _route_kernel(lg_ref, meta_ref, wts_ref, cnt_ref):
    lg = lg_ref[...]
    lane = lax.broadcasted_iota(I32, (TM, RW), 1)
    lane_f = lane.astype(F32)
    ninf = jnp.float32(-jnp.inf)

    def first_max(x):
        m = jnp.max(x, axis=-1, keepdims=True)
        first = jnp.min(jnp.where(x == m, lane_f, float(RW)), axis=-1, keepdims=True)
        return m, first.astype(I32)

    is_g = (lane >= NEXP) & (lane < NEXP + NGRP)
    gmax, glane = first_max(jnp.where(is_g, lg, ninf))
    gsum = jnp.sum(jnp.where(is_g, jnp.exp(lg - gmax), 0.0), axis=-1, keepdims=True)
    g_w = 1.0 / gsum
    lo = (glane - NEXP) * EPG
    in_grp = (lane >= lo) & (lane < lo + EPG)
    el = jnp.where(in_grp, lg, ninf)
    v1, l1 = first_max(el)
    v2, l2 = first_max(jnp.where(lane == l1, ninf, el))
    e = jnp.exp(v2 - v1)
    w1 = g_w / (1.0 + e)
    w2 = g_w * e / (1.0 + e)

    hit1 = lane == l1
    hit2 = lane == l2
    m = (hit1 | hit2).astype(BF16)
    ri = lax.broadcasted_iota(I32, (TM, TM), 0)
    ci = lax.broadcasted_iota(I32, (TM, TM), 1)
    before = _dot((ci < ri).astype(BF16), m)
    r1 = jnp.sum(jnp.where(hit1, before, 0.0), axis=-1, keepdims=True).astype(I32)
    r2 = jnp.sum(jnp.where(hit2, before, 0.0), axis=-1, keepdims=True).astype(I32)
    total = jnp.sum(m.astype(F32), axis=0, keepdims=True)
    cnt_ref[...] = jnp.broadcast_to(total, cnt_ref.shape).astype(I32)
    meta_ref[...] = jnp.where(lane == 0, l1, jnp.where(lane == 1, l2, jnp.where(lane == 2, r1,
                              jnp.where(lane == 3, r2, 0))))
    wts_ref[...] = jnp.where(lane == 0, w1, jnp.where(lane == 1, w2, 0.0))


def _route(logits):
    return pl.pallas_call(
        _route_kernel,
        grid=(NT,),
        in_specs=[pl.BlockSpec((TM, RW), lambda t: (t, 0))],
        out_specs=[pl.BlockSpec((TM, RW), lambda t: (t, 0)), pl.BlockSpec((TM, RW), lambda t: (t, 0)),
                   pl.BlockSpec((8, RW), lambda t: (t, 0))],
        out_shape=[jax.ShapeDtypeStruct((P, RW), I32), jax.ShapeDtypeStruct((P, RW), F32),
                   jax.ShapeDtypeStruct((NT * 8, RW), I32)],
        compiler_params=_cp(("arbitrary",)),
        name="route",
    )(logits)


def _plan(cnt8):
    cnt = cnt8.reshape(NT, 8, RW)[:, 0, :NEXP]
    c8 = (cnt + (CH - 1)) // CH
    lend = jnp.cumsum(c8, axis=1)
    lstart = lend - c8
    nloc = lend[:, -1]
    reg = jnp.sum(c8, axis=0)
    rpad = (reg + (TCH - 1)) // TCH * TCH
    rend = jnp.cumsum(rpad)
    rstart = rend - rpad
    gbase = rstart[None, :] + jnp.cumsum(c8, axis=0) - c8
    j = jnp.arange(LCH, dtype=I32)
    owner = jnp.sum((lend[:, None, :] <= j[None, :, None]).astype(I32), axis=2)
    sel = (owner[:, :, None] == jnp.arange(NEXP, dtype=I32)[None, None, :]).astype(I32)
    dch = jnp.sum(sel * (gbase - lstart)[:, None, :], axis=2) + j[None, :]
    n_act = rend[-1] // TCH
    tid = jnp.minimum(jnp.arange(NTILE, dtype=I32), n_act - 1)
    tile_expert = jnp.minimum(jnp.sum((rend[None, :] <= (tid * TCH)[:, None]).astype(I32), axis=1), NEXP - 1)
    nxt_tile = jnp.sum((tile_expert[:, None] == jnp.arange(NEXP, dtype=I32)[None, :]).astype(I32)
                       * (rend // TCH)[None, :], axis=1)
    nxt_onehot = (nxt_tile[:, None] == jnp.arange(NTILE, dtype=I32)[None, :]).astype(I32)
    next_expert = jnp.where(nxt_tile < n_act, jnp.sum(nxt_onehot * tile_expert[None, :], axis=1), -1)
    lrow = jnp.pad((lstart * CH).astype(F32), ((0, 0), (0, RW - NEXP))).reshape(NT, 1, RW)
    return dict(nloc=nloc.astype(I32), dch=dch.astype(I32).reshape(NT, 1, LCH), lrow=lrow,
                pstart=(rstart + reg).astype(I32), npad=(rpad - reg).astype(I32),
                tile_expert=tile_expert.astype(I32), next_expert=next_expert.astype(I32),
                n_act=n_act.reshape(1).astype(I32))


def _local_pos(meta, lrow):
    lane = lax.broadcasted_iota(I32, (TM, RW), 1)
    meta_f = meta.astype(F32)
    col = lambda k: jnp.sum(jnp.where(lane == k, meta_f, 0.0), axis=-1, keepdims=True).astype(I32)
    l1, l2, r1, r2 = col(0), col(1), col(2), col(3)
    off = lambda l: jnp.sum(jnp.where(lane == l, lrow, 0.0), axis=-1, keepdims=True).astype(I32)
    return off(l1) + r1, off(l2) + r2


def _dispatch_kernel(nloc_ref, pstart_ref, npad_ref, na_ref, dch_ref, x_ref, meta_ref, lrow_ref, xs_hbm,
                     xloc, zbuf, sem, zsem):
    t = pl.program_id(0)
    slot = t % 2

    def chunk_copy(buf_slot, j, g):
        return pltpu.make_async_copy(xloc.at[buf_slot, pl.ds(pl.multiple_of(j * CH, CH), CH)],
                                     xs_hbm.at[pl.ds(pl.multiple_of(g * CH, CH), CH)], sem.at[buf_slot])

    @pl.when(t == 0)
    def _():
        zbuf[...] = jnp.zeros_like(zbuf)

        def zero_chunks(first, n):
            zcopy = lambda c: pltpu.make_async_copy(
                zbuf, xs_hbm.at[pl.ds(pl.multiple_of((first + c) * CH, CH), CH)], zsem.at[0])
            lax.fori_loop(0, n, lambda c, z: (zcopy(c).start(), z)[1], 0)
            lax.fori_loop(0, n, lambda c, z: (zcopy(0).wait(), z)[1], 0)

        for e in range(NEXP):
            zero_chunks(pstart_ref[e], npad_ref[e])
        zero_chunks(na_ref[0] * TCH, NTILE * TCH - na_ref[0] * TCH)

    p1, p2 = _local_pos(meta_ref[...], lrow_ref[0])
    pos = lax.broadcasted_iota(I32, (TM, LROWS), 1)
    sel = ((pos == p1) | (pos == p2)).astype(BF16)
    xloc[slot] = _pack_bf16_pair(_dot_lt(sel, x_ref[...].astype(BF16)))

    @pl.when(t > 0)
    def _():
        lax.fori_loop(0, nloc_ref[t - 1], lambda j, z: (chunk_copy(1 - slot, 0, 0).wait(), z)[1], 0)

    lax.fori_loop(0, nloc_ref[t], lambda j, z: (chunk_copy(slot, j, dch_ref[0, 0, j]).start(), z)[1], 0)

    @pl.when(t == NT - 1)
    def _():
        lax.fori_loop(0, nloc_ref[t], lambda j, z: (chunk_copy(slot, 0, 0).wait(), z)[1], 0)


def _dispatch(plan, xm, meta):
    grid_spec = pltpu.PrefetchScalarGridSpec(
        num_scalar_prefetch=4,
        grid=(NT,),
        in_specs=[pl.BlockSpec((1, 1, LCH), lambda t, *_: (t, 0, 0), memory_space=pltpu.SMEM),
                  pl.BlockSpec((TM, D), lambda t, *_: (t, 0)),
                  pl.BlockSpec((TM, RW), lambda t, *_: (t, 0)),
                  pl.BlockSpec((1, 1, RW), lambda t, *_: (t, 0, 0))],
        out_specs=pl.BlockSpec(memory_space=pl.ANY),
        scratch_shapes=[pltpu.VMEM((2, LROWS, DH), U32), pltpu.VMEM((CH, DH), U32),
                        pltpu.SemaphoreType.DMA((2,)), pltpu.SemaphoreType.DMA((1,))],
    )
    return pl.pallas_call(
        _dispatch_kernel,
        grid_spec=grid_spec,
        out_shape=jax.ShapeDtypeStruct((NSLOT, DH), U32),
        compiler_params=_cp(("arbitrary",)),
        name="dispatch",
    )(plan["nloc"], plan["pstart"], plan["npad"], plan["n_act"], plan["dch"], xm, meta, plan["lrow"])


def _experts_kernel(layer, te_ref, nx_ref, na_ref, x_ref, wg_hbm, wu_hbm, wd_hbm, y_ref,
                    wgf, wuf, wdf, wgb, wub, wdb, wsem):
    i = pl.program_id(0)

    def weight_copies(e):
        return (pltpu.make_async_copy(wg_hbm.at[layer, e], wgf, wsem.at[0]),
                pltpu.make_async_copy(wu_hbm.at[layer, e], wuf, wsem.at[1]),
                pltpu.make_async_copy(wd_hbm.at[layer, e], wdf, wsem.at[2]))

    @pl.when(i < na_ref[0])
    def _():
        e = te_ref[i]
        first = jnp.logical_or(i == 0, e != te_ref[jnp.maximum(i - 1, 0)])

        @pl.when(i == 0)
        def _():
            for cp in weight_copies(e):
                cp.start()

        @pl.when(first)
        def _():
            for cp in weight_copies(e):
                cp.wait()
            wgb[...] = wgf[...].astype(BF16)
            wub[...] = wuf[...].astype(BF16)
            wdb[...] = wdf[...].astype(BF16)

            @pl.when(nx_ref[i] >= 0)
            def _():
                for cp in weight_copies(nx_ref[i]):
                    cp.start()

        x_lo, x_hi = _unpack_bf16_pair(x_ref[...])
        hg = _dot(x_lo, wgb[0:DH, :]) + _dot(x_hi, wgb[DH:D, :])
        hu = _dot(x_lo, wub[0:DH, :]) + _dot(x_hi, wub[DH:D, :])
        act = (_silu(hg) * hu).astype(BF16)
        y = _dot(act, wdb[...])
        y_ref[...] = _pack_bf16_pair(y.astype(BF16).astype(F32))

    @pl.when(i >= na_ref[0])
    def _():
        y_ref[...] = jnp.zeros_like(y_ref)


def _experts(plan, xs, wg, wu, wd, layer):
    hbm = pl.BlockSpec(memory_space=pl.ANY)
    grid_spec = pltpu.PrefetchScalarGridSpec(
        num_scalar_prefetch=3,
        grid=(NTILE,),
        in_specs=[pl.BlockSpec((TR, DH), lambda i, te, nx, na: (jnp.minimum(i, na[0] - 1), 0)), hbm, hbm, hbm],
        out_specs=pl.BlockSpec((TR, DH), lambda i, te, nx, na: (i, 0)),
        scratch_shapes=[pltpu.VMEM((D, DEXP), F32), pltpu.VMEM((D, DEXP), F32), pltpu.VMEM((DEXP, D), F32),
                        pltpu.VMEM((D, DEXP), BF16), pltpu.VMEM((D, DEXP), BF16), pltpu.VMEM((DEXP, D), BF16),
                        pltpu.SemaphoreType.DMA((3,))],
    )
    return pl.pallas_call(
        functools.partial(_experts_kernel, layer),
        grid_spec=grid_spec,
        out_shape=jax.ShapeDtypeStruct((NSLOT, DH), U32),
        compiler_params=_cp(("arbitrary",)),
        name="experts",
    )(plan["tile_expert"], plan["next_expert"], plan["n_act"], xs, wg, wu, wd)


def _combine_kernel(last, nloc_ref, dch_ref, dchn_ref, meta_ref, w_ref, lrow_ref, h_ref, gt2_ref, g_ref,
                    sh_ref, sc_ref, ys_hbm, *rest):
    if last:
        out_ref, yloc, sem = rest
    else:
        h_out, xn_out, yloc, sem = rest
    t = pl.program_id(0)
    slot = t % 2

    def fetch(idx_ref, buf_slot, n):
        def body(j, z):
            g = idx_ref[0, 0, j]
            pltpu.make_async_copy(ys_hbm.at[pl.ds(pl.multiple_of(g * CH, CH), CH)],
                                  yloc.at[buf_slot, pl.ds(pl.multiple_of(j * CH, CH), CH)],
                                  sem.at[buf_slot]).start()
            return z
        lax.fori_loop(0, n, body, 0)

    @pl.when(t == 0)
    def _():
        yloc[...] = jnp.zeros_like(yloc)
        fetch(dch_ref, 0, nloc_ref[0])

    @pl.when(t + 1 < NT)
    def _():
        fetch(dchn_ref, 1 - slot, nloc_ref[jnp.minimum(t + 1, NT - 1)])

    def wait_one(j, z):
        pltpu.make_async_copy(ys_hbm.at[pl.ds(0, CH)], yloc.at[slot, pl.ds(0, CH)], sem.at[slot]).wait()
        return z
    lax.fori_loop(0, nloc_ref[t], wait_one, 0)

    p1, p2 = _local_pos(meta_ref[...], lrow_ref[0])
    pos = lax.broadcasted_iota(I32, (TM, LROWS), 1)
    y_lo, y_hi = _unpack_bf16_pair(yloc[slot])
    w = w_ref[...]
    q1 = (pos == p1).astype(BF16)
    q2 = (pos == p2).astype(BF16)
    pick = lambda y: w[:, 0:1] * _dot(q1, y) + w[:, 1:2] * _dot(q2, y)
    moe = jnp.concatenate([pick(y_lo), pick(y_hi)], axis=1)
    h = h_ref[...] + gt2_ref[0] * moe
    if last:
        @pl.when(t % TPB > 0)
        def _():
            out_ref[0] = _rms(h, g_ref[0])
    else:
        h_out[...] = h
        xn_out[...] = (_rms(h, g_ref[0]) * (1.0 + sc_ref[0]) + sh_ref[0]).astype(BF16)


def _combine(layer, plan, ys, meta, wts, hmid, mods3, norm_g3):
    last = layer == NLAYER - 1
    row = lambda t, *_: _tile_row(t)
    nxt = 0 if last else layer + 1
    tile = lambda w: pl.BlockSpec((TM, w), lambda t, *_: (t, 0))
    if last:
        out_specs = pl.BlockSpec((1, TM, D), lambda t, *_: (t // TPB, jnp.maximum(t % TPB - 1, 0), 0))
        out_shape = jax.ShapeDtypeStruct((NB, SEQ, D), F32)
    else:
        out_specs = [tile(D), tile(D)]
        out_shape = [jax.ShapeDtypeStruct((P, D), F32), jax.ShapeDtypeStruct((P, D), BF16)]
    grid_spec = pltpu.PrefetchScalarGridSpec(
        num_scalar_prefetch=1,
        grid=(NT,),
        in_specs=[pl.BlockSpec((1, 1, LCH), lambda t, *_: (t, 0, 0), memory_space=pltpu.SMEM),
                  pl.BlockSpec((1, 1, LCH), lambda t, *_: (jnp.minimum(t + 1, NT - 1), 0, 0),
                               memory_space=pltpu.SMEM),
                  tile(RW), tile(RW), pl.BlockSpec((1, 1, RW), lambda t, *_: (t, 0, 0)), tile(D),
                  _mod_spec(layer, 5, row),
                  pl.BlockSpec((1, 1, D), lambda t, *_: (NLAYER if last else nxt, 0, 0)),
                  _mod_spec(nxt, 0, row), _mod_spec(nxt, 1, row),
                  pl.BlockSpec(memory_space=pl.ANY)],
        out_specs=out_specs,
        scratch_shapes=[pltpu.VMEM((2, LROWS, DH), U32), pltpu.SemaphoreType.DMA((2,))],
    )
    return pl.pallas_call(
        functools.partial(_combine_kernel, last),
        grid_spec=grid_spec,
        out_shape=out_shape,
        compiler_params=_cp(("arbitrary",)),
        name="final" if last else "combine",
    )(plan["nloc"], plan["dch"], plan["dch"], meta, wts, plan["lrow"], hmid, mods3, norm_g3, mods3, mods3, ys)


def kernel(x, c, ctx, c_ctx, w_mod, b_mod, norm1_g, norm2_g, w_in, p_fourier, gmlp_norm_g, gmlp_ws,
           gmlp_bs, p_gmlp, gla_w_a2, gla_b_a, gla_norm_g, p_gla, w_out, router_group_w, router_group_b,
           router_expert_w, router_expert_b, expert_w_gate, expert_w_up, expert_w_down, final_norm_g):
    cvec = jnp.concatenate([c, c_ctx[None, :], jnp.zeros((3, D), F32)], axis=0)
    w_int = jnp.swapaxes(w_in, 1, 2)
    wa_pad = jnp.stack([jnp.pad(gla_w_a2[:, 0], ((0, 0), (0, LRW - LRANK), (0, 0))),
                        jnp.pad(gla_w_a2[:, 1], ((0, 0), (LRANK, LRW - 2 * LRANK), (0, 0)))], axis=1)
    wa_cat = wa_pad.reshape(NLAYER, 2, LRW, LH, LDK).transpose(0, 3, 2, 1, 4).reshape(NLAYER, LH, LRW, 2 * LDK)
    wa_cat = wa_cat.astype(BF16)
    ba_cat = gla_b_a.reshape(NLAYER, 2, LH, LDK).transpose(0, 2, 1, 3).reshape(NLAYER, LH, 1, 2 * LDK)
    norm1_g3 = jnp.concatenate([norm1_g, final_norm_g[None, :]], axis=0).reshape(NLAYER + 1, 1, D)
    norm2_g3 = norm2_g.reshape(NLAYER, 1, D)
    gng = gmlp_norm_g.reshape(NLAYER, 1, GDIM)
    ws = gmlp_ws.astype(BF16)
    bsb = jnp.broadcast_to(gmlp_bs[:, :, :, None], (NLAYER, GH, GCH, GCH))
    lng = gla_norm_g.reshape(NLAYER, 1, LVD)
    pf, pg, plw, wo = (p_fourier.astype(BF16), p_gmlp.astype(BF16), p_gla.astype(BF16), w_out.astype(BF16))
    w_r = jnp.pad(jnp.concatenate([router_expert_w, router_group_w], axis=-1),
                  ((0, 0), (0, 0), (0, RW - NEXP - NGRP)))
    wrh = w_r.astype(BF16)
    wrc = jnp.concatenate([wrh, (w_r - wrh.astype(F32)).astype(BF16)], axis=-1)
    br =jnp.pad(jnp.concatenate([router_expert_b, router_group_b], axis=-1),
                 ((0, 0), (0, RW - NEXP - NGRP))).reshape(NLAYER, 1, RW)
    cs_lat, cs_ctx, cc = _dft_consts()

    mods3 = _mods(cvec, w_mod, b_mod).reshape(NLAYER * 8 * 6, 1, D)

    h, xn = _init(x, ctx, norm1_g3, mods3)
    res = None
    for layer in range(NLAYER):
        pm = _inproj(xn, w_int, layer)
        lr = _lrproj(xn, w_int, layer)
        o = _gla(pm, lr, wa_cat, ba_cat, layer)
        yf = _fourier(pm, cs_lat, cs_ctx, cc)
        hmid, xm, logits = _merge(layer, yf, pm, o, h, mods3, norm2_g3, gng, ws, bsb, lng,
                                  pf, pg, plw, wo, wrc, br)
        meta, wts, cnt8 = _route(logits)
        plan = _plan(cnt8)
        xs = _dispatch(plan, xm, meta)
        ys = _experts(plan, xs, expert_w_gate, expert_w_up, expert_w_down, layer)
        res = _combine(layer, plan, ys, meta, wts, hmid, mods3, norm1_g3)
        if layer + 1 < NLAYER:
            h, xn = res
    return res
```

```python
import functools
import math

import numpy as np
import jax
import jax.numpy as jnp
from jax import lax
from jax.experimental import pallas as pl
from jax.experimental.pallas import tpu as pltpu

F32 = jnp.float32
BF16 = jnp.bfloat16
I32 = jnp.int32

D = 2048
NB = 4
SEQ = 2048
NLAYER = 4
CTX = 256
EPS = 1e-6
LB = CTX + SEQ
P = NB * LB
TM = 256
TPB = LB // TM
NT = P // TM

FG, FGD = 4, 128
FDIM = FG * FGD
GH, GHD, GCH = 4, 128, 128
GDIM = GH * GHD
LH, LDK, LDV, LRANK, LTAU, LC = 4, 128, 256, 16, 16.0, 64
LKD, LVD = LH * LDK, LH * LDV
NCH = LB // LC
NCTXCH = CTX // LC

W_ALIGNED = 4608
W_GATE0 = W_ALIGNED + 2 * 16
TN = 1536
NBLK_AL, NBLK_GT = W_ALIGNED // TN, 3 * D // TN
C_G0, C_A, C_ZU, C_ZV, C_Q, C_K, C_V, C_R = 0, 6144, 6656, 7168, 7680, 8192, 8704, 9728
NMAIN = 10752
LRW = 128

NGRP, EPG, NEXP, DEXP = 4, 8, 32, 512
TR = 256
CH = 8
LCH = (2 * TM + NEXP * (CH - 1)) // CH
LROWS = 768
TCH = TR // CH
NCHUNK = 2 * P // CH + NT * NEXP * (CH - 1) // CH + NEXP * (TCH - 1)
NTILE = NCHUNK // TCH + 1
NSLOT = NTILE * TR
RW = 128

VMEM_LIMIT = 56 * 1024 * 1024


def _cp(sem, vmem=VMEM_LIMIT):
    return pltpu.CompilerParams(dimension_semantics=sem, vmem_limit_bytes=vmem)


def _dot(a, b):
    return jnp.dot(a, b, preferred_element_type=F32)


def _dot_t(a, b):
    return lax.dot_general(a, b, (((1,), (1,)), ((), ())), preferred_element_type=F32)


def _dot_lt(a, b):
    return lax.dot_general(a, b, (((0,), (0,)), ((), ())), preferred_element_type=F32)


def _split(x):
    hi = x.astype(BF16)
    lo = (x - hi.astype(F32)).astype(BF16)
    return hi, lo


U32 = jnp.uint32
DH = D // 2


def _pack_bf16_pair(x):
    lo = lax.bitcast_convert_type(x[:, :DH], U32)
    hi = lax.bitcast_convert_type(x[:, DH:], U32)
    return lax.shift_right_logical(lo, jnp.uint32(16)) | (hi & jnp.uint32(0xFFFF0000))


def _unpack_bf16_pair(u):
    lo = lax.bitcast_convert_type(lax.shift_left(u, jnp.uint32(16)), F32)
    hi = lax.bitcast_convert_type(u & jnp.uint32(0xFFFF0000), F32)
    return lo.astype(BF16), hi.astype(BF16)


def _sigmoid(x):
    return 1.0 / (1.0 + jnp.exp(-x))


def _silu(x):
    return x * _sigmoid(x)


def _gelu_tanh(x):
    return 0.5 * x * (1.0 + jnp.tanh(math.sqrt(2.0 / math.pi) * (x + 0.044715 * (x * x * x))))


def _rms(x, g):
    return x * lax.rsqrt(jnp.mean(x * x, axis=-1, keepdims=True) + EPS) * g


def _tile_row(t):
    return jnp.where(t % TPB == 0, 4, t // TPB)


def _mod_spec(layer, comp, row_fn):
    return pl.BlockSpec((1, 1, D), lambda *g: ((layer * 8 + row_fn(*g)) * 6 + comp, 0, 0))


def _mods_kernel(c_ref, w_ref, b_ref, o_ref):
    c = c_ref[...]
    s = _silu(c).astype(BF16)
    o_ref[0] = _dot(s, w_ref[0].astype(BF16)) + b_ref[0]


def _mods(cvec, w_mod, b_mod):
    tn = 1024
    return pl.pallas_call(
        _mods_kernel,
        grid=(NLAYER, 6 * D // tn),
        in_specs=[pl.BlockSpec((8, D), lambda l, j: (0, 0)),
                  pl.BlockSpec((1, D, tn), lambda l, j: (l, 0, j)),
                  pl.BlockSpec((1, 1, tn), lambda l, j: (l, 0, j))],
        out_specs=pl.BlockSpec((1, 8, tn), lambda l, j: (l, 0, j)),
        out_shape=jax.ShapeDtypeStruct((NLAYER, 8, 6 * D), F32),
        compiler_params=_cp(("arbitrary", "arbitrary")),
        name="mods",
    )(cvec, w_mod, b_mod.reshape(NLAYER, 1, 6 * D))


def _init_kernel(x_ref, c_ref, g_ref, sh_ref, sc_ref, h_ref, xn_ref):
    j = pl.program_id(1)

    def emit(v):
        h_ref[...] = v
        xn_ref[...] = (_rms(v, g_ref[0]) * (1.0 + sc_ref[0]) + sh_ref[0]).astype(BF16)

    @pl.when(j == 0)
    def _():
        emit(c_ref[0])

    @pl.when(j > 0)
    def _():
        emit(x_ref[0])


def _init(x, ctx, norm1_g3, mods3):
    row = lambda b, j: jnp.where(j == 0, 4, b)
    return pl.pallas_call(
        _init_kernel,
        grid=(NB, TPB),
        in_specs=[pl.BlockSpec((1, TM, D), lambda b, j: (b, jnp.maximum(j - 1, 0), 0)),
                  pl.BlockSpec((1, CTX, D), lambda b, j: (b, 0, 0)),
                  pl.BlockSpec((1, 1, D), lambda b, j: (0, 0, 0)),
                  _mod_spec(0, 0, row), _mod_spec(0, 1, row)],
        out_specs=[pl.BlockSpec((TM, D), lambda b, j: (b * TPB + j, 0)),
                   pl.BlockSpec((TM, D), lambda b, j: (b * TPB + j, 0))],
        out_shape=[jax.ShapeDtypeStruct((P, D), F32), jax.ShapeDtypeStruct((P, D), BF16)],
        compiler_params=_cp(("arbitrary", "arbitrary")),
        name="init",
    )(x, ctx, norm1_g3, mods3, mods3)


TMP = 1024


GOFF = W_GATE0 - W_ALIGNED


def _inproj_kernel(x_ref, w_ref, wt_ref, o_ref, wb_ref):
    j = pl.program_id(0)

    @pl.when((pl.program_id(1) == 0) & (j < NBLK_AL))
    def _():
        wb_ref[...] = w_ref[0].astype(BF16)

    @pl.when((pl.program_id(1) == 0) & (j >= NBLK_AL))
    def _():
        wide = jnp.concatenate([w_ref[0], wt_ref[0]], axis=0)
        wb_ref[...] = wide[GOFF:GOFF + TN, :].astype(BF16)

    o_ref[...] = _dot_t(x_ref[...], wb_ref[...]).astype(BF16)


def _inproj(xn, w_int, layer):
    nblk = NBLK_AL + NBLK_GT
    tail = lambda j, i: (layer, jnp.where(j >= NBLK_AL, (j + 1) * (TN // GOFF), (NBLK_AL + 1) * (TN // GOFF)), 0)
    return pl.pallas_call(
        _inproj_kernel,
        grid=(nblk, P // TMP),
        in_specs=[pl.BlockSpec((TMP, D), lambda j, i: (i, 0)),
                  pl.BlockSpec((1, TN, D), lambda j, i: (layer, j, 0)),
                  pl.BlockSpec((1, GOFF, D), tail)],
        out_specs=pl.BlockSpec((TMP, TN), lambda j, i: (i, (j + NBLK_GT) % nblk)),
        out_shape=jax.ShapeDtypeStruct((P, NMAIN), BF16),
        scratch_shapes=[pltpu.VMEM((TN, D), BF16)],
        compiler_params=_cp(("arbitrary", "arbitrary")),
        name="inproj",
    )(xn, w_int, w_int)


def _lrproj_kernel(x_ref, w_ref, o_ref):
    o_ref[...] = _dot_t(x_ref[...], w_ref[0].astype(BF16))


def _lrproj(xn, w_int, layer):
    tm = 1024
    return pl.pallas_call(
        _lrproj_kernel,
        grid=(P // tm,),
        in_specs=[pl.BlockSpec((tm, D), lambda i: (i, 0)),
                  pl.BlockSpec((1, LRW, D), lambda i: (layer, W_ALIGNED // LRW, 0))],
        out_specs=pl.BlockSpec((tm, LRW), lambda i: (i, 0)),
        out_shape=jax.ShapeDtypeStruct((P, LRW), F32),
        compiler_params=_cp(("arbitrary",)),
        name="lrproj",
    )(xn, w_int)


def _dft_consts():
    def cs(n):
        k = np.arange(n, dtype=np.int64)
        ang = 2.0 * np.pi * ((k[:, None] * k[None, :]) % n).astype(np.float64) / n
        return np.cos(ang) / math.sqrt(n), np.sin(ang) / math.sqrt(n)

    c_l, s_l = cs(SEQ)
    c_c, s_c = cs(CTX)
    c_g, s_g = cs(FGD)
    cs_lat = np.concatenate([c_l, -s_l], axis=1).astype(BF16)
    cs_ctx = np.concatenate([c_c, -s_c], axis=1).astype(BF16)
    cc = np.concatenate([c_g, s_g], axis=1).astype(BF16)
    return cs_lat, cs_ctx, cc


def _fourier_kernel(a_ref, csl_ref, csc_ref, cc_ref, o_ref, rl_ref, rc_ref):
    j = pl.program_id(1)

    @pl.when(j == 0)
    def _():
        for g in range(FG):
            cols = slice(g * FGD, (g + 1) * FGD)
            t = _dot(a_ref[:, cols], cc_ref[...]).astype(BF16)
            rc_ref[0:CTX, cols] = t[0:CTX, 0:FGD]
            rc_ref[CTX:2 * CTX, cols] = t[0:CTX, FGD:2 * FGD]
            rl_ref[0:SEQ, cols] = t[CTX:LB, 0:FGD]
            rl_ref[SEQ:2 * SEQ, cols] = t[CTX:LB, FGD:2 * FGD]
        o_ref[...] = _dot(csc_ref[...], rc_ref[...]).astype(BF16)

    @pl.when(j > 0)
    def _():
        o_ref[...] = _dot(csl_ref[...], rl_ref[...]).astype(BF16)


def _fourier(pm, cs_lat, cs_ctx, cc):
    return pl.pallas_call(
        _fourier_kernel,
        grid=(NB, TPB),
        in_specs=[pl.BlockSpec((LB, FDIM), lambda b, j: (b, C_A // FDIM)),
                  pl.BlockSpec((TM, 2 * SEQ), lambda b, j: (jnp.maximum(j - 1, 0), 0)),
                  pl.BlockSpec((CTX, 2 * CTX), lambda b, j: (0, 0)),
                  pl.BlockSpec((FGD, 2 * FGD), lambda b, j: (0, 0))],
        out_specs=pl.BlockSpec((TM, FDIM), lambda b, j: (b * TPB + j, 0)),
        out_shape=jax.ShapeDtypeStruct((P, FDIM), BF16),
        scratch_shapes=[pltpu.VMEM((2 * SEQ, FDIM), BF16), pltpu.VMEM((2 * CTX, FDIM), BF16)],
        compiler_params=_cp(("arbitrary", "arbitrary")),
        name="fourier",
    )(pm, cs_lat, cs_ctx, cc)


SB = 256
SBC = SB // LC
NSB = LB // SB


def _gla_kernel(q_ref, k_ref, v_ref, lr_ref, wa_ref, ba_ref, o_ref,
                qd_ref, oacc_ref, ds_ref, gam_ref, sall_ref, sf_ref, sb_ref):
    ri = lax.broadcasted_iota(I32, (SB, SB), 0)
    ci = lax.broadcasted_iota(I32, (SB, SB), 1)
    same = (ri // LC) == (ci // LC)
    tri = (same & (ci <= ri)).astype(BF16)
    keep_f = same & (ci <= ri)
    keep_b = same & (ci > ri)
    rchunk = lax.broadcasted_iota(I32, (SB, LDK), 0) // LC
    scale = LDK ** -0.5
    wa = wa_ref[...]
    ba = ba_ref[...]

    def phase1(sb, carry):
        rows = pl.ds(pl.multiple_of(sb * SB, SB), SB)
        logits = _dot(lr_ref[rows, :].astype(BF16), wa) + ba
        g = (jnp.minimum(logits, 0.0) - jnp.log1p(jnp.exp(-jnp.abs(logits)))) * (1.0 / LTAU)
        g_hi, g_lo = _split(g)
        pre = _dot(tri, g_hi) + _dot(tri, g_lo)
        tot = jnp.concatenate(
            [jnp.broadcast_to(pre[c * LC + LC - 1:c * LC + LC, :], (LC, 2 * LDK)) for c in range(SBC)], axis=0)
        q = q_ref[rows, :].astype(F32) * scale
        k = k_ref[rows, :].astype(F32)
        v = v_ref[rows, :]
        s_sum = None
        kts = []
        for d in range(2):
            cols = slice(d * LDK, (d + 1) * LDK)
            t_d = tot[:, cols]
            b_d = pre[:, cols] if d == 0 else t_d - pre[:, cols] + g[:, cols]
            q_dec = (q * jnp.exp(b_d)).astype(BF16)
            k_inv = (k * jnp.exp(-b_d)).astype(BF16)
            k_tail = (k * jnp.exp(t_d - b_d)).astype(BF16)
            sc = jnp.where(keep_f if d == 0 else keep_b, _dot_t(q_dec, k_inv), 0.0)
            s_sum = sc if s_sum is None else s_sum + sc
            qd_ref[rows, cols] = q_dec
            zero = jnp.zeros_like(k_tail)
            kts += [jnp.where(rchunk == c, k_tail, zero) for c in range(SBC)]
            for c in range(SBC):
                gam_ref[d, sb * SBC + c] = jnp.exp(t_d[c * LC:c * LC + 1, :])
        oacc_ref[rows, :] = _dot(s_sum.astype(BF16), v)
        dst = _dot_lt(v, jnp.concatenate(kts, axis=1))
        for d in range(2):
            for c in range(SBC):
                j = d * SBC + c
                ds_ref[d, sb * SBC + c] = dst[:, j * LDK:(j + 1) * LDK]
        return carry

    lax.fori_loop(0, NSB, phase1, 0, unroll=3)

    sf_ref[...] = jnp.zeros_like(sf_ref)
    sb_ref[...] = jnp.zeros_like(sb_ref)

    def phase2(i, carry):
        nb = jnp.where(i < NCTXCH, NCTXCH - 1 - i, NCH + NCTXCH - 1 - i)
        s_f = sf_ref[...]
        s_b = sb_ref[...]
        sall_ref[i, :, 0:LDK] = s_f.astype(BF16)
        sall_ref[nb, :, LDK:2 * LDK] = s_b.astype(BF16)
        sf_ref[...] = s_f * gam_ref[0, i] + ds_ref[0, i]
        sb_ref[...] = s_b * gam_ref[1, nb] + ds_ref[1, nb]
        return carry

    lax.fori_loop(0, NCH, phase2, 0)

    def phase3(n, carry):
        rows = pl.ds(pl.multiple_of(n * LC, LC), LC)
        o_ref[rows, :] = (oacc_ref[rows, :] + _dot_t(qd_ref[rows, :], sall_ref[n])).astype(BF16)
        return carry

    lax.fori_loop(0, NCH, phase3, 0, unroll=4)


def _gla(pm, lr, wa_cat, ba_cat, layer):
    return pl.pallas_call(
        _gla_kernel,
        grid=(NB, LH),
        in_specs=[pl.BlockSpec((LB, LDK), lambda b, h: (b, C_Q // LDK + h)),
                  pl.BlockSpec((LB, LDK), lambda b, h: (b, C_K // LDK + h)),
                  pl.BlockSpec((LB, LDV), lambda b, h: (b, C_V // LDV + h)),
                  pl.BlockSpec((LB, LRW), lambda b, h: (b, 0)),
                  pl.BlockSpec((None, None, LRW, 2 * LDK), lambda b, h: (layer, h, 0, 0)),
                  pl.BlockSpec((None, None, 1, 2 * LDK), lambda b, h: (layer, h, 0, 0))],
        out_specs=pl.BlockSpec((LB, LDV), lambda b, h: (b, h)),
        out_shape=jax.ShapeDtypeStruct((P, LVD), BF16),
        scratch_shapes=[pltpu.VMEM((LB, 2 * LDK), BF16), pltpu.VMEM((LB, LDV), F32),
                        pltpu.VMEM((2, NCH, LDV, LDK), F32), pltpu.VMEM((2, NCH, 1, LDK), F32),
                        pltpu.VMEM((NCH, LDV, 2 * LDK), BF16),
                        pltpu.VMEM((LDV, LDK), F32), pltpu.VMEM((LDV, LDK), F32)],
        compiler_params=_cp(("arbitrary", "arbitrary")),
        name="gla",
    )(pm, pm, pm, lr, wa_cat, ba_cat)


def _merge_kernel(yf_ref, zu_ref, zv_ref, o_ref, ra_ref, rb_ref, g0_ref, g1_ref, g2_ref, h_ref,
                  gt1_ref, sh2_ref, sc2_ref, n2g_ref, gng_ref, ws_ref, bs_ref, lng_ref,
                  pf_ref, pg_ref, pl_ref, wo_ref, wrc_ref, br_ref,
                  hmid_ref, xm_ref, lg_ref):
    y = _sigmoid(g0_ref[...].astype(F32)) * _dot(yf_ref[...], pf_ref[...])

    u = _gelu_tanh(zu_ref[...].astype(F32))
    v = _rms(_gelu_tanh(zv_ref[...].astype(F32)), gng_ref[...]).astype(BF16)
    chunks = []
    for ch in range(TM // GCH):
        rows = slice(ch * GCH, (ch + 1) * GCH)
        heads = [_dot(ws_ref[hd], v[rows, hd * GHD:(hd + 1) * GHD]) + bs_ref[hd] for hd in range(GH)]
        chunks.append(jnp.concatenate(heads, axis=1))
    s = jnp.concatenate(chunks, axis=0)
    y += _sigmoid(g1_ref[...].astype(F32)) * _dot((u * s).astype(BF16), pg_ref[...])

    o = o_ref[...].astype(F32)
    lng = lng_ref[...]
    heads = [_rms(o[:, hd * LDV:(hd + 1) * LDV], lng[:, hd * LDV:(hd + 1) * LDV]) for hd in range(LH)]
    r = jnp.concatenate([ra_ref[...], rb_ref[...]], axis=1).astype(F32)
    ol = (jnp.concatenate(heads, axis=1) * _silu(r)).astype(BF16)
    y += _sigmoid(g2_ref[...].astype(F32)) * _dot(ol, pl_ref[...])

    hmid = h_ref[...] + gt1_ref[0] * _dot(y.astype(BF16), wo_ref[...])
    hmid_ref[...] = hmid

    xm = _rms(hmid, n2g_ref[0]) * (1.0 + sc2_ref[0]) + sh2_ref[0]
    xm_ref[...] = xm
    xm_hi, xm_lo = _split(xm)
    hh_hl = _dot(xm_hi, wrc_ref[...])
    lg_ref[...] = hh_hl[:, 0:RW] + hh_hl[:, RW:2 * RW] + _dot(xm_lo, wrc_ref[:, 0:RW]) + br_ref[...]


def _merge(layer, yf, pm, o, h, mods3, norm2_g3, gng, ws, bsb, lng, pf, pg, plw, wo, wrc, br):
    row = _tile_row
    tile = lambda w, c: pl.BlockSpec((TM, w), lambda t: (t, c))
    lay3 = lambda a, b: pl.BlockSpec((None, a, b), lambda t: (layer, 0, 0), pipeline_mode=pl.Buffered(1))
    lay4 = lambda a, b, c: pl.BlockSpec((None, a, b, c), lambda t: (layer, 0, 0, 0))
    return pl.pallas_call(
        _merge_kernel,
        grid=(NT,),
        in_specs=[tile(FDIM, 0), tile(GDIM, C_ZU // GDIM), tile(GDIM, C_ZV // GDIM), tile(LVD, 0),
                  tile(LVD // 2, C_R // (LVD // 2)), tile(LVD // 2, C_R // (LVD // 2) + 1),
                  tile(D, 0), tile(D, 1), tile(D, 2), tile(D, 0),
                  _mod_spec(layer, 2, row), _mod_spec(layer, 3, row), _mod_spec(layer, 4, row),
                  pl.BlockSpec((1, 1, D), lambda t: (layer, 0, 0)),
                  lay3(1, GDIM), lay4(GH, GCH, GCH), lay4(GH, GCH, GCH), lay3(1, LVD),
                  lay3(FDIM, D), lay3(GDIM, D), lay3(LVD, D), lay3(D, D),
                  lay3(D, 2 * RW), lay3(1, RW)],
        out_specs=[tile(D, 0), tile(D, 0), tile(RW, 0)],
        out_shape=[jax.ShapeDtypeStruct((P, D), F32), jax.ShapeDtypeStruct((P, D), F32),
                   jax.ShapeDtypeStruct((P, RW), F32)],
        compiler_params=_cp(("arbitrary",)),
        name="merge",
    )(yf, pm, pm, o, pm, pm, pm, pm, pm, h, mods3, mods3, mods3, norm2_g3, gng, ws, bsb, lng,
      pf, pg, plw, wo, wrc, br)


def _route_kernel(lg_ref, meta_ref, wts_ref, cnt_ref):
    lg = lg_ref[...]
    lane = lax.broadcasted_iota(I32, (TM, RW), 1)
    lane_f = lane.astype(F32)
    ninf = jnp.float32(-jnp.inf)

    def first_max(x):
        m = jnp.max(x, axis=-1, keepdims=True)
        first = jnp.min(jnp.where(x == m, lane_f, float(RW)), axis=-1, keepdims=True)
        return m, first.astype(I32)

    is_g = (lane >= NEXP) & (lane < NEXP + NGRP)
    gmax, glane = first_max(jnp.where(is_g, lg, ninf))
    gsum = jnp.sum(jnp.where(is_g, jnp.exp(lg - gmax), 0.0), axis=-1, keepdims=True)
    g_w = 1.0 / gsum
    lo = (glane - NEXP) * EPG
    in_grp = (lane >= lo) & (lane < lo + EPG)
    el = jnp.where(in_grp, lg, ninf)
    v1, l1 = first_max(el)
    v2, l2 = first_max(jnp.where(lane == l1, ninf, el))
    e = jnp.exp(v2 - v1)
    w1 = g_w / (1.0 + e)
    w2 = g_w * e / (1.0 + e)

    hit1 = lane == l1
    hit2 = lane == l2
    m = (hit1 | hit2).astype(BF16)
    ri = lax.broadcasted_iota(I32, (TM, TM), 0)
    ci = lax.broadcasted_iota(I32, (TM, TM), 1)
    before = _dot((ci < ri).astype(BF16), m)
    r1 = jnp.sum(jnp.where(hit1, before, 0.0), axis=-1, keepdims=True).astype(I32)
    r2 = jnp.sum(jnp.where(hit2, before, 0.0), axis=-1, keepdims=True).astype(I32)
    total = jnp.sum(m.astype(F32), axis=0, keepdims=True)
    cnt_ref[...] = jnp.broadcast_to(total, cnt_ref.shape).astype(I32)
    meta_ref[...] = jnp.where(lane == 0, l1, jnp.where(lane == 1, l2, jnp.where(lane == 2, r1,
                              jnp.where(lane == 3, r2, 0))))
    wts_ref[...] = jnp.where(lane == 0, w1, jnp.where(lane == 1, w2, 0.0))


def _route(logits):
    return pl.pallas_call(
        _route_kernel,
        grid=(NT,),
        in_specs=[pl.BlockSpec((TM, RW), lambda t: (t, 0))],
        out_specs=[pl.BlockSpec((TM, RW), lambda t: (t, 0)), pl.BlockSpec((TM, RW), lambda t: (t, 0)),
                   pl.BlockSpec((8, RW), lambda t: (t, 0))],
        out_shape=[jax.ShapeDtypeStruct((P, RW), I32), jax.ShapeDtypeStruct((P, RW), F32),
                   jax.ShapeDtypeStruct((NT * 8, RW), I32)],
        compiler_params=_cp(("arbitrary",)),
        name="route",
    )(logits)


def _plan(cnt8):
    cnt = cnt8.reshape(NT, 8, RW)[:, 0, :NEXP]
    c8 = (cnt + (CH - 1)) // CH
    lend = jnp.cumsum(c8, axis=1)
    lstart = lend - c8
    nloc = lend[:, -1]
    reg = jnp.sum(c8, axis=0)
    rpad = (reg + (TCH - 1)) // TCH * TCH
    rend = jnp.cumsum(rpad)
    rstart = rend - rpad
    gbase = rstart[None, :] + jnp.cumsum(c8, axis=0) - c8
    j = jnp.arange(LCH, dtype=I32)
    owner = jnp.sum((lend[:, None, :] <= j[None, :, None]).astype(I32), axis=2)
    sel = (owner[:, :, None] == jnp.arange(NEXP, dtype=I32)[None, None, :]).astype(I32)
    dch = jnp.sum(sel * (gbase - lstart)[:, None, :], axis=2) + j[None, :]
    n_act = rend[-1] // TCH
    tid = jnp.minimum(jnp.arange(NTILE, dtype=I32), n_act - 1)
    tile_expert = jnp.minimum(jnp.sum((rend[None, :] <= (tid * TCH)[:, None]).astype(I32), axis=1), NEXP - 1)
    ex = jnp.arange(NEXP, dtype=I32)
    nonempty = (reg > 0).astype(I32)
    seg_of_e = jnp.cumsum(nonempty) - 1
    nseg = jnp.sum(nonempty)
    seg = jnp.sum((tile_expert[:, None] == ex[None, :]).astype(I32) * seg_of_e[None, :], axis=1)
    seg_expert = jnp.sum(((seg_of_e[None, :] == ex[:, None]) & (reg[None, :] > 0)).astype(I32) * ex[None, :], axis=1)
    ahead = lambda k: jnp.where(seg + k < nseg,
                                jnp.sum(((seg + k)[:, None] == ex[None, :]).astype(I32) * seg_expert[None, :], axis=1), -1)
    lrow = jnp.pad((lstart * CH).astype(F32), ((0, 0), (0, RW - NEXP))).reshape(NT, 1, RW)
    return dict(nloc=nloc.astype(I32), dch=dch.astype(I32).reshape(NT, 1, LCH), lrow=lrow,
                pstart=(rstart + reg).astype(I32), npad=(rpad - reg).astype(I32),
                tile_expert=tile_expert.astype(I32), seg=seg.astype(I32),
                next1=ahead(1).astype(I32), next2=ahead(2).astype(I32),
                n_act=n_act.reshape(1).astype(I32))


def _local_pos(meta, lrow):
    lane = lax.broadcasted_iota(I32, (TM, RW), 1)
    meta_f = meta.astype(F32)
    col = lambda k: jnp.sum(jnp.where(lane == k, meta_f, 0.0), axis=-1, keepdims=True).astype(I32)
    l1, l2, r1, r2 = col(0), col(1), col(2), col(3)
    off = lambda l: jnp.sum(jnp.where(lane == l, lrow, 0.0), axis=-1, keepdims=True).astype(I32)
    return off(l1) + r1, off(l2) + r2


def _dispatch_kernel(nloc_ref, pstart_ref, npad_ref, na_ref, dch_ref, x_ref, meta_ref, lrow_ref, xs_hbm,
                     xloc, zbuf, sem, zsem):
    t = pl.program_id(0)
    slot = t % 2

    def chunk_copy(buf_slot, j, g):
        return pltpu.make_async_copy(xloc.at[buf_slot, pl.ds(pl.multiple_of(j * CH, CH), CH)],
                                     xs_hbm.at[pl.ds(pl.multiple_of(g * CH, CH), CH)], sem.at[buf_slot])

    @pl.when(t == 0)
    def _():
        zbuf[...] = jnp.zeros_like(zbuf)

        def zero_chunks(first, n):
            zcopy = lambda c: pltpu.make_async_copy(
                zbuf, xs_hbm.at[pl.ds(pl.multiple_of((first + c) * CH, CH), CH)], zsem.at[0])
            lax.fori_loop(0, n, lambda c, z: (zcopy(c).start(), z)[1], 0)
            lax.fori_loop(0, n, lambda c, z: (zcopy(0).wait(), z)[1], 0)

        for e in range(NEXP):
            zero_chunks(pstart_ref[e], npad_ref[e])
        zero_chunks(na_ref[0] * TCH, NTILE * TCH - na_ref[0] * TCH)

    p1, p2 = _local_pos(meta_ref[...], lrow_ref[0])
    pos = lax.broadcasted_iota(I32, (TM, LROWS), 1)
    sel = ((pos == p1) | (pos == p2)).astype(BF16)
    xloc[slot] = _pack_bf16_pair(_dot_lt(sel, x_ref[...].astype(BF16)))

    @pl.when(t > 0)
    def _():
        lax.fori_loop(0, nloc_ref[t - 1], lambda j, z: (chunk_copy(1 - slot, 0, 0).wait(), z)[1], 0)

    lax.fori_loop(0, nloc_ref[t], lambda j, z: (chunk_copy(slot, j, dch_ref[0, 0, j]).start(), z)[1], 0)

    @pl.when(t == NT - 1)
    def _():
        lax.fori_loop(0, nloc_ref[t], lambda j, z: (chunk_copy(slot, 0, 0).wait(), z)[1], 0)


def _dispatch(plan, xm, meta):
    grid_spec = pltpu.PrefetchScalarGridSpec(
        num_scalar_prefetch=4,
        grid=(NT,),
        in_specs=[pl.BlockSpec((1, 1, LCH), lambda t, *_: (t, 0, 0), memory_space=pltpu.SMEM),
                  pl.BlockSpec((TM, D), lambda t, *_: (t, 0)),
                  pl.BlockSpec((TM, RW), lambda t, *_: (t, 0)),
                  pl.BlockSpec((1, 1, RW), lambda t, *_: (t, 0, 0))],
        out_specs=pl.BlockSpec(memory_space=pl.ANY),
        scratch_shapes=[pltpu.VMEM((2, LROWS, DH), U32), pltpu.VMEM((CH, DH), U32),
                        pltpu.SemaphoreType.DMA((2,)), pltpu.SemaphoreType.DMA((1,))],
    )
    return pl.pallas_call(
        _dispatch_kernel,
        grid_spec=grid_spec,
        out_shape=jax.ShapeDtypeStruct((NSLOT, DH), U32),
        compiler_params=_cp(("arbitrary",)),
        name="dispatch",
    )(plan["nloc"], plan["pstart"], plan["npad"], plan["n_act"], plan["dch"], xm, meta, plan["lrow"])


def _experts_kernel(layer, te_ref, seg_ref, n1_ref, n2_ref, na_ref, x_ref, wg_hbm, wu_hbm, wd_hbm, y_ref,
                    wgf, wuf, wdf, wgb, wub, wdb, wsem):
    i = pl.program_id(0)

    def weight_copies(e, b):
        return (pltpu.make_async_copy(wg_hbm.at[layer, e], wgf.at[b], wsem.at[b, 0]),
                pltpu.make_async_copy(wu_hbm.at[layer, e], wuf.at[b], wsem.at[b, 1]),
                pltpu.make_async_copy(wd_hbm.at[layer, e], wdf.at[b], wsem.at[b, 2]))

    @pl.when(i < na_ref[0])
    def _():
        e = te_ref[i]
        b = seg_ref[i] % 2
        first = jnp.logical_or(i == 0, e != te_ref[jnp.maximum(i - 1, 0)])

        @pl.when(i == 0)
        def _():
            for cp in weight_copies(e, 0):
                cp.start()

            @pl.when(n1_ref[0] >= 0)
            def _():
                for cp in weight_copies(n1_ref[0], 1):
                    cp.start()

        @pl.when(first)
        def _():
            for cp in weight_copies(e, b):
                cp.wait()
            wgb[...] = wgf[b].astype(BF16)
            wub[...] = wuf[b].astype(BF16)
            wdb[...] = wdf[b].astype(BF16)

            @pl.when(n2_ref[i] >= 0)
            def _():
                for cp in weight_copies(n2_ref[i], b):
                    cp.start()

        x_lo, x_hi = _unpack_bf16_pair(x_ref[...])
        hg = _dot(x_lo, wgb[0:DH, :]) + _dot(x_hi, wgb[DH:D, :])
        hu = _dot(x_lo, wub[0:DH, :]) + _dot(x_hi, wub[DH:D, :])
        act = (_silu(hg) * hu).astype(BF16)
        y = _dot(act, wdb[...])
        y_ref[...] = _pack_bf16_pair(y.astype(BF16).astype(F32))

    @pl.when(i >= na_ref[0])
    def _():
        y_ref[...] = jnp.zeros_like(y_ref)


def _experts(plan, xs, wg, wu, wd, layer):
    hbm = pl.BlockSpec(memory_space=pl.ANY)
    grid_spec = pltpu.PrefetchScalarGridSpec(
        num_scalar_prefetch=5,
        grid=(NTILE,),
        in_specs=[pl.BlockSpec((TR, DH), lambda i, te, sg, n1, n2, na: (jnp.minimum(i, na[0] - 1), 0)),
                  hbm, hbm, hbm],
        out_specs=pl.BlockSpec((TR, DH), lambda i, *_: (i, 0)),
        scratch_shapes=[pltpu.VMEM((2, D, DEXP), F32), pltpu.VMEM((2, D, DEXP), F32), pltpu.VMEM((2, DEXP, D), F32),
                        pltpu.VMEM((D, DEXP), BF16), pltpu.VMEM((D, DEXP), BF16), pltpu.VMEM((DEXP, D), BF16),
                        pltpu.SemaphoreType.DMA((2, 3))],
    )
    return pl.pallas_call(
        functools.partial(_experts_kernel, layer),
        grid_spec=grid_spec,
        out_shape=jax.ShapeDtypeStruct((NSLOT, DH), U32),
        compiler_params=_cp(("arbitrary",)),
        name="experts",
    )(plan["tile_expert"], plan["seg"], plan["next1"], plan["next2"], plan["n_act"], xs, wg, wu, wd)


def _combine_kernel(last, nloc_ref, dch_ref, dchn_ref, meta_ref, w_ref, lrow_ref, h_ref, gt2_ref, g_ref,
                    sh_ref, sc_ref, ys_hbm, *rest):
    if last:
        out_ref, yloc, sem = rest
    else:
        h_out, xn_out, yloc, sem = rest
    t = pl.program_id(0)
    slot = t % 2

    def fetch(idx_ref, buf_slot, n):
        def body(j, z):
            g = idx_ref[0, 0, j]
            pltpu.make_async_copy(ys_hbm.at[pl.ds(pl.multiple_of(g * CH, CH), CH)],
                                  yloc.at[buf_slot, pl.ds(pl.multiple_of(j * CH, CH), CH)],
                                  sem.at[buf_slot]).start()
            return z
        lax.fori_loop(0, n, body, 0)

    @pl.when(t == 0)
    def _():
        yloc[...] = jnp.zeros_like(yloc)
        fetch(dch_ref, 0, nloc_ref[0])

    @pl.when(t + 1 < NT)
    def _():
        fetch(dchn_ref, 1 - slot, nloc_ref[jnp.minimum(t + 1, NT - 1)])

    def wait_one(j, z):
        pltpu.make_async_copy(ys_hbm.at[pl.ds(0, CH)], yloc.at[slot, pl.ds(0, CH)], sem.at[slot]).wait()
        return z
    lax.fori_loop(0, nloc_ref[t], wait_one, 0)

    p1, p2 = _local_pos(meta_ref[...], lrow_ref[0])
    pos = lax.broadcasted_iota(I32, (TM, LROWS), 1)
    y_lo, y_hi = _unpack_bf16_pair(yloc[slot])
    w = w_ref[...]
    q = (jnp.where(pos == p1, w[:, 0:1], 0.0) + jnp.where(pos == p2, w[:, 1:2], 0.0)).astype(BF16)
    moe = jnp.concatenate([_dot(q, y_lo), _dot(q, y_hi)], axis=1)
    h = h_ref[...] + gt2_ref[0] * moe
    if last:
        @pl.when(t % TPB > 0)
        def _():
            out_ref[0] = _rms(h, g_ref[0])
    else:
        h_out[...] = h
        xn_out[...] = (_rms(h, g_ref[0]) * (1.0 + sc_ref[0]) + sh_ref[0]).astype(BF16)


def _combine(layer, plan, ys, meta, wts, hmid, mods3, norm_g3):
    last = layer == NLAYER - 1
    row = lambda t, *_: _tile_row(t)
    nxt = 0 if last else layer + 1
    tile = lambda w: pl.BlockSpec((TM, w), lambda t, *_: (t, 0))
    if last:
        out_specs = pl.BlockSpec((1, TM, D), lambda t, *_: (t // TPB, jnp.maximum(t % TPB - 1, 0), 0))
        out_shape = jax.ShapeDtypeStruct((NB, SEQ, D), F32)
    else:
        out_specs = [tile(D), tile(D)]
        out_shape = [jax.ShapeDtypeStruct((P, D), F32), jax.ShapeDtypeStruct((P, D), BF16)]
    grid_spec = pltpu.PrefetchScalarGridSpec(
        num_scalar_prefetch=1,
        grid=(NT,),
        in_specs=[pl.BlockSpec((1, 1, LCH), lambda t, *_: (t, 0, 0), memory_space=pltpu.SMEM),
                  pl.BlockSpec((1, 1, LCH), lambda t, *_: (jnp.minimum(t + 1, NT - 1), 0, 0),
                               memory_space=pltpu.SMEM),
                  tile(RW), tile(RW), pl.BlockSpec((1, 1, RW), lambda t, *_: (t, 0, 0)), tile(D),
                  _mod_spec(layer, 5, row),
                  pl.BlockSpec((1, 1, D), lambda t, *_: (NLAYER if last else nxt, 0, 0)),
                  _mod_spec(nxt, 0, row), _mod_spec(nxt, 1, row),
                  pl.BlockSpec(memory_space=pl.ANY)],
        out_specs=out_specs,
        scratch_shapes=[pltpu.VMEM((2, LROWS, DH), U32), pltpu.SemaphoreType.DMA((2,))],
    )
    return pl.pallas_call(
        functools.partial(_combine_kernel, last),
        grid_spec=grid_spec,
        out_shape=out_shape,
        compiler_params=_cp(("arbitrary",)),
        name="final" if last else "combine",
    )(plan["nloc"], plan["dch"], plan["dch"], meta, wts, plan["lrow"], hmid, mods3, norm_g3, mods3, mods3, ys)


def kernel(x, c, ctx, c_ctx, w_mod, b_mod, norm1_g, norm2_g, w_in, p_fourier, gmlp_norm_g, gmlp_ws,
           gmlp_bs, p_gmlp, gla_w_a2, gla_b_a, gla_norm_g, p_gla, w_out, router_group_w, router_group_b,
           router_expert_w, router_expert_b, expert_w_gate, expert_w_up, expert_w_down, final_norm_g):
    cvec = jnp.concatenate([c, c_ctx[None, :], jnp.zeros((3, D), F32)], axis=0)
    w_int = jnp.swapaxes(w_in, 1, 2)
    wa_pad = jnp.stack([jnp.pad(gla_w_a2[:, 0], ((0, 0), (0, LRW - LRANK), (0, 0))),
                        jnp.pad(gla_w_a2[:, 1], ((0, 0), (LRANK, LRW - 2 * LRANK), (0, 0)))], axis=1)
    wa_cat = wa_pad.reshape(NLAYER, 2, LRW, LH, LDK).transpose(0, 3, 2, 1, 4).reshape(NLAYER, LH, LRW, 2 * LDK)
    wa_cat = wa_cat.astype(BF16)
    ba_cat = gla_b_a.reshape(NLAYER, 2, LH, LDK).transpose(0, 2, 1, 3).reshape(NLAYER, LH, 1, 2 * LDK)
    norm1_g3 = jnp.concatenate([norm1_g, final_norm_g[None, :]], axis=0).reshape(NLAYER + 1, 1, D)
    norm2_g3 = norm2_g.reshape(NLAYER, 1, D)
    gng = gmlp_norm_g.reshape(NLAYER, 1, GDIM)
    ws = gmlp_ws.astype(BF16)
    bsb = jnp.broadcast_to(gmlp_bs[:, :, :, None], (NLAYER, GH, GCH, GCH))
    lng = gla_norm_g.reshape(NLAYER, 1, LVD)
    pf, pg, plw, wo = (p_fourier.astype(BF16), p_gmlp.astype(BF16), p_gla.astype(BF16), w_out.astype(BF16))
    w_r = jnp.pad(jnp.concatenate([router_expert_w, router_group_w], axis=-1),
                  ((0, 0), (0, 0), (0, RW - NEXP - NGRP)))
    wrh = w_r.astype(BF16)
    wrc = jnp.concatenate([wrh, (w_r - wrh.astype(F32)).astype(BF16)], axis=-1)
    br =jnp.pad(jnp.concatenate([router_expert_b, router_group_b], axis=-1),
                 ((0, 0), (0, RW - NEXP - NGRP))).reshape(NLAYER, 1, RW)
    cs_lat, cs_ctx, cc = _dft_consts()

    mods3 = _mods(cvec, w_mod, b_mod).reshape(NLAYER * 8 * 6, 1, D)

    h, xn = _init(x, ctx, norm1_g3, mods3)
    res = None
    for layer in range(NLAYER):
        pm = _inproj(xn, w_int, layer)
        lr = _lrproj(xn, w_int, layer)
        o = _gla(pm, lr, wa_cat, ba_cat, layer)
        yf = _fourier(pm, cs_lat, cs_ctx, cc)
        hmid, xm, logits = _merge(layer, yf, pm, o, h, mods3, norm2_g3, gng, ws, bsb, lng,
                                  pf, pg, plw, wo, wrc, br)
        meta, wts, cnt8 = _route(logits)
        plan = _plan(cnt8)
        xs = _dispatch(plan, xm, meta)
        ys = _experts(plan, xs, expert_w_gate, expert_w_up, expert_w_down, layer)
        res = _combine(layer, plan, ys, meta, wts, hmid, mods3, norm1_g3)
        if layer + 1 < NLAYER:
            h, xn = res
    return res
```

```python
import functools
import math

import numpy as np
import jax
import jax.numpy as jnp
from jax import lax
from jax.experimental import pallas as pl
from jax.experimental.pallas import tpu as pltpu

F32 = jnp.float32
BF16 = jnp.bfloat16
I32 = jnp.int32

D = 2048
NB = 4
SEQ = 2048
NLAYER = 4
CTX = 256
EPS = 1e-6
LB = CTX + SEQ
P = NB * LB
TM = 256
TPB = LB // TM
NT = P // TM

FG, FGD = 4, 128
FDIM = FG * FGD
GH, GHD, GCH = 4, 128, 128
GDIM = GH * GHD
LH, LDK, LDV, LRANK, LTAU, LC = 4, 128, 256, 16, 16.0, 64
LKD, LVD = LH * LDK, LH * LDV
NCH = LB // LC
NCTXCH = CTX // LC

W_ALIGNED = 4608
W_GATE0 = W_ALIGNED + 2 * 16
TN = 1536
NBLK_AL, NBLK_GT = W_ALIGNED // TN, 3 * D // TN
C_G0, C_A, C_ZU, C_ZV, C_Q, C_K, C_V, C_R = 0, 6144, 6656, 7168, 7680, 8192, 8704, 9728
NMAIN = 10752
LRW = 128

NGRP, EPG, NEXP, DEXP = 4, 8, 32, 512
TR = 256
CH = 8
LCH = (2 * TM + NEXP * (CH - 1)) // CH
LROWS = 768
TCH = TR // CH
NCHUNK = 2 * P // CH + NT * NEXP * (CH - 1) // CH + NEXP * (TCH - 1)
NTILE = NCHUNK // TCH + 1
NSLOT = NTILE * TR
RW = 128

VMEM_LIMIT = 56 * 1024 * 1024


def _cp(sem, vmem=VMEM_LIMIT):
    return pltpu.CompilerParams(dimension_semantics=sem, vmem_limit_bytes=vmem)


def _dot(a, b):
    return jnp.dot(a, b, preferred_element_type=F32)


def _dot_t(a, b):
    return lax.dot_general(a, b, (((1,), (1,)), ((), ())), preferred_element_type=F32)


def _dot_lt(a, b):
    return lax.dot_general(a, b, (((0,), (0,)), ((), ())), preferred_element_type=F32)


def _split(x):
    hi = x.astype(BF16)
    lo = (x - hi.astype(F32)).astype(BF16)
    return hi, lo


U32 = jnp.uint32
DH = D // 2


def _pack_bf16_pair(x):
    lo = lax.bitcast_convert_type(x[:, :DH], U32)
    hi = lax.bitcast_convert_type(x[:, DH:], U32)
    return lax.shift_right_logical(lo, jnp.uint32(16)) | (hi & jnp.uint32(0xFFFF0000))


def _unpack_bf16_pair(u):
    lo = lax.bitcast_convert_type(lax.shift_left(u, jnp.uint32(16)), F32)
    hi = lax.bitcast_convert_type(u & jnp.uint32(0xFFFF0000), F32)
    return lo.astype(BF16), hi.astype(BF16)


def _sigmoid(x):
    return 1.0 / (1.0 + jnp.exp(-x))


def _silu(x):
    return x * _sigmoid(x)


def _gelu_tanh(x):
    return 0.5 * x * (1.0 + jnp.tanh(math.sqrt(2.0 / math.pi) * (x + 0.044715 * (x * x * x))))


def _rms(x, g):
    return x * lax.rsqrt(jnp.mean(x * x, axis=-1, keepdims=True) + EPS) * g


def _tile_row(t):
    return jnp.where(t % TPB == 0, 4, t // TPB)


def _mod_spec(layer, comp, row_fn):
    return pl.BlockSpec((1, 1, D), lambda *g: ((layer * 8 + row_fn(*g)) * 6 + comp, 0, 0))


def _mods_kernel(c_ref, w_ref, b_ref, o_ref):
    c = c_ref[...]
    s = _silu(c).astype(BF16)
    o_ref[0] = _dot(s, w_ref[0].astype(BF16)) + b_ref[0]


def _mods(cvec, w_mod, b_mod):
    tn = 1024
    return pl.pallas_call(
        _mods_kernel,
        grid=(NLAYER, 6 * D // tn),
        in_specs=[pl.BlockSpec((8, D), lambda l, j: (0, 0)),
                  pl.BlockSpec((1, D, tn), lambda l, j: (l, 0, j)),
                  pl.BlockSpec((1, 1, tn), lambda l, j: (l, 0, j))],
        out_specs=pl.BlockSpec((1, 8, tn), lambda l, j: (l, 0, j)),
        out_shape=jax.ShapeDtypeStruct((NLAYER, 8, 6 * D), F32),
        compiler_params=_cp(("arbitrary", "arbitrary")),
        name="mods",
    )(cvec, w_mod, b_mod.reshape(NLAYER, 1, 6 * D))


def _init_kernel(x_ref, c_ref, g_ref, sh_ref, sc_ref, h_ref, xn_ref):
    j = pl.program_id(1)

    def emit(v):
        h_ref[...] = v
        xn_ref[...] = (_rms(v, g_ref[0]) * (1.0 + sc_ref[0]) + sh_ref[0]).astype(BF16)

    @pl.when(j == 0)
    def _():
        emit(c_ref[0])

    @pl.when(j > 0)
    def _():
        emit(x_ref[0])


def _init(x, ctx, norm1_g3, mods3):
    row = lambda b, j: jnp.where(j == 0, 4, b)
    return pl.pallas_call(
        _init_kernel,
        grid=(NB, TPB),
        in_specs=[pl.BlockSpec((1, TM, D), lambda b, j: (b, jnp.maximum(j - 1, 0), 0)),
                  pl.BlockSpec((1, CTX, D), lambda b, j: (b, 0, 0)),
                  pl.BlockSpec((1, 1, D), lambda b, j: (0, 0, 0)),
                  _mod_spec(0, 0, row), _mod_spec(0, 1, row)],
        out_specs=[pl.BlockSpec((TM, D), lambda b, j: (b * TPB + j, 0)),
                   pl.BlockSpec((TM, D), lambda b, j: (b * TPB + j, 0))],
        out_shape=[jax.ShapeDtypeStruct((P, D), F32), jax.ShapeDtypeStruct((P, D), BF16)],
        compiler_params=_cp(("arbitrary", "arbitrary")),
        name="init",
    )(x, ctx, norm1_g3, mods3, mods3)


TMP = 1024


GOFF = W_GATE0 - W_ALIGNED


def _inproj_kernel(x_ref, w_ref, wt_ref, o_ref, wb_ref):
    j = pl.program_id(0)

    @pl.when((pl.program_id(1) == 0) & (j < NBLK_AL))
    def _():
        wb_ref[...] = w_ref[0].astype(BF16)

    @pl.when((pl.program_id(1) == 0) & (j >= NBLK_AL))
    def _():
        wide = jnp.concatenate([w_ref[0], wt_ref[0]], axis=0)
        wb_ref[...] = wide[GOFF:GOFF + TN, :].astype(BF16)

    o_ref[...] = _dot_t(x_ref[...], wb_ref[...]).astype(BF16)


def _inproj(xn, w_int, layer):
    nblk = NBLK_AL + NBLK_GT
    tail = lambda j, i: (layer, jnp.where(j >= NBLK_AL, (j + 1) * (TN // GOFF), (NBLK_AL + 1) * (TN // GOFF)), 0)
    return pl.pallas_call(
        _inproj_kernel,
        grid=(nblk, P // TMP),
        in_specs=[pl.BlockSpec((TMP, D), lambda j, i: (i, 0)),
                  pl.BlockSpec((1, TN, D), lambda j, i: (layer, j, 0)),
                  pl.BlockSpec((1, GOFF, D), tail)],
        out_specs=pl.BlockSpec((TMP, TN), lambda j, i: (i, (j + NBLK_GT) % nblk)),
        out_shape=jax.ShapeDtypeStruct((P, NMAIN), BF16),
        scratch_shapes=[pltpu.VMEM((TN, D), BF16)],
        compiler_params=_cp(("arbitrary", "arbitrary")),
        name="inproj",
    )(xn, w_int, w_int)


def _lrproj_kernel(x_ref, w_ref, o_ref):
    o_ref[...] = _dot_t(x_ref[...], w_ref[0].astype(BF16))


def _lrproj(xn, w_int, layer):
    tm = 1024
    return pl.pallas_call(
        _lrproj_kernel,
        grid=(P // tm,),
        in_specs=[pl.BlockSpec((tm, D), lambda i: (i, 0)),
                  pl.BlockSpec((1, LRW, D), lambda i: (layer, W_ALIGNED // LRW, 0))],
        out_specs=pl.BlockSpec((tm, LRW), lambda i: (i, 0)),
        out_shape=jax.ShapeDtypeStruct((P, LRW), F32),
        compiler_params=_cp(("arbitrary",)),
        name="lrproj",
    )(xn, w_int)


def _dft_consts():
    def cs(n):
        k = np.arange(n, dtype=np.int64)
        ang = 2.0 * np.pi * ((k[:, None] * k[None, :]) % n).astype(np.float64) / n
        return np.cos(ang) / math.sqrt(n), np.sin(ang) / math.sqrt(n)

    c_l, s_l = cs(SEQ)
    c_c, s_c = cs(CTX)
    c_g, s_g = cs(FGD)
    cs_lat = np.concatenate([c_l, -s_l], axis=1).astype(BF16)
    cs_ctx = np.concatenate([c_c, -s_c], axis=1).astype(BF16)
    cc = np.concatenate([c_g, s_g], axis=1).astype(BF16)
    return cs_lat, cs_ctx, cc


def _fourier_kernel(a_ref, csl_ref, csc_ref, cc_ref, o_ref, rl_ref, rc_ref):
    j = pl.program_id(1)

    @pl.when(j == 0)
    def _():
        for g in range(FG):
            cols = slice(g * FGD, (g + 1) * FGD)
            t = _dot(a_ref[:, cols], cc_ref[...]).astype(BF16)
            rc_ref[0:CTX, cols] = t[0:CTX, 0:FGD]
            rc_ref[CTX:2 * CTX, cols] = t[0:CTX, FGD:2 * FGD]
            rl_ref[0:SEQ, cols] = t[CTX:LB, 0:FGD]
            rl_ref[SEQ:2 * SEQ, cols] = t[CTX:LB, FGD:2 * FGD]
        o_ref[...] = _dot(csc_ref[...], rc_ref[...]).astype(BF16)

    @pl.when(j > 0)
    def _():
        o_ref[...] = _dot(csl_ref[...], rl_ref[...]).astype(BF16)


def _fourier(pm, cs_lat, cs_ctx, cc):
    return pl.pallas_call(
        _fourier_kernel,
        grid=(NB, TPB),
        in_specs=[pl.BlockSpec((LB, FDIM), lambda b, j: (b, C_A // FDIM)),
                  pl.BlockSpec((TM, 2 * SEQ), lambda b, j: (jnp.maximum(j - 1, 0), 0)),
                  pl.BlockSpec((CTX, 2 * CTX), lambda b, j: (0, 0)),
                  pl.BlockSpec((FGD, 2 * FGD), lambda b, j: (0, 0))],
        out_specs=pl.BlockSpec((TM, FDIM), lambda b, j: (b * TPB + j, 0)),
        out_shape=jax.ShapeDtypeStruct((P, FDIM), BF16),
        scratch_shapes=[pltpu.VMEM((2 * SEQ, FDIM), BF16), pltpu.VMEM((2 * CTX, FDIM), BF16)],
        compiler_params=_cp(("arbitrary", "arbitrary")),
        name="fourier",
    )(pm, cs_lat, cs_ctx, cc)


SB = 256
SBC = SB // LC
NSB = LB // SB


def _gla_kernel(q_ref, k_ref, v_ref, lr_ref, wa_ref, ba_ref, o_ref,
                qd_ref, oacc_ref, ds_ref, gam_ref, sall_ref, sf_ref, sb_ref):
    ri = lax.broadcasted_iota(I32, (SB, SB), 0)
    ci = lax.broadcasted_iota(I32, (SB, SB), 1)
    same = (ri // LC) == (ci // LC)
    tri = (same & (ci <= ri)).astype(BF16)
    keep_f = same & (ci <= ri)
    keep_b = same & (ci > ri)
    rchunk = lax.broadcasted_iota(I32, (SB, LDK), 0) // LC
    scale = LDK ** -0.5
    wa = wa_ref[...]
    ba = ba_ref[...]

    def phase1(sb, carry):
        rows = pl.ds(pl.multiple_of(sb * SB, SB), SB)
        logits = _dot(lr_ref[rows, :].astype(BF16), wa) + ba
        g = (jnp.minimum(logits, 0.0) - jnp.log1p(jnp.exp(-jnp.abs(logits)))) * (1.0 / LTAU)
        g_hi, g_lo = _split(g)
        pre = _dot(tri, g_hi) + _dot(tri, g_lo)
        tot = jnp.concatenate(
            [jnp.broadcast_to(pre[c * LC + LC - 1:c * LC + LC, :], (LC, 2 * LDK)) for c in range(SBC)], axis=0)
        q = q_ref[rows, :].astype(F32) * scale
        k = k_ref[rows, :].astype(F32)
        v = v_ref[rows, :]
        s_sum = None
        kts = []
        for d in range(2):
            cols = slice(d * LDK, (d + 1) * LDK)
            t_d = tot[:, cols]
            b_d = pre[:, cols] if d == 0 else t_d - pre[:, cols] + g[:, cols]
            q_dec = (q * jnp.exp(b_d)).astype(BF16)
            k_inv = (k * jnp.exp(-b_d)).astype(BF16)
            k_tail = (k * jnp.exp(t_d - b_d)).astype(BF16)
            sc = jnp.where(keep_f if d == 0 else keep_b, _dot_t(q_dec, k_inv), 0.0)
            s_sum = sc if s_sum is None else s_sum + sc
            qd_ref[rows, cols] = q_dec
            zero = jnp.zeros_like(k_tail)
            kts += [jnp.where(rchunk == c, k_tail, zero) for c in range(SBC)]
            for c in range(SBC):
                gam_ref[d, sb * SBC + c] = jnp.exp(t_d[c * LC:c * LC + 1, :])
        oacc_ref[rows, :] = _dot(s_sum.astype(BF16), v)
        dst = _dot_lt(v, jnp.concatenate(kts, axis=1))
        for d in range(2):
            for c in range(SBC):
                j = d * SBC + c
                ds_ref[d, sb * SBC + c] = dst[:, j * LDK:(j + 1) * LDK]
        return carry

    lax.fori_loop(0, NSB, phase1, 0, unroll=3)

    sf_ref[...] = jnp.zeros_like(sf_ref)
    sb_ref[...] = jnp.zeros_like(sb_ref)

    def phase2(i, carry):
        nb = jnp.where(i < NCTXCH, NCTXCH - 1 - i, NCH + NCTXCH - 1 - i)
        s_f = sf_ref[...]
        s_b = sb_ref[...]
        sall_ref[i, :, 0:LDK] = s_f.astype(BF16)
        sall_ref[nb, :, LDK:2 * LDK] = s_b.astype(BF16)
        sf_ref[...] = s_f * gam_ref[0, i] + ds_ref[0, i]
        sb_ref[...] = s_b * gam_ref[1, nb] + ds_ref[1, nb]
        return carry

    lax.fori_loop(0, NCH, phase2, 0)

    def phase3(n, carry):
        rows = pl.ds(pl.multiple_of(n * LC, LC), LC)
        o_ref[rows, :] = (oacc_ref[rows, :] + _dot_t(qd_ref[rows, :], sall_ref[n])).astype(BF16)
        return carry

    lax.fori_loop(0, NCH, phase3, 0, unroll=4)


def _gla(pm, lr, wa_cat, ba_cat, layer):
    return pl.pallas_call(
        _gla_kernel,
        grid=(NB, LH),
        in_specs=[pl.BlockSpec((LB, LDK), lambda b, h: (b, C_Q // LDK + h)),
                  pl.BlockSpec((LB, LDK), lambda b, h: (b, C_K // LDK + h)),
                  pl.BlockSpec((LB, LDV), lambda b, h: (b, C_V // LDV + h)),
                  pl.BlockSpec((LB, LRW), lambda b, h: (b, 0)),
                  pl.BlockSpec((None, None, LRW, 2 * LDK), lambda b, h: (layer, h, 0, 0)),
                  pl.BlockSpec((None, None, 1, 2 * LDK), lambda b, h: (layer, h, 0, 0))],
        out_specs=pl.BlockSpec((LB, LDV), lambda b, h: (b, h)),
        out_shape=jax.ShapeDtypeStruct((P, LVD), BF16),
        scratch_shapes=[pltpu.VMEM((LB, 2 * LDK), BF16), pltpu.VMEM((LB, LDV), F32),
                        pltpu.VMEM((2, NCH, LDV, LDK), F32), pltpu.VMEM((2, NCH, 1, LDK), F32),
                        pltpu.VMEM((NCH, LDV, 2 * LDK), BF16),
                        pltpu.VMEM((LDV, LDK), F32), pltpu.VMEM((LDV, LDK), F32)],
        compiler_params=_cp(("arbitrary", "arbitrary")),
        name="gla",
    )(pm, pm, pm, lr, wa_cat, ba_cat)


def _merge_kernel(yf_ref, zu_ref, zv_ref, o_ref, ra_ref, rb_ref, g0_ref, g1_ref, g2_ref, h_ref,
                  gt1_ref, sh2_ref, sc2_ref, n2g_ref, gng_ref, ws_ref, bs_ref, lng_ref,
                  pf_ref, pg_ref, pl_ref, wo_ref, wrc_ref, br_ref,
                  hmid_ref, xm_ref, lg_ref):
    y = _sigmoid(g0_ref[...].astype(F32)) * _dot(yf_ref[...], pf_ref[...])

    u = _gelu_tanh(zu_ref[...].astype(F32))
    v = _rms(_gelu_tanh(zv_ref[...].astype(F32)), gng_ref[...]).astype(BF16)
    chunks = []
    for ch in range(TM // GCH):
        rows = slice(ch * GCH, (ch + 1) * GCH)
        heads = [_dot(ws_ref[hd], v[rows, hd * GHD:(hd + 1) * GHD]) + bs_ref[hd] for hd in range(GH)]
        chunks.append(jnp.concatenate(heads, axis=1))
    s = jnp.concatenate(chunks, axis=0)
    y += _sigmoid(g1_ref[...].astype(F32)) * _dot((u * s).astype(BF16), pg_ref[...])

    o = o_ref[...].astype(F32)
    lng = lng_ref[...]
    heads = [_rms(o[:, hd * LDV:(hd + 1) * LDV], lng[:, hd * LDV:(hd + 1) * LDV]) for hd in range(LH)]
    r = jnp.concatenate([ra_ref[...], rb_ref[...]], axis=1).astype(F32)
    ol = (jnp.concatenate(heads, axis=1) * _silu(r)).astype(BF16)
    y += _sigmoid(g2_ref[...].astype(F32)) * _dot(ol, pl_ref[...])

    hmid = h_ref[...] + gt1_ref[0] * _dot(y.astype(BF16), wo_ref[...])
    hmid_ref[...] = hmid

    xm = _rms(hmid, n2g_ref[0]) * (1.0 + sc2_ref[0]) + sh2_ref[0]
    xm_hi, xm_lo = _split(xm)
    xm_ref[...] = xm_hi
    hh_hl = _dot(xm_hi, wrc_ref[...])
    lg_ref[...] = hh_hl[:, 0:RW] + hh_hl[:, RW:2 * RW] + _dot(xm_lo, wrc_ref[:, 0:RW]) + br_ref[...]


def _merge(layer, yf, pm, o, h, mods3, norm2_g3, gng, ws, bsb, lng, pf, pg, plw, wo, wrc, br):
    row = _tile_row
    tile = lambda w, c: pl.BlockSpec((TM, w), lambda t: (t, c))
    lay3 = lambda a, b: pl.BlockSpec((None, a, b), lambda t: (layer, 0, 0), pipeline_mode=pl.Buffered(1))
    lay4 = lambda a, b, c: pl.BlockSpec((None, a, b, c), lambda t: (layer, 0, 0, 0))
    return pl.pallas_call(
        _merge_kernel,
        grid=(NT,),
        in_specs=[tile(FDIM, 0), tile(GDIM, C_ZU // GDIM), tile(GDIM, C_ZV // GDIM), tile(LVD, 0),
                  tile(LVD // 2, C_R // (LVD // 2)), tile(LVD // 2, C_R // (LVD // 2) + 1),
                  tile(D, 0), tile(D, 1), tile(D, 2), tile(D, 0),
                  _mod_spec(layer, 2, row), _mod_spec(layer, 3, row), _mod_spec(layer, 4, row),
                  pl.BlockSpec((1, 1, D), lambda t: (layer, 0, 0)),
                  lay3(1, GDIM), lay4(GH, GCH, GCH), lay4(GH, GCH, GCH), lay3(1, LVD),
                  lay3(FDIM, D), lay3(GDIM, D), lay3(LVD, D), lay3(D, D),
                  lay3(D, 2 * RW), lay3(1, RW)],
        out_specs=[tile(D, 0), tile(D, 0), tile(RW, 0)],
        out_shape=[jax.ShapeDtypeStruct((P, D), F32), jax.ShapeDtypeStruct((P, D), BF16),
                   jax.ShapeDtypeStruct((P, RW), F32)],
        compiler_params=_cp(("arbitrary",)),
        name="merge",
    )(yf, pm, pm, o, pm, pm, pm, pm, pm, h, mods3, mods3, mods3, norm2_g3, gng, ws, bsb, lng,
      pf, pg, plw, wo, wrc, br)


def _route_kernel(lg_ref, meta_ref, wts_ref, cnt_ref):
    lg = lg_ref[...]
    lane = lax.broadcasted_iota(I32, (TM, RW), 1)
    lane_f = lane.astype(F32)
    ninf = jnp.float32(-jnp.inf)

    def first_max(x):
        m = jnp.max(x, axis=-1, keepdims=True)
        first = jnp.min(jnp.where(x == m, lane_f, float(RW)), axis=-1, keepdims=True)
        return m, first.astype(I32)

    is_g = (lane >= NEXP) & (lane < NEXP + NGRP)
    gmax, glane = first_max(jnp.where(is_g, lg, ninf))
    gsum = jnp.sum(jnp.where(is_g, jnp.exp(lg - gmax), 0.0), axis=-1, keepdims=True)
    g_w = 1.0 / gsum
    lo = (glane - NEXP) * EPG
    in_grp = (lane >= lo) & (lane < lo + EPG)
    el = jnp.where(in_grp, lg, ninf)
    v1, l1 = first_max(el)
    v2, l2 = first_max(jnp.where(lane == l1, ninf, el))
    e = jnp.exp(v2 - v1)
    w1 = g_w / (1.0 + e)
    w2 = g_w * e / (1.0 + e)

    hit1 = lane == l1
    hit2 = lane == l2
    m = (hit1 | hit2).astype(BF16)
    ri = lax.broadcasted_iota(I32, (TM, TM), 0)
    ci = lax.broadcasted_iota(I32, (TM, TM), 1)
    before = _dot((ci < ri).astype(BF16), m)
    r1 = jnp.sum(jnp.where(hit1, before, 0.0), axis=-1, keepdims=True).astype(I32)
    r2 = jnp.sum(jnp.where(hit2, before, 0.0), axis=-1, keepdims=True).astype(I32)
    total = jnp.sum(m.astype(F32), axis=0, keepdims=True)
    cnt_ref[...] = jnp.broadcast_to(total, cnt_ref.shape).astype(I32)
    meta_ref[...] = jnp.where(lane == 0, l1, jnp.where(lane == 1, l2, jnp.where(lane == 2, r1,
                              jnp.where(lane == 3, r2, 0))))
    wts_ref[...] = jnp.where(lane == 0, w1, jnp.where(lane == 1, w2, 0.0))


def _route(logits):
    return pl.pallas_call(
        _route_kernel,
        grid=(NT,),
        in_specs=[pl.BlockSpec((TM, RW), lambda t: (t, 0))],
        out_specs=[pl.BlockSpec((TM, RW), lambda t: (t, 0)), pl.BlockSpec((TM, RW), lambda t: (t, 0)),
                   pl.BlockSpec((8, RW), lambda t: (t, 0))],
        out_shape=[jax.ShapeDtypeStruct((P, RW), I32), jax.ShapeDtypeStruct((P, RW), F32),
                   jax.ShapeDtypeStruct((NT * 8, RW), I32)],
        compiler_params=_cp(("arbitrary",)),
        name="route",
    )(logits)


def _plan(cnt8):
    cnt = cnt8.reshape(NT, 8, RW)[:, 0, :NEXP]
    c8 = (cnt + (CH - 1)) // CH
    lend = jnp.cumsum(c8, axis=1)
    lstart = lend - c8
    nloc = lend[:, -1]
    reg = jnp.sum(c8, axis=0)
    rpad = (reg + (TCH - 1)) // TCH * TCH
    rend = jnp.cumsum(rpad)
    rstart = rend - rpad
    gbase = rstart[None, :] + jnp.cumsum(c8, axis=0) - c8
    j = jnp.arange(LCH, dtype=I32)
    owner = jnp.sum((lend[:, None, :] <= j[None, :, None]).astype(I32), axis=2)
    sel = (owner[:, :, None] == jnp.arange(NEXP, dtype=I32)[None, None, :]).astype(I32)
    dch = jnp.sum(sel * (gbase - lstart)[:, None, :], axis=2) + j[None, :]
    n_act = rend[-1] // TCH
    tid = jnp.minimum(jnp.arange(NTILE, dtype=I32), n_act - 1)
    tile_expert = jnp.minimum(jnp.sum((rend[None, :] <= (tid * TCH)[:, None]).astype(I32), axis=1), NEXP - 1)
    ex = jnp.arange(NEXP, dtype=I32)
    nonempty = (reg > 0).astype(I32)
    seg_of_e = jnp.cumsum(nonempty) - 1
    nseg = jnp.sum(nonempty)
    seg = jnp.sum((tile_expert[:, None] == ex[None, :]).astype(I32) * seg_of_e[None, :], axis=1)
    seg_expert = jnp.sum(((seg_of_e[None, :] == ex[:, None]) & (reg[None, :] > 0)).astype(I32) * ex[None, :], axis=1)
    ahead = lambda k: jnp.where(seg + k < nseg,
                                jnp.sum(((seg + k)[:, None] == ex[None, :]).astype(I32) * seg_expert[None, :], axis=1), -1)
    lrow = jnp.pad((lstart * CH).astype(F32), ((0, 0), (0, RW - NEXP))).reshape(NT, 1, RW)
    return dict(nloc=nloc.astype(I32), dch=dch.astype(I32).reshape(NT, 1, LCH), lrow=lrow,
                pstart=(rstart + reg).astype(I32), npad=(rpad - reg).astype(I32),
                tile_expert=tile_expert.astype(I32), seg=seg.astype(I32),
                next1=ahead(1).astype(I32), next2=ahead(2).astype(I32),
                n_act=n_act.reshape(1).astype(I32))


def _local_pos(meta, lrow):
    lane = lax.broadcasted_iota(I32, (TM, RW), 1)
    meta_f = meta.astype(F32)
    col = lambda k: jnp.sum(jnp.where(lane == k, meta_f, 0.0), axis=-1, keepdims=True).astype(I32)
    l1, l2, r1, r2 = col(0), col(1), col(2), col(3)
    off = lambda l: jnp.sum(jnp.where(lane == l, lrow, 0.0), axis=-1, keepdims=True).astype(I32)
    return off(l1) + r1, off(l2) + r2


def _dispatch_kernel(nloc_ref, pstart_ref, npad_ref, na_ref, dch_ref, x_ref, meta_ref, lrow_ref, xs_hbm,
                     xloc, zbuf, sem, zsem):
    t = pl.program_id(0)
    slot = t % 2

    def chunk_copy(buf_slot, j, g):
        return pltpu.make_async_copy(xloc.at[buf_slot, pl.ds(pl.multiple_of(j * CH, CH), CH)],
                                     xs_hbm.at[pl.ds(pl.multiple_of(g * CH, CH), CH)], sem.at[buf_slot])

    def zero_chunks(first, n, wait):
        zcopy = lambda c: pltpu.make_async_copy(
            zbuf, xs_hbm.at[pl.ds(pl.multiple_of((first + c) * CH, CH), CH)], zsem.at[0])
        if wait:
            lax.fori_loop(0, n, lambda c, z: (zcopy(0).wait(), z)[1], 0)
        else:
            lax.fori_loop(0, n, lambda c, z: (zcopy(c).start(), z)[1], 0)

    def zero_fill(wait):
        for e in range(NEXP):
            zero_chunks(pstart_ref[e], npad_ref[e], wait)
        zero_chunks(na_ref[0] * TCH, NTILE * TCH - na_ref[0] * TCH, wait)

    @pl.when(t == 0)
    def _():
        zbuf[...] = jnp.zeros_like(zbuf)
        zero_fill(False)

    p1, p2 = _local_pos(meta_ref[...], lrow_ref[0])
    pos = lax.broadcasted_iota(I32, (TM, LROWS), 1)
    sel = ((pos == p1) | (pos == p2)).astype(BF16)
    xloc[slot] = _pack_bf16_pair(_dot_lt(sel, x_ref[...]))

    @pl.when(t > 0)
    def _():
        lax.fori_loop(0, nloc_ref[t - 1], lambda j, z: (chunk_copy(1 - slot, 0, 0).wait(), z)[1], 0)

    lax.fori_loop(0, nloc_ref[t], lambda j, z: (chunk_copy(slot, j, dch_ref[0, 0, j]).start(), z)[1], 0)

    @pl.when(t == NT - 1)
    def _():
        lax.fori_loop(0, nloc_ref[t], lambda j, z: (chunk_copy(slot, 0, 0).wait(), z)[1], 0)
        zero_fill(True)


def _dispatch(plan, xm, meta):
    grid_spec = pltpu.PrefetchScalarGridSpec(
        num_scalar_prefetch=4,
        grid=(NT,),
        in_specs=[pl.BlockSpec((1, 1, LCH), lambda t, *_: (t, 0, 0), memory_space=pltpu.SMEM),
                  pl.BlockSpec((TM, D), lambda t, *_: (t, 0)),
                  pl.BlockSpec((TM, RW), lambda t, *_: (t, 0)),
                  pl.BlockSpec((1, 1, RW), lambda t, *_: (t, 0, 0))],
        out_specs=pl.BlockSpec(memory_space=pl.ANY),
        scratch_shapes=[pltpu.VMEM((2, LROWS, DH), U32), pltpu.VMEM((CH, DH), U32),
                        pltpu.SemaphoreType.DMA((2,)), pltpu.SemaphoreType.DMA((1,))],
    )
    return pl.pallas_call(
        _dispatch_kernel,
        grid_spec=grid_spec,
        out_shape=jax.ShapeDtypeStruct((NSLOT, DH), U32),
        compiler_params=_cp(("arbitrary",)),
        name="dispatch",
    )(plan["nloc"], plan["pstart"], plan["npad"], plan["n_act"], plan["dch"], xm, meta, plan["lrow"])


def _experts_kernel(layer, te_ref, seg_ref, n1_ref, n2_ref, na_ref, x_ref, wg_hbm, wu_hbm, wd_hbm, y_ref,
                    wgf, wuf, wdf, wgb, wub, wdb, wsem):
    i = pl.program_id(0)

    def weight_copies(e, b):
        return (pltpu.make_async_copy(wg_hbm.at[layer, e], wgf.at[b], wsem.at[b, 0]),
                pltpu.make_async_copy(wu_hbm.at[layer, e], wuf.at[b], wsem.at[b, 1]),
                pltpu.make_async_copy(wd_hbm.at[layer, e], wdf.at[b], wsem.at[b, 2]))

    @pl.when(i < na_ref[0])
    def _():
        e = te_ref[i]
        b = seg_ref[i] % 2
        first = jnp.logical_or(i == 0, e != te_ref[jnp.maximum(i - 1, 0)])

        @pl.when(i == 0)
        def _():
            for cp in weight_copies(e, 0):
                cp.start()

            @pl.when(n1_ref[0] >= 0)
            def _():
                for cp in weight_copies(n1_ref[0], 1):
                    cp.start()

        @pl.when(first)
        def _():
            for cp in weight_copies(e, b):
                cp.wait()
            wgb[...] = wgf[b].astype(BF16)
            wub[...] = wuf[b].astype(BF16)
            wdb[...] = wdf[b].astype(BF16)

            @pl.when(n2_ref[i] >= 0)
            def _():
                for cp in weight_copies(n2_ref[i], b):
                    cp.start()

        x_lo, x_hi = _unpack_bf16_pair(x_ref[...])
        hg = _dot(x_lo, wgb[0:DH, :]) + _dot(x_hi, wgb[DH:D, :])
        hu = _dot(x_lo, wub[0:DH, :]) + _dot(x_hi, wub[DH:D, :])
        act = (_silu(hg) * hu).astype(BF16)
        y = _dot(act, wdb[...])
        y_ref[...] = _pack_bf16_pair(y.astype(BF16).astype(F32))

    @pl.when(i >= na_ref[0])
    def _():
        y_ref[...] = jnp.zeros_like(y_ref)


def _experts(plan, xs, wg, wu, wd, layer):
    hbm = pl.BlockSpec(memory_space=pl.ANY)
    grid_spec = pltpu.PrefetchScalarGridSpec(
        num_scalar_prefetch=5,
        grid=(NTILE,),
        in_specs=[pl.BlockSpec((TR, DH), lambda i, te, sg, n1, n2, na: (jnp.minimum(i, na[0] - 1), 0)),
                  hbm, hbm, hbm],
        out_specs=pl.BlockSpec((TR, DH), lambda i, *_: (i, 0)),
        scratch_shapes=[pltpu.VMEM((2, D, DEXP), F32), pltpu.VMEM((2, D, DEXP), F32), pltpu.VMEM((2, DEXP, D), F32),
                        pltpu.VMEM((D, DEXP), BF16), pltpu.VMEM((D, DEXP), BF16), pltpu.VMEM((DEXP, D), BF16),
                        pltpu.SemaphoreType.DMA((2, 3))],
    )
    return pl.pallas_call(
        functools.partial(_experts_kernel, layer),
        grid_spec=grid_spec,
        out_shape=jax.ShapeDtypeStruct((NSLOT, DH), U32),
        compiler_params=_cp(("arbitrary",)),
        name="experts",
    )(plan["tile_expert"], plan["seg"], plan["next1"], plan["next2"], plan["n_act"], xs, wg, wu, wd)


def _combine_kernel(last, nloc_ref, dch_ref, dchn_ref, meta_ref, w_ref, lrow_ref, h_ref, gt2_ref, g_ref,
                    sh_ref, sc_ref, ys_hbm, *rest):
    if last:
        out_ref, yloc, sem = rest
    else:
        h_out, xn_out, yloc, sem = rest
    t = pl.program_id(0)
    slot = t % 2

    def fetch(idx_ref, buf_slot, n):
        def body(j, z):
            g = idx_ref[0, 0, j]
            pltpu.make_async_copy(ys_hbm.at[pl.ds(pl.multiple_of(g * CH, CH), CH)],
                                  yloc.at[buf_slot, pl.ds(pl.multiple_of(j * CH, CH), CH)],
                                  sem.at[buf_slot]).start()
            return z
        lax.fori_loop(0, n, body, 0)

    @pl.when(t == 0)
    def _():
        yloc[...] = jnp.zeros_like(yloc)
        fetch(dch_ref, 0, nloc_ref[0])

    @pl.when(t + 1 < NT)
    def _():
        fetch(dchn_ref, 1 - slot, nloc_ref[jnp.minimum(t + 1, NT - 1)])

    def wait_one(j, z):
        pltpu.make_async_copy(ys_hbm.at[pl.ds(0, CH)], yloc.at[slot, pl.ds(0, CH)], sem.at[slot]).wait()
        return z
    lax.fori_loop(0, nloc_ref[t], wait_one, 0)

    p1, p2 = _local_pos(meta_ref[...], lrow_ref[0])
    pos = lax.broadcasted_iota(I32, (TM, LROWS), 1)
    y_lo, y_hi = _unpack_bf16_pair(yloc[slot])
    w = w_ref[...]
    q = (jnp.where(pos == p1, w[:, 0:1], 0.0) + jnp.where(pos == p2, w[:, 1:2], 0.0)).astype(BF16)
    moe = jnp.concatenate([_dot(q, y_lo), _dot(q, y_hi)], axis=1)
    h = h_ref[...] + gt2_ref[0] * moe
    if last:
        @pl.when(t % TPB > 0)
        def _():
            out_ref[0] = _rms(h, g_ref[0])
    else:
        h_out[...] = h
        xn_out[...] = (_rms(h, g_ref[0]) * (1.0 + sc_ref[0]) + sh_ref[0]).astype(BF16)


def _combine(layer, plan, ys, meta, wts, hmid, mods3, norm_g3):
    last = layer == NLAYER - 1
    row = lambda t, *_: _tile_row(t)
    nxt = 0 if last else layer + 1
    tile = lambda w: pl.BlockSpec((TM, w), lambda t, *_: (t, 0))
    if last:
        out_specs = pl.BlockSpec((1, TM, D), lambda t, *_: (t // TPB, jnp.maximum(t % TPB - 1, 0), 0))
        out_shape = jax.ShapeDtypeStruct((NB, SEQ, D), F32)
    else:
        out_specs = [tile(D), tile(D)]
        out_shape = [jax.ShapeDtypeStruct((P, D), F32), jax.ShapeDtypeStruct((P, D), BF16)]
    grid_spec = pltpu.PrefetchScalarGridSpec(
        num_scalar_prefetch=1,
        grid=(NT,),
        in_specs=[pl.BlockSpec((1, 1, LCH), lambda t, *_: (t, 0, 0), memory_space=pltpu.SMEM),
                  pl.BlockSpec((1, 1, LCH), lambda t, *_: (jnp.minimum(t + 1, NT - 1), 0, 0),
                               memory_space=pltpu.SMEM),
                  tile(RW), tile(RW), pl.BlockSpec((1, 1, RW), lambda t, *_: (t, 0, 0)), tile(D),
                  _mod_spec(layer, 5, row),
                  pl.BlockSpec((1, 1, D), lambda t, *_: (NLAYER if last else nxt, 0, 0)),
                  _mod_spec(nxt, 0, row), _mod_spec(nxt, 1, row),
                  pl.BlockSpec(memory_space=pl.ANY)],
        out_specs=out_specs,
        scratch_shapes=[pltpu.VMEM((2, LROWS, DH), U32), pltpu.SemaphoreType.DMA((2,))],
    )
    return pl.pallas_call(
        functools.partial(_combine_kernel, last),
        grid_spec=grid_spec,
        out_shape=out_shape,
        compiler_params=_cp(("arbitrary",)),
        name="final" if last else "combine",
    )(plan["nloc"], plan["dch"], plan["dch"], meta, wts, plan["lrow"], hmid, mods3, norm_g3, mods3, mods3, ys)


def kernel(x, c, ctx, c_ctx, w_mod, b_mod, norm1_g, norm2_g, w_in, p_fourier, gmlp_norm_g, gmlp_ws,
           gmlp_bs, p_gmlp, gla_w_a2, gla_b_a, gla_norm_g, p_gla, w_out, router_group_w, router_group_b,
           router_expert_w, router_expert_b, expert_w_gate, expert_w_up, expert_w_down, final_norm_g):
    cvec = jnp.concatenate([c, c_ctx[None, :], jnp.zeros((3, D), F32)], axis=0)
    w_int = jnp.swapaxes(w_in, 1, 2)
    wa_pad = jnp.stack([jnp.pad(gla_w_a2[:, 0], ((0, 0), (0, LRW - LRANK), (0, 0))),
                        jnp.pad(gla_w_a2[:, 1], ((0, 0), (LRANK, LRW - 2 * LRANK), (0, 0)))], axis=1)
    wa_cat = wa_pad.reshape(NLAYER, 2, LRW, LH, LDK).transpose(0, 3, 2, 1, 4).reshape(NLAYER, LH, LRW, 2 * LDK)
    wa_cat = wa_cat.astype(BF16)
    ba_cat = gla_b_a.reshape(NLAYER, 2, LH, LDK).transpose(0, 2, 1, 3).reshape(NLAYER, LH, 1, 2 * LDK)
    norm1_g3 = jnp.concatenate([norm1_g, final_norm_g[None, :]], axis=0).reshape(NLAYER + 1, 1, D)
    norm2_g3 = norm2_g.reshape(NLAYER, 1, D)
    gng = gmlp_norm_g.reshape(NLAYER, 1, GDIM)
    ws = gmlp_ws.astype(BF16)
    bsb = jnp.broadcast_to(gmlp_bs[:, :, :, None], (NLAYER, GH, GCH, GCH))
    lng = gla_norm_g.reshape(NLAYER, 1, LVD)
    pf, pg, plw, wo = (p_fourier.astype(BF16), p_gmlp.astype(BF16), p_gla.astype(BF16), w_out.astype(BF16))
    w_r = jnp.pad(jnp.concatenate([router_expert_w, router_group_w], axis=-1),
                  ((0, 0), (0, 0), (0, RW - NEXP - NGRP)))
    wrh = w_r.astype(BF16)
    wrc = jnp.concatenate([wrh, (w_r - wrh.astype(F32)).astype(BF16)], axis=-1)
    br =jnp.pad(jnp.concatenate([router_expert_b, router_group_b], axis=-1),
                 ((0, 0), (0, RW - NEXP - NGRP))).reshape(NLAYER, 1, RW)
    cs_lat, cs_ctx, cc = _dft_consts()

    mods3 = _mods(cvec, w_mod, b_mod).reshape(NLAYER * 8 * 6, 1, D)

    h, xn = _init(x, ctx, norm1_g3, mods3)
    res = None
    for layer in range(NLAYER):
        pm = _inproj(xn, w_int, layer)
        lr = _lrproj(xn, w_int, layer)
        o = _gla(pm, lr, wa_cat, ba_cat, layer)
        yf = _fourier(pm, cs_lat, cs_ctx, cc)
        hmid, xm, logits = _merge(layer, yf, pm, o, h, mods3, norm2_g3, gng, ws, bsb, lng,
                                  pf, pg, plw, wo, wrc, br)
        meta, wts, cnt8 = _route(logits)
        plan = _plan(cnt8)
        xs = _dispatch(plan, xm, meta)
        ys = _experts(plan, xs, expert_w_gate, expert_w_up, expert_w_down, layer)
        res = _combine(layer, plan, ys, meta, wts, hmid, mods3, norm1_g3)
        if layer + 1 < NLAYER:
            h, xn = res
    return res
```

```python
import functools
import math

import numpy as np
import jax
import jax.numpy as jnp
from jax import lax
from jax.experimental import pallas as pl
from jax.experimental.pallas import tpu as pltpu

F32 = jnp.float32
BF16 = jnp.bfloat16
I32 = jnp.int32

D = 2048
NB = 4
SEQ = 2048
NLAYER = 4
CTX = 256
EPS = 1e-6
LB = CTX + SEQ
P = NB * LB
TM = 256
TPB = LB // TM
NT = P // TM

FG, FGD = 4, 128
FDIM = FG * FGD
GH, GHD, GCH = 4, 128, 128
GDIM = GH * GHD
LH, LDK, LDV, LRANK, LTAU, LC = 4, 128, 256, 16, 16.0, 64
LKD, LVD = LH * LDK, LH * LDV
NCH = LB // LC
NCTXCH = CTX // LC

W_ALIGNED = 4608
W_GATE0 = W_ALIGNED + 2 * 16
TN = 1536
NBLK_AL, NBLK_GT = W_ALIGNED // TN, 3 * D // TN
C_G0, C_A, C_ZU, C_ZV, C_Q, C_K, C_V, C_R = 0, 6144, 6656, 7168, 7680, 8192, 8704, 9728
NMAIN = 10752
LRW = 128

NGRP, EPG, NEXP, DEXP = 4, 8, 32, 512
TR = 256
CH = 8
LCH = (2 * TM + NEXP * (CH - 1)) // CH
LROWS = 768
TCH = TR // CH
NCHUNK = 2 * P // CH + NT * NEXP * (CH - 1) // CH + NEXP * (TCH - 1)
NTILE = NCHUNK // TCH + 1
NSLOT = NTILE * TR
RW = 128

VMEM_LIMIT = 56 * 1024 * 1024


def _cp(sem, vmem=VMEM_LIMIT):
    return pltpu.CompilerParams(dimension_semantics=sem, vmem_limit_bytes=vmem)


def _dot(a, b):
    return jnp.dot(a, b, preferred_element_type=F32)


def _dot_t(a, b):
    return lax.dot_general(a, b, (((1,), (1,)), ((), ())), preferred_element_type=F32)


def _dot_lt(a, b):
    return lax.dot_general(a, b, (((0,), (0,)), ((), ())), preferred_element_type=F32)


def _split(x):
    hi = x.astype(BF16)
    lo = (x - hi.astype(F32)).astype(BF16)
    return hi, lo


U32 = jnp.uint32
DH = D // 2


def _pack_bf16_pair(x):
    lo = lax.bitcast_convert_type(x[:, :DH], U32)
    hi = lax.bitcast_convert_type(x[:, DH:], U32)
    return lax.shift_right_logical(lo, jnp.uint32(16)) | (hi & jnp.uint32(0xFFFF0000))


def _unpack_bf16_pair(u):
    lo = lax.bitcast_convert_type(lax.shift_left(u, jnp.uint32(16)), F32)
    hi = lax.bitcast_convert_type(u & jnp.uint32(0xFFFF0000), F32)
    return lo.astype(BF16), hi.astype(BF16)


def _sigmoid(x):
    return 1.0 / (1.0 + jnp.exp(-x))


def _silu(x):
    return x * _sigmoid(x)


def _gelu_tanh(x):
    return 0.5 * x * (1.0 + jnp.tanh(math.sqrt(2.0 / math.pi) * (x + 0.044715 * (x * x * x))))


def _rms(x, g):
    return x * lax.rsqrt(jnp.mean(x * x, axis=-1, keepdims=True) + EPS) * g


def _tile_row(t):
    return jnp.where(t % TPB == 0, 4, t // TPB)


def _mod_spec(layer, comp, row_fn):
    return pl.BlockSpec((1, 1, D), lambda *g: ((layer * 8 + row_fn(*g)) * 6 + comp, 0, 0))


def _mods_kernel(c_ref, w_ref, b_ref, o_ref):
    c = c_ref[...]
    s = _silu(c).astype(BF16)
    o_ref[0] = _dot(s, w_ref[0].astype(BF16)) + b_ref[0]


def _mods(cvec, w_mod, b_mod):
    tn = 1024
    return pl.pallas_call(
        _mods_kernel,
        grid=(NLAYER, 6 * D // tn),
        in_specs=[pl.BlockSpec((8, D), lambda l, j: (0, 0)),
                  pl.BlockSpec((1, D, tn), lambda l, j: (l, 0, j)),
                  pl.BlockSpec((1, 1, tn), lambda l, j: (l, 0, j))],
        out_specs=pl.BlockSpec((1, 8, tn), lambda l, j: (l, 0, j)),
        out_shape=jax.ShapeDtypeStruct((NLAYER, 8, 6 * D), F32),
        compiler_params=_cp(("arbitrary", "arbitrary")),
        name="mods",
    )(cvec, w_mod, b_mod.reshape(NLAYER, 1, 6 * D))


def _init_kernel(x_ref, c_ref, g_ref, sh_ref, sc_ref, h_ref, xn_ref):
    j = pl.program_id(1)

    def emit(v):
        h_ref[...] = v
        xn_ref[...] = (_rms(v, g_ref[0]) * (1.0 + sc_ref[0]) + sh_ref[0]).astype(BF16)

    @pl.when(j == 0)
    def _():
        emit(c_ref[0])

    @pl.when(j > 0)
    def _():
        emit(x_ref[0])


def _init(x, ctx, norm1_g3, mods3):
    row = lambda b, j: jnp.where(j == 0, 4, b)
    return pl.pallas_call(
        _init_kernel,
        grid=(NB, TPB),
        in_specs=[pl.BlockSpec((1, TM, D), lambda b, j: (b, jnp.maximum(j - 1, 0), 0)),
                  pl.BlockSpec((1, CTX, D), lambda b, j: (b, 0, 0)),
                  pl.BlockSpec((1, 1, D), lambda b, j: (0, 0, 0)),
                  _mod_spec(0, 0, row), _mod_spec(0, 1, row)],
        out_specs=[pl.BlockSpec((TM, D), lambda b, j: (b * TPB + j, 0)),
                   pl.BlockSpec((TM, D), lambda b, j: (b * TPB + j, 0))],
        out_shape=[jax.ShapeDtypeStruct((P, D), F32), jax.ShapeDtypeStruct((P, D), BF16)],
        compiler_params=_cp(("arbitrary", "arbitrary")),
        name="init",
    )(x, ctx, norm1_g3, mods3, mods3)


TMP = 1024


GOFF = W_GATE0 - W_ALIGNED


def _inproj_kernel(x_ref, w_ref, wt_ref, o_ref, wb_ref):
    j = pl.program_id(0)

    @pl.when((pl.program_id(1) == 0) & (j < NBLK_AL))
    def _():
        wb_ref[...] = w_ref[0].astype(BF16)

    @pl.when((pl.program_id(1) == 0) & (j >= NBLK_AL))
    def _():
        wide = jnp.concatenate([w_ref[0], wt_ref[0]], axis=0)
        wb_ref[...] = wide[GOFF:GOFF + TN, :].astype(BF16)

    o_ref[...] = _dot_t(x_ref[...], wb_ref[...]).astype(BF16)


def _inproj(xn, w_int, layer):
    nblk = NBLK_AL + NBLK_GT
    tail = lambda j, i: (layer, jnp.where(j >= NBLK_AL, (j + 1) * (TN // GOFF), (NBLK_AL + 1) * (TN // GOFF)), 0)
    return pl.pallas_call(
        _inproj_kernel,
        grid=(nblk, P // TMP),
        in_specs=[pl.BlockSpec((TMP, D), lambda j, i: (i, 0)),
                  pl.BlockSpec((1, TN, D), lambda j, i: (layer, j, 0)),
                  pl.BlockSpec((1, GOFF, D), tail)],
        out_specs=pl.BlockSpec((TMP, TN), lambda j, i: (i, (j + NBLK_GT) % nblk)),
        out_shape=jax.ShapeDtypeStruct((P, NMAIN), BF16),
        scratch_shapes=[pltpu.VMEM((TN, D), BF16)],
        compiler_params=_cp(("arbitrary", "arbitrary")),
        name="inproj",
    )(xn, w_int, w_int)


def _lrproj_kernel(x_ref, w_ref, o_ref):
    o_ref[...] = _dot_t(x_ref[...], w_ref[0].astype(BF16))


def _lrproj(xn, w_int, layer):
    tm = 1024
    return pl.pallas_call(
        _lrproj_kernel,
        grid=(P // tm,),
        in_specs=[pl.BlockSpec((tm, D), lambda i: (i, 0)),
                  pl.BlockSpec((1, LRW, D), lambda i: (layer, W_ALIGNED // LRW, 0))],
        out_specs=pl.BlockSpec((tm, LRW), lambda i: (i, 0)),
        out_shape=jax.ShapeDtypeStruct((P, LRW), F32),
        compiler_params=_cp(("arbitrary",)),
        name="lrproj",
    )(xn, w_int)


def _dft_consts():
    def cs(n):
        k = np.arange(n, dtype=np.int64)
        ang = 2.0 * np.pi * ((k[:, None] * k[None, :]) % n).astype(np.float64) / n
        return np.cos(ang) / math.sqrt(n), np.sin(ang) / math.sqrt(n)

    c_l, s_l = cs(SEQ)
    c_c, s_c = cs(CTX)
    c_g, s_g = cs(FGD)
    cs_lat = np.concatenate([c_l, -s_l], axis=1).astype(BF16)
    cs_ctx = np.concatenate([c_c, -s_c], axis=1).astype(BF16)
    cc = np.concatenate([c_g, s_g], axis=1).astype(BF16)
    return cs_lat, cs_ctx, cc


def _fourier_kernel(a_ref, csl_ref, csc_ref, cc_ref, o_ref, rl_ref, rc_ref):
    j = pl.program_id(1)

    @pl.when(j == 0)
    def _():
        for g in range(FG):
            cols = slice(g * FGD, (g + 1) * FGD)
            t = _dot(a_ref[:, cols], cc_ref[...]).astype(BF16)
            rc_ref[0:CTX, cols] = t[0:CTX, 0:FGD]
            rc_ref[CTX:2 * CTX, cols] = t[0:CTX, FGD:2 * FGD]
            rl_ref[0:SEQ, cols] = t[CTX:LB, 0:FGD]
            rl_ref[SEQ:2 * SEQ, cols] = t[CTX:LB, FGD:2 * FGD]
        o_ref[...] = _dot(csc_ref[...], rc_ref[...]).astype(BF16)

    @pl.when(j > 0)
    def _():
        o_ref[...] = _dot(csl_ref[...], rl_ref[...]).astype(BF16)


def _fourier(pm, cs_lat, cs_ctx, cc):
    return pl.pallas_call(
        _fourier_kernel,
        grid=(NB, TPB),
        in_specs=[pl.BlockSpec((LB, FDIM), lambda b, j: (b, C_A // FDIM)),
                  pl.BlockSpec((TM, 2 * SEQ), lambda b, j: (jnp.maximum(j - 1, 0), 0)),
                  pl.BlockSpec((CTX, 2 * CTX), lambda b, j: (0, 0)),
                  pl.BlockSpec((FGD, 2 * FGD), lambda b, j: (0, 0))],
        out_specs=pl.BlockSpec((TM, FDIM), lambda b, j: (b * TPB + j, 0)),
        out_shape=jax.ShapeDtypeStruct((P, FDIM), BF16),
        scratch_shapes=[pltpu.VMEM((2 * SEQ, FDIM), BF16), pltpu.VMEM((2 * CTX, FDIM), BF16)],
        compiler_params=_cp(("arbitrary", "arbitrary")),
        name="fourier",
    )(pm, cs_lat, cs_ctx, cc)


SB = 256
SBC = SB // LC
NSB = LB // SB


def _gla_kernel(q_ref, k_ref, v_ref, lr_ref, wa_ref, ba_ref, o_ref,
                qd_ref, oacc_ref, ds_ref, gam_ref, sall_ref, sf_ref, sb_ref):
    ri = lax.broadcasted_iota(I32, (SB, SB), 0)
    ci = lax.broadcasted_iota(I32, (SB, SB), 1)
    same = (ri // LC) == (ci // LC)
    tri = (same & (ci <= ri)).astype(BF16)
    keep_f = same & (ci <= ri)
    keep_b = same & (ci > ri)
    rchunk = lax.broadcasted_iota(I32, (SB, LDK), 0) // LC
    scale = LDK ** -0.5
    wa = wa_ref[...]
    ba = ba_ref[...]

    def phase1(sb, carry):
        rows = pl.ds(pl.multiple_of(sb * SB, SB), SB)
        logits = _dot(lr_ref[rows, :].astype(BF16), wa) + ba
        g = (jnp.minimum(logits, 0.0) - jnp.log1p(jnp.exp(-jnp.abs(logits)))) * (1.0 / LTAU)
        g_hi, g_lo = _split(g)
        pre = _dot(tri, g_hi) + _dot(tri, g_lo)
        tot = jnp.concatenate(
            [jnp.broadcast_to(pre[c * LC + LC - 1:c * LC + LC, :], (LC, 2 * LDK)) for c in range(SBC)], axis=0)
        q = q_ref[rows, :].astype(F32) * scale
        k = k_ref[rows, :].astype(F32)
        v = v_ref[rows, :]
        s_sum = None
        kts = []
        for d in range(2):
            cols = slice(d * LDK, (d + 1) * LDK)
            t_d = tot[:, cols]
            b_d = pre[:, cols] if d == 0 else t_d - pre[:, cols] + g[:, cols]
            q_dec = (q * jnp.exp(b_d)).astype(BF16)
            k_inv = (k * jnp.exp(-b_d)).astype(BF16)
            k_tail = (k * jnp.exp(t_d - b_d)).astype(BF16)
            sc = jnp.where(keep_f if d == 0 else keep_b, _dot_t(q_dec, k_inv), 0.0)
            s_sum = sc if s_sum is None else s_sum + sc
            qd_ref[rows, cols] = q_dec
            zero = jnp.zeros_like(k_tail)
            kts += [jnp.where(rchunk == c, k_tail, zero) for c in range(SBC)]
            for c in range(SBC):
                gam_ref[d, sb * SBC + c] = jnp.exp(t_d[c * LC:c * LC + 1, :])
        oacc_ref[rows, :] = _dot(s_sum.astype(BF16), v)
        dst = _dot_lt(v, jnp.concatenate(kts, axis=1))
        for d in range(2):
            for c in range(SBC):
                j = d * SBC + c
                ds_ref[d, sb * SBC + c] = dst[:, j * LDK:(j + 1) * LDK]
        return carry

    lax.fori_loop(0, NSB, phase1, 0, unroll=True)

    sf_ref[...] = jnp.zeros_like(sf_ref)
    sb_ref[...] = jnp.zeros_like(sb_ref)

    def phase2(i, carry):
        nb = jnp.where(i < NCTXCH, NCTXCH - 1 - i, NCH + NCTXCH - 1 - i)
        s_f = sf_ref[...]
        s_b = sb_ref[...]
        sall_ref[i, :, 0:LDK] = s_f.astype(BF16)
        sall_ref[nb, :, LDK:2 * LDK] = s_b.astype(BF16)
        sf_ref[...] = s_f * gam_ref[0, i] + ds_ref[0, i]
        sb_ref[...] = s_b * gam_ref[1, nb] + ds_ref[1, nb]
        return carry

    lax.fori_loop(0, NCH, phase2, 0)

    def phase3(n, carry):
        rows = pl.ds(pl.multiple_of(n * LC, LC), LC)
        o_ref[rows, :] = (oacc_ref[rows, :] + _dot_t(qd_ref[rows, :], sall_ref[n])).astype(BF16)
        return carry

    lax.fori_loop(0, NCH, phase3, 0, unroll=4)


def _gla(pm, lr, wa_cat, ba_cat, layer):
    return pl.pallas_call(
        _gla_kernel,
        grid=(NB, LH),
        in_specs=[pl.BlockSpec((LB, LDK), lambda b, h: (b, C_Q // LDK + h)),
                  pl.BlockSpec((LB, LDK), lambda b, h: (b, C_K // LDK + h)),
                  pl.BlockSpec((LB, LDV), lambda b, h: (b, C_V // LDV + h)),
                  pl.BlockSpec((LB, LRW), lambda b, h: (b, 0)),
                  pl.BlockSpec((None, None, LRW, 2 * LDK), lambda b, h: (layer, h, 0, 0)),
                  pl.BlockSpec((None, None, 1, 2 * LDK), lambda b, h: (layer, h, 0, 0))],
        out_specs=pl.BlockSpec((LB, LDV), lambda b, h: (b, h)),
        out_shape=jax.ShapeDtypeStruct((P, LVD), BF16),
        scratch_shapes=[pltpu.VMEM((LB, 2 * LDK), BF16), pltpu.VMEM((LB, LDV), F32),
                        pltpu.VMEM((2, NCH, LDV, LDK), F32), pltpu.VMEM((2, NCH, 1, LDK), F32),
                        pltpu.VMEM((NCH, LDV, 2 * LDK), BF16),
                        pltpu.VMEM((LDV, LDK), F32), pltpu.VMEM((LDV, LDK), F32)],
        compiler_params=_cp(("arbitrary", "arbitrary")),
        name="gla",
    )(pm, pm, pm, lr, wa_cat, ba_cat)


def _merge_kernel(yf_ref, zu_ref, zv_ref, o_ref, ra_ref, rb_ref, g0_ref, g1_ref, g2_ref, h_ref,
                  gt1_ref, sh2_ref, sc2_ref, n2g_ref, gng_ref, ws_ref, bs_ref, lng_ref,
                  pf_ref, pg_ref, pl_ref, wo_ref, wrc_ref, br_ref,
                  hmid_ref, xm_ref, lg_ref):
    y = _sigmoid(g0_ref[...].astype(F32)) * _dot(yf_ref[...], pf_ref[...])

    u = _gelu_tanh(zu_ref[...].astype(F32))
    v = _rms(_gelu_tanh(zv_ref[...].astype(F32)), gng_ref[...]).astype(BF16)
    chunks = []
    for ch in range(TM // GCH):
        rows = slice(ch * GCH, (ch + 1) * GCH)
        heads = [_dot(ws_ref[hd], v[rows, hd * GHD:(hd + 1) * GHD]) + bs_ref[hd] for hd in range(GH)]
        chunks.append(jnp.concatenate(heads, axis=1))
    s = jnp.concatenate(chunks, axis=0)
    y += _sigmoid(g1_ref[...].astype(F32)) * _dot((u * s).astype(BF16), pg_ref[...])

    o = o_ref[...].astype(F32)
    lng = lng_ref[...]
    heads = [_rms(o[:, hd * LDV:(hd + 1) * LDV], lng[:, hd * LDV:(hd + 1) * LDV]) for hd in range(LH)]
    r = jnp.concatenate([ra_ref[...], rb_ref[...]], axis=1).astype(F32)
    ol = (jnp.concatenate(heads, axis=1) * _silu(r)).astype(BF16)
    y += _sigmoid(g2_ref[...].astype(F32)) * _dot(ol, pl_ref[...])

    hmid = h_ref[...] + gt1_ref[0] * _dot(y.astype(BF16), wo_ref[...])
    hmid_ref[...] = hmid

    xm = _rms(hmid, n2g_ref[0]) * (1.0 + sc2_ref[0]) + sh2_ref[0]
    xm_hi, xm_lo = _split(xm)
    xm_ref[...] = xm_hi
    hh_hl = _dot(xm_hi, wrc_ref[...])
    lg_ref[...] = hh_hl[:, 0:RW] + hh_hl[:, RW:2 * RW] + _dot(xm_lo, wrc_ref[:, 0:RW]) + br_ref[...]


def _merge(layer, yf, pm, o, h, mods3, norm2_g3, gng, ws, bsb, lng, pf, pg, plw, wo, wrc, br):
    row = _tile_row
    tile = lambda w, c: pl.BlockSpec((TM, w), lambda t: (t, c))
    lay3 = lambda a, b: pl.BlockSpec((None, a, b), lambda t: (layer, 0, 0), pipeline_mode=pl.Buffered(1))
    lay4 = lambda a, b, c: pl.BlockSpec((None, a, b, c), lambda t: (layer, 0, 0, 0))
    return pl.pallas_call(
        _merge_kernel,
        grid=(NT,),
        in_specs=[tile(FDIM, 0), tile(GDIM, C_ZU // GDIM), tile(GDIM, C_ZV // GDIM), tile(LVD, 0),
                  tile(LVD // 2, C_R // (LVD // 2)), tile(LVD // 2, C_R // (LVD // 2) + 1),
                  tile(D, 0), tile(D, 1), tile(D, 2), tile(D, 0),
                  _mod_spec(layer, 2, row), _mod_spec(layer, 3, row), _mod_spec(layer, 4, row),
                  pl.BlockSpec((1, 1, D), lambda t: (layer, 0, 0)),
                  lay3(1, GDIM), lay4(GH, GCH, GCH), lay4(GH, GCH, GCH), lay3(1, LVD),
                  lay3(FDIM, D), lay3(GDIM, D), lay3(LVD, D), lay3(D, D),
                  lay3(D, 2 * RW), lay3(1, RW)],
        out_specs=[tile(D, 0), tile(D, 0), tile(RW, 0)],
        out_shape=[jax.ShapeDtypeStruct((P, D), F32), jax.ShapeDtypeStruct((P, D), BF16),
                   jax.ShapeDtypeStruct((P, RW), F32)],
        compiler_params=_cp(("arbitrary",)),
        name="merge",
    )(yf, pm, pm, o, pm, pm, pm, pm, pm, h, mods3, mods3, mods3, norm2_g3, gng, ws, bsb, lng,
      pf, pg, plw, wo, wrc, br)


def _route_kernel(lg_ref, meta_ref, wts_ref, cnt_ref):
    lg = lg_ref[...]
    lane = lax.broadcasted_iota(I32, (TM, RW), 1)
    lane_f = lane.astype(F32)
    ninf = jnp.float32(-jnp.inf)

    def first_max(x):
        m = jnp.max(x, axis=-1, keepdims=True)
        first = jnp.min(jnp.where(x == m, lane_f, float(RW)), axis=-1, keepdims=True)
        return m, first.astype(I32)

    is_g = (lane >= NEXP) & (lane < NEXP + NGRP)
    gmax, glane = first_max(jnp.where(is_g, lg, ninf))
    gsum = jnp.sum(jnp.where(is_g, jnp.exp(lg - gmax), 0.0), axis=-1, keepdims=True)
    g_w = 1.0 / gsum
    lo = (glane - NEXP) * EPG
    in_grp = (lane >= lo) & (lane < lo + EPG)
    el = jnp.where(in_grp, lg, ninf)
    v1, l1 = first_max(el)
    v2, l2 = first_max(jnp.where(lane == l1, ninf, el))
    e = jnp.exp(v2 - v1)
    w1 = g_w / (1.0 + e)
    w2 = g_w * e / (1.0 + e)

    hit1 = lane == l1
    hit2 = lane == l2
    m = (hit1 | hit2).astype(BF16)
    ri = lax.broadcasted_iota(I32, (TM, TM), 0)
    ci = lax.broadcasted_iota(I32, (TM, TM), 1)
    before = _dot((ci < ri).astype(BF16), m)
    r1 = jnp.sum(jnp.where(hit1, before, 0.0), axis=-1, keepdims=True).astype(I32)
    r2 = jnp.sum(jnp.where(hit2, before, 0.0), axis=-1, keepdims=True).astype(I32)
    total = jnp.sum(m.astype(F32), axis=0, keepdims=True)
    cnt_ref[...] = jnp.broadcast_to(total, cnt_ref.shape).astype(I32)
    meta_ref[...] = jnp.where(lane == 0, l1, jnp.where(lane == 1, l2, jnp.where(lane == 2, r1,
                              jnp.where(lane == 3, r2, 0))))
    wts_ref[...] = jnp.where(lane == 0, w1, jnp.where(lane == 1, w2, 0.0))


def _route(logits):
    return pl.pallas_call(
        _route_kernel,
        grid=(NT,),
        in_specs=[pl.BlockSpec((TM, RW), lambda t: (t, 0))],
        out_specs=[pl.BlockSpec((TM, RW), lambda t: (t, 0)), pl.BlockSpec((TM, RW), lambda t: (t, 0)),
                   pl.BlockSpec((8, RW), lambda t: (t, 0))],
        out_shape=[jax.ShapeDtypeStruct((P, RW), I32), jax.ShapeDtypeStruct((P, RW), F32),
                   jax.ShapeDtypeStruct((NT * 8, RW), I32)],
        compiler_params=_cp(("arbitrary",)),
        name="route",
    )(logits)


def _plan(cnt8):
    cnt = cnt8.reshape(NT, 8, RW)[:, 0, :NEXP]
    c8 = (cnt + (CH - 1)) // CH
    lend = jnp.cumsum(c8, axis=1)
    lstart = lend - c8
    nloc = lend[:, -1]
    reg = jnp.sum(c8, axis=0)
    rpad = (reg + (TCH - 1)) // TCH * TCH
    rend = jnp.cumsum(rpad)
    rstart = rend - rpad
    gbase = rstart[None, :] + jnp.cumsum(c8, axis=0) - c8
    j = jnp.arange(LCH, dtype=I32)
    owner = jnp.sum((lend[:, None, :] <= j[None, :, None]).astype(I32), axis=2)
    sel = (owner[:, :, None] == jnp.arange(NEXP, dtype=I32)[None, None, :]).astype(I32)
    dch = jnp.sum(sel * (gbase - lstart)[:, None, :], axis=2) + j[None, :]
    n_act = rend[-1] // TCH
    tid = jnp.minimum(jnp.arange(NTILE, dtype=I32), n_act - 1)
    tile_expert = jnp.minimum(jnp.sum((rend[None, :] <= (tid * TCH)[:, None]).astype(I32), axis=1), NEXP - 1)
    ex = jnp.arange(NEXP, dtype=I32)
    nonempty = (reg > 0).astype(I32)
    seg_of_e = jnp.cumsum(nonempty) - 1
    nseg = jnp.sum(nonempty)
    seg = jnp.sum((tile_expert[:, None] == ex[None, :]).astype(I32) * seg_of_e[None, :], axis=1)
    seg_expert = jnp.sum(((seg_of_e[None, :] == ex[:, None]) & (reg[None, :] > 0)).astype(I32) * ex[None, :], axis=1)
    ahead = lambda k: jnp.where(seg + k < nseg,
                                jnp.sum(((seg + k)[:, None] == ex[None, :]).astype(I32) * seg_expert[None, :], axis=1), -1)
    lrow = jnp.pad((lstart * CH).astype(F32), ((0, 0), (0, RW - NEXP))).reshape(NT, 1, RW)
    return dict(nloc=nloc.astype(I32), dch=dch.astype(I32).reshape(NT, 1, LCH), lrow=lrow,
                pstart=(rstart + reg).astype(I32), npad=(rpad - reg).astype(I32),
                tile_expert=tile_expert.astype(I32), seg=seg.astype(I32),
                next1=ahead(1).astype(I32), next2=ahead(2).astype(I32),
                n_act=n_act.reshape(1).astype(I32))


def _local_pos(meta, lrow):
    lane = lax.broadcasted_iota(I32, (TM, RW), 1)
    meta_f = meta.astype(F32)
    col = lambda k: jnp.sum(jnp.where(lane == k, meta_f, 0.0), axis=-1, keepdims=True).astype(I32)
    l1, l2, r1, r2 = col(0), col(1), col(2), col(3)
    off = lambda l: jnp.sum(jnp.where(lane == l, lrow, 0.0), axis=-1, keepdims=True).astype(I32)
    return off(l1) + r1, off(l2) + r2


def _dispatch_kernel(nloc_ref, pstart_ref, npad_ref, na_ref, dch_ref, x_ref, meta_ref, lrow_ref, xs_hbm,
                     xloc, zbuf, sem, zsem):
    t = pl.program_id(0)
    slot = t % 2

    def chunk_copy(buf_slot, j, g):
        return pltpu.make_async_copy(xloc.at[buf_slot, pl.ds(pl.multiple_of(j * CH, CH), CH)],
                                     xs_hbm.at[pl.ds(pl.multiple_of(g * CH, CH), CH)], sem.at[buf_slot])

    def zero_chunks(first, n, wait):
        zcopy = lambda c: pltpu.make_async_copy(
            zbuf, xs_hbm.at[pl.ds(pl.multiple_of((first + c) * CH, CH), CH)], zsem.at[0])
        if wait:
            lax.fori_loop(0, n, lambda c, z: (zcopy(0).wait(), z)[1], 0)
        else:
            lax.fori_loop(0, n, lambda c, z: (zcopy(c).start(), z)[1], 0)

    def zero_fill(wait):
        for e in range(NEXP):
            zero_chunks(pstart_ref[e], npad_ref[e], wait)
        zero_chunks(na_ref[0] * TCH, NTILE * TCH - na_ref[0] * TCH, wait)

    @pl.when(t == 0)
    def _():
        zbuf[...] = jnp.zeros_like(zbuf)
        zero_fill(False)

    p1, p2 = _local_pos(meta_ref[...], lrow_ref[0])
    pos = lax.broadcasted_iota(I32, (TM, LROWS), 1)
    sel = ((pos == p1) | (pos == p2)).astype(BF16)
    xloc[slot] = _pack_bf16_pair(_dot_lt(sel, x_ref[...]))

    @pl.when(t > 0)
    def _():
        lax.fori_loop(0, nloc_ref[t - 1], lambda j, z: (chunk_copy(1 - slot, 0, 0).wait(), z)[1], 0)

    lax.fori_loop(0, nloc_ref[t], lambda j, z: (chunk_copy(slot, j, dch_ref[0, 0, j]).start(), z)[1], 0)

    @pl.when(t == NT - 1)
    def _():
        lax.fori_loop(0, nloc_ref[t], lambda j, z: (chunk_copy(slot, 0, 0).wait(), z)[1], 0)
        zero_fill(True)


def _dispatch(plan, xm, meta):
    grid_spec = pltpu.PrefetchScalarGridSpec(
        num_scalar_prefetch=4,
        grid=(NT,),
        in_specs=[pl.BlockSpec((1, 1, LCH), lambda t, *_: (t, 0, 0), memory_space=pltpu.SMEM),
                  pl.BlockSpec((TM, D), lambda t, *_: (t, 0)),
                  pl.BlockSpec((TM, RW), lambda t, *_: (t, 0)),
                  pl.BlockSpec((1, 1, RW), lambda t, *_: (t, 0, 0))],
        out_specs=pl.BlockSpec(memory_space=pl.ANY),
        scratch_shapes=[pltpu.VMEM((2, LROWS, DH), U32), pltpu.VMEM((CH, DH), U32),
                        pltpu.SemaphoreType.DMA((2,)), pltpu.SemaphoreType.DMA((1,))],
    )
    return pl.pallas_call(
        _dispatch_kernel,
        grid_spec=grid_spec,
        out_shape=jax.ShapeDtypeStruct((NSLOT, DH), U32),
        compiler_params=_cp(("arbitrary",)),
        name="dispatch",
    )(plan["nloc"], plan["pstart"], plan["npad"], plan["n_act"], plan["dch"], xm, meta, plan["lrow"])


def _experts_kernel(layer, te_ref, seg_ref, n1_ref, n2_ref, na_ref, x_ref, wg_hbm, wu_hbm, wd_hbm, y_ref,
                    wgf, wuf, wdf, wgb, wub, wdb, ybuf, wsem):
    i = pl.program_id(0)

    def weight_copies(e, b):
        return (pltpu.make_async_copy(wg_hbm.at[layer, e], wgf.at[b], wsem.at[b, 0]),
                pltpu.make_async_copy(wu_hbm.at[layer, e], wuf.at[b], wsem.at[b, 1]),
                pltpu.make_async_copy(wd_hbm.at[layer, e], wdf.at[b], wsem.at[b, 2]))

    @pl.when(i < na_ref[0])
    def _():
        e = te_ref[i]
        b = seg_ref[i] % 2
        first = jnp.logical_or(i == 0, e != te_ref[jnp.maximum(i - 1, 0)])

        @pl.when(i == 0)
        def _():
            for cp in weight_copies(e, 0):
                cp.start()

            @pl.when(n1_ref[0] >= 0)
            def _():
                for cp in weight_copies(n1_ref[0], 1):
                    cp.start()

        @pl.when(first)
        def _():
            for cp in weight_copies(e, b):
                cp.wait()
            wgb[...] = wgf[b].astype(BF16)
            wub[...] = wuf[b].astype(BF16)
            wdb[...] = wdf[b].astype(BF16)

            @pl.when(n2_ref[i] >= 0)
            def _():
                for cp in weight_copies(n2_ref[i], b):
                    cp.start()

    def compute():
        x_lo, x_hi = _unpack_bf16_pair(x_ref[...])
        hg = _dot(x_lo, wgb[0:DH, :]) + _dot(x_hi, wgb[DH:D, :])
        hu = _dot(x_lo, wub[0:DH, :]) + _dot(x_hi, wub[DH:D, :])
        act = (_silu(hg) * hu).astype(BF16)
        ybuf[...] = _dot(act, wdb[...])

    def emit():
        y_ref[...] = _pack_bf16_pair(ybuf[...].astype(BF16).astype(F32))

    @pl.when(i == 0)
    def _():
        compute()

    @pl.when((i >= 1) & (i < na_ref[0]))
    def _():
        emit()
        compute()

    @pl.when((i >= 1) & (i == na_ref[0]))
    def _():
        emit()

    @pl.when(i > na_ref[0])
    def _():
        y_ref[...] = jnp.zeros_like(y_ref)


def _experts(plan, xs, wg, wu, wd, layer):
    hbm = pl.BlockSpec(memory_space=pl.ANY)
    grid_spec = pltpu.PrefetchScalarGridSpec(
        num_scalar_prefetch=5,
        grid=(NTILE + 1,),
        in_specs=[pl.BlockSpec((TR, DH), lambda i, te, sg, n1, n2, na: (jnp.minimum(i, na[0] - 1), 0)),
                  hbm, hbm, hbm],
        out_specs=pl.BlockSpec((TR, DH), lambda i, *_: (jnp.maximum(i - 1, 0), 0)),
        scratch_shapes=[pltpu.VMEM((2, D, DEXP), F32), pltpu.VMEM((2, D, DEXP), F32), pltpu.VMEM((2, DEXP, D), F32),
                        pltpu.VMEM((D, DEXP), BF16), pltpu.VMEM((D, DEXP), BF16), pltpu.VMEM((DEXP, D), BF16),
                        pltpu.VMEM((TR, D), F32), pltpu.SemaphoreType.DMA((2, 3))],
    )
    return pl.pallas_call(
        functools.partial(_experts_kernel, layer),
        grid_spec=grid_spec,
        out_shape=jax.ShapeDtypeStruct((NSLOT, DH), U32),
        compiler_params=_cp(("arbitrary",)),
        name="experts",
    )(plan["tile_expert"], plan["seg"], plan["next1"], plan["next2"], plan["n_act"], xs, wg, wu, wd)


def _combine_kernel(last, nloc_ref, dch_ref, dchn_ref, meta_ref, w_ref, lrow_ref, h_ref, gt2_ref, g_ref,
                    sh_ref, sc_ref, ys_hbm, *rest):
    if last:
        out_ref, yloc, sem = rest
    else:
        h_out, xn_out, yloc, sem = rest
    t = pl.program_id(0)
    slot = t % 2

    def fetch(idx_ref, buf_slot, n):
        def body(j, z):
            g = idx_ref[0, 0, j]
            pltpu.make_async_copy(ys_hbm.at[pl.ds(pl.multiple_of(g * CH, CH), CH)],
                                  yloc.at[buf_slot, pl.ds(pl.multiple_of(j * CH, CH), CH)],
                                  sem.at[buf_slot]).start()
            return z
        lax.fori_loop(0, n, body, 0)

    @pl.when(t == 0)
    def _():
        yloc[...] = jnp.zeros_like(yloc)
        fetch(dch_ref, 0, nloc_ref[0])

    @pl.when(t + 1 < NT)
    def _():
        fetch(dchn_ref, 1 - slot, nloc_ref[jnp.minimum(t + 1, NT - 1)])

    def wait_one(j, z):
        pltpu.make_async_copy(ys_hbm.at[pl.ds(0, CH)], yloc.at[slot, pl.ds(0, CH)], sem.at[slot]).wait()
        return z
    lax.fori_loop(0, nloc_ref[t], wait_one, 0)

    p1, p2 = _local_pos(meta_ref[...], lrow_ref[0])
    pos = lax.broadcasted_iota(I32, (TM, LROWS), 1)
    y_lo, y_hi = _unpack_bf16_pair(yloc[slot])
    w = w_ref[...]
    q = (jnp.where(pos == p1, w[:, 0:1], 0.0) + jnp.where(pos == p2, w[:, 1:2], 0.0)).astype(BF16)
    moe = jnp.concatenate([_dot(q, y_lo), _dot(q, y_hi)], axis=1)
    h = h_ref[...] + gt2_ref[0] * moe
    if last:
        @pl.when(t % TPB > 0)
        def _():
            out_ref[0] = _rms(h, g_ref[0])
    else:
        h_out[...] = h
        xn_out[...] = (_rms(h, g_ref[0]) * (1.0 + sc_ref[0]) + sh_ref[0]).astype(BF16)


def _combine(layer, plan, ys, meta, wts, hmid, mods3, norm_g3):
    last = layer == NLAYER - 1
    row = lambda t, *_: _tile_row(t)
    nxt = 0 if last else layer + 1
    tile = lambda w: pl.BlockSpec((TM, w), lambda t, *_: (t, 0))
    if last:
        out_specs = pl.BlockSpec((1, TM, D), lambda t, *_: (t // TPB, jnp.maximum(t % TPB - 1, 0), 0))
        out_shape = jax.ShapeDtypeStruct((NB, SEQ, D), F32)
    else:
        out_specs = [tile(D), tile(D)]
        out_shape = [jax.ShapeDtypeStruct((P, D), F32), jax.ShapeDtypeStruct((P, D), BF16)]
    grid_spec = pltpu.PrefetchScalarGridSpec(
        num_scalar_prefetch=1,
        grid=(NT,),
        in_specs=[pl.BlockSpec((1, 1, LCH), lambda t, *_: (t, 0, 0), memory_space=pltpu.SMEM),
                  pl.BlockSpec((1, 1, LCH), lambda t, *_: (jnp.minimum(t + 1, NT - 1), 0, 0),
                               memory_space=pltpu.SMEM),
                  tile(RW), tile(RW), pl.BlockSpec((1, 1, RW), lambda t, *_: (t, 0, 0)), tile(D),
                  _mod_spec(layer, 5, row),
                  pl.BlockSpec((1, 1, D), lambda t, *_: (NLAYER if last else nxt, 0, 0)),
                  _mod_spec(nxt, 0, row), _mod_spec(nxt, 1, row),
                  pl.BlockSpec(memory_space=pl.ANY)],
        out_specs=out_specs,
        scratch_shapes=[pltpu.VMEM((2, LROWS, DH), U32), pltpu.SemaphoreType.DMA((2,))],
    )
    return pl.pallas_call(
        functools.partial(_combine_kernel, last),
        grid_spec=grid_spec,
        out_shape=out_shape,
        compiler_params=_cp(("arbitrary",)),
        name="final" if last else "combine",
    )(plan["nloc"], plan["dch"], plan["dch"], meta, wts, plan["lrow"], hmid, mods3, norm_g3, mods3, mods3, ys)


def kernel(x, c, ctx, c_ctx, w_mod, b_mod, norm1_g, norm2_g, w_in, p_fourier, gmlp_norm_g, gmlp_ws,
           gmlp_bs, p_gmlp, gla_w_a2, gla_b_a, gla_norm_g, p_gla, w_out, router_group_w, router_group_b,
           router_expert_w, router_expert_b, expert_w_gate, expert_w_up, expert_w_down, final_norm_g):
    cvec = jnp.concatenate([c, c_ctx[None, :], jnp.zeros((3, D), F32)], axis=0)
    w_int = jnp.swapaxes(w_in, 1, 2)
    wa_pad = jnp.stack([jnp.pad(gla_w_a2[:, 0], ((0, 0), (0, LRW - LRANK), (0, 0))),
                        jnp.pad(gla_w_a2[:, 1], ((0, 0), (LRANK, LRW - 2 * LRANK), (0, 0)))], axis=1)
    wa_cat = wa_pad.reshape(NLAYER, 2, LRW, LH, LDK).transpose(0, 3, 2, 1, 4).reshape(NLAYER, LH, LRW, 2 * LDK)
    wa_cat = wa_cat.astype(BF16)
    ba_cat = gla_b_a.reshape(NLAYER, 2, LH, LDK).transpose(0, 2, 1, 3).reshape(NLAYER, LH, 1, 2 * LDK)
    norm1_g3 = jnp.concatenate([norm1_g, final_norm_g[None, :]], axis=0).reshape(NLAYER + 1, 1, D)
    norm2_g3 = norm2_g.reshape(NLAYER, 1, D)
    gng = gmlp_norm_g.reshape(NLAYER, 1, GDIM)
    ws = gmlp_ws.astype(BF16)
    bsb = jnp.broadcast_to(gmlp_bs[:, :, :, None], (NLAYER, GH, GCH, GCH))
    lng = gla_norm_g.reshape(NLAYER, 1, LVD)
    pf, pg, plw, wo = (p_fourier.astype(BF16), p_gmlp.astype(BF16), p_gla.astype(BF16), w_out.astype(BF16))
    w_r = jnp.pad(jnp.concatenate([router_expert_w, router_group_w], axis=-1),
                  ((0, 0), (0, 0), (0, RW - NEXP - NGRP)))
    wrh = w_r.astype(BF16)
    wrc = jnp.concatenate([wrh, (w_r - wrh.astype(F32)).astype(BF16)], axis=-1)
    br =jnp.pad(jnp.concatenate([router_expert_b, router_group_b], axis=-1),
                 ((0, 0), (0, RW - NEXP - NGRP))).reshape(NLAYER, 1, RW)
    cs_lat, cs_ctx, cc = _dft_consts()

    mods3 = _mods(cvec, w_mod, b_mod).reshape(NLAYER * 8 * 6, 1, D)

    h, xn = _init(x, ctx, norm1_g3, mods3)
    res = None
    for layer in range(NLAYER):
        pm = _inproj(xn, w_int, layer)
        lr = _lrproj(xn, w_int, layer)
        o = _gla(pm, lr, wa_cat, ba_cat, layer)
        yf = _fourier(pm, cs_lat, cs_ctx, cc)
        hmid, xm, logits = _merge(layer, yf, pm, o, h, mods3, norm2_g3, gng, ws, bsb, lng,
                                  pf, pg, plw, wo, wrc, br)
        meta, wts, cnt8 = _route(logits)
        plan = _plan(cnt8)
        xs = _dispatch(plan, xm, meta)
        ys = _experts(plan, xs, expert_w_gate, expert_w_up, expert_w_down, layer)
        res = _combine(layer, plan, ys, meta, wts, hmid, mods3, norm1_g3)
        if layer + 1 < NLAYER:
            h, xn = res
    return res
```

```python
import functools
import math

import numpy as np
import jax
import jax.numpy as jnp
from jax import lax
from jax.experimental import pallas as pl
from jax.experimental.pallas import tpu as pltpu

F32 = jnp.float32
BF16 = jnp.bfloat16
I32 = jnp.int32

D = 2048
NB = 4
SEQ = 2048
NLAYER = 4
CTX = 256
EPS = 1e-6
LB = CTX + SEQ
P = NB * LB
TM = 256
TPB = LB // TM
NT = P // TM

FG, FGD = 4, 128
FDIM = FG * FGD
GH, GHD, GCH = 4, 128, 128
GDIM = GH * GHD
LH, LDK, LDV, LRANK, LTAU, LC = 4, 128, 256, 16, 16.0, 64
LKD, LVD = LH * LDK, LH * LDV
NCH = LB // LC
NCTXCH = CTX // LC

W_ALIGNED = 4608
W_GATE0 = W_ALIGNED + 2 * 16
TN = 1536
NBLK_AL, NBLK_GT = W_ALIGNED // TN, 3 * D // TN
C_G0, C_A, C_ZU, C_ZV, C_Q, C_K, C_V, C_R = 0, 6144, 6656, 7168, 7680, 8192, 8704, 9728
NMAIN = 10752
LRW = 128

NGRP, EPG, NEXP, DEXP = 4, 8, 32, 512
TR = 256
CH = 8
LCH = (2 * TM + NEXP * (CH - 1)) // CH
LROWS = 768
TCH = TR // CH
NCHUNK = 2 * P // CH + NT * NEXP * (CH - 1) // CH + NEXP * (TCH - 1)
NTILE = NCHUNK // TCH + 1
NSLOT = NTILE * TR
RW = 128

VMEM_LIMIT = 56 * 1024 * 1024


def _cp(sem, vmem=VMEM_LIMIT):
    return pltpu.CompilerParams(dimension_semantics=sem, vmem_limit_bytes=vmem)


def _dot(a, b):
    return jnp.dot(a, b, preferred_element_type=F32)


def _dot_t(a, b):
    return lax.dot_general(a, b, (((1,), (1,)), ((), ())), preferred_element_type=F32)


def _dot_lt(a, b):
    return lax.dot_general(a, b, (((0,), (0,)), ((), ())), preferred_element_type=F32)


def _split(x):
    hi = x.astype(BF16)
    lo = (x - hi.astype(F32)).astype(BF16)
    return hi, lo


U32 = jnp.uint32
DH = D // 2


def _pack_bf16_pair(x):
    lo = lax.bitcast_convert_type(x[:, :DH], U32)
    hi = lax.bitcast_convert_type(x[:, DH:], U32)
    return lax.shift_right_logical(lo, jnp.uint32(16)) | (hi & jnp.uint32(0xFFFF0000))


def _unpack_bf16_pair(u):
    lo = lax.bitcast_convert_type(lax.shift_left(u, jnp.uint32(16)), F32)
    hi = lax.bitcast_convert_type(u & jnp.uint32(0xFFFF0000), F32)
    return lo.astype(BF16), hi.astype(BF16)


def _sigmoid(x):
    return 1.0 / (1.0 + jnp.exp(-x))


def _silu(x):
    return x * _sigmoid(x)


def _gelu_tanh(x):
    return 0.5 * x * (1.0 + jnp.tanh(math.sqrt(2.0 / math.pi) * (x + 0.044715 * (x * x * x))))


def _rms(x, g):
    return x * lax.rsqrt(jnp.mean(x * x, axis=-1, keepdims=True) + EPS) * g


def _tile_row(t):
    return jnp.where(t % TPB == 0, 4, t // TPB)


def _mod_spec(layer, comp, row_fn):
    return pl.BlockSpec((1, 1, D), lambda *g: ((layer * 8 + row_fn(*g)) * 6 + comp, 0, 0))


def _mods_kernel(c_ref, w_ref, b_ref, o_ref):
    c = c_ref[...]
    s = _silu(c).astype(BF16)
    o_ref[0] = _dot(s, w_ref[0].astype(BF16)) + b_ref[0]


def _mods(cvec, w_mod, b_mod):
    tn = 1024
    return pl.pallas_call(
        _mods_kernel,
        grid=(NLAYER, 6 * D // tn),
        in_specs=[pl.BlockSpec((8, D), lambda l, j: (0, 0)),
                  pl.BlockSpec((1, D, tn), lambda l, j: (l, 0, j)),
                  pl.BlockSpec((1, 1, tn), lambda l, j: (l, 0, j))],
        out_specs=pl.BlockSpec((1, 8, tn), lambda l, j: (l, 0, j)),
        out_shape=jax.ShapeDtypeStruct((NLAYER, 8, 6 * D), F32),
        compiler_params=_cp(("arbitrary", "arbitrary")),
        name="mods",
    )(cvec, w_mod, b_mod.reshape(NLAYER, 1, 6 * D))


def _init_kernel(x_ref, c_ref, g_ref, sh_ref, sc_ref, h_ref, xn_ref):
    j = pl.program_id(1)

    def emit(v):
        h_ref[...] = v
        xn_ref[...] = (_rms(v, g_ref[0]) * (1.0 + sc_ref[0]) + sh_ref[0]).astype(BF16)

    @pl.when(j == 0)
    def _():
        emit(c_ref[0])

    @pl.when(j > 0)
    def _():
        emit(x_ref[0])


def _init(x, ctx, norm1_g3, mods3):
    row = lambda b, j: jnp.where(j == 0, 4, b)
    return pl.pallas_call(
        _init_kernel,
        grid=(NB, TPB),
        in_specs=[pl.BlockSpec((1, TM, D), lambda b, j: (b, jnp.maximum(j - 1, 0), 0)),
                  pl.BlockSpec((1, CTX, D), lambda b, j: (b, 0, 0)),
                  pl.BlockSpec((1, 1, D), lambda b, j: (0, 0, 0)),
                  _mod_spec(0, 0, row), _mod_spec(0, 1, row)],
        out_specs=[pl.BlockSpec((TM, D), lambda b, j: (b * TPB + j, 0)),
                   pl.BlockSpec((TM, D), lambda b, j: (b * TPB + j, 0))],
        out_shape=[jax.ShapeDtypeStruct((P, D), F32), jax.ShapeDtypeStruct((P, D), BF16)],
        compiler_params=_cp(("arbitrary", "arbitrary")),
        name="init",
    )(x, ctx, norm1_g3, mods3, mods3)


TMP = 1024


GOFF = W_GATE0 - W_ALIGNED


def _inproj_kernel(x_ref, w_ref, wt_ref, o_ref, wb_ref):
    j = pl.program_id(0)

    @pl.when((pl.program_id(1) == 0) & (j < NBLK_AL))
    def _():
        wb_ref[...] = w_ref[0].astype(BF16)

    @pl.when((pl.program_id(1) == 0) & (j >= NBLK_AL))
    def _():
        wide = jnp.concatenate([w_ref[0], wt_ref[0]], axis=0)
        wb_ref[...] = wide[GOFF:GOFF + TN, :].astype(BF16)

    o_ref[...] = _dot_t(x_ref[...], wb_ref[...]).astype(BF16)


def _inproj(xn, w_int, layer):
    nblk = NBLK_AL + NBLK_GT
    tail = lambda j, i: (layer, jnp.where(j >= NBLK_AL, (j + 1) * (TN // GOFF), (NBLK_AL + 1) * (TN // GOFF)), 0)
    return pl.pallas_call(
        _inproj_kernel,
        grid=(nblk, P // TMP),
        in_specs=[pl.BlockSpec((TMP, D), lambda j, i: (i, 0)),
                  pl.BlockSpec((1, TN, D), lambda j, i: (layer, j, 0)),
                  pl.BlockSpec((1, GOFF, D), tail)],
        out_specs=pl.BlockSpec((TMP, TN), lambda j, i: (i, (j + NBLK_GT) % nblk)),
        out_shape=jax.ShapeDtypeStruct((P, NMAIN), BF16),
        scratch_shapes=[pltpu.VMEM((TN, D), BF16)],
        compiler_params=_cp(("arbitrary", "arbitrary")),
        name="inproj",
    )(xn, w_int, w_int)


def _lrproj_kernel(x_ref, w_ref, o_ref):
    o_ref[...] = _dot_t(x_ref[...], w_ref[0].astype(BF16))


def _lrproj(xn, w_int, layer):
    tm = 1024
    return pl.pallas_call(
        _lrproj_kernel,
        grid=(P // tm,),
        in_specs=[pl.BlockSpec((tm, D), lambda i: (i, 0)),
                  pl.BlockSpec((1, LRW, D), lambda i: (layer, W_ALIGNED // LRW, 0))],
        out_specs=pl.BlockSpec((tm, LRW), lambda i: (i, 0)),
        out_shape=jax.ShapeDtypeStruct((P, LRW), F32),
        compiler_params=_cp(("arbitrary",)),
        name="lrproj",
    )(xn, w_int)


def _dft_consts():
    def cs(n):
        k = np.arange(n, dtype=np.int64)
        ang = 2.0 * np.pi * ((k[:, None] * k[None, :]) % n).astype(np.float64) / n
        return np.cos(ang) / math.sqrt(n), np.sin(ang) / math.sqrt(n)

    c_l, s_l = cs(SEQ)
    c_c, s_c = cs(CTX)
    c_g, s_g = cs(FGD)
    cs_lat = np.concatenate([c_l, -s_l], axis=1).astype(BF16)
    cs_ctx = np.concatenate([c_c, -s_c], axis=1).astype(BF16)
    cc = np.concatenate([c_g, s_g], axis=1).astype(BF16)
    return cs_lat, cs_ctx, cc


def _fourier_kernel(a_ref, csl_ref, csc_ref, cc_ref, o_ref, rl_ref, rc_ref):
    j = pl.program_id(1)

    @pl.when(j == 0)
    def _():
        for g in range(FG):
            cols = slice(g * FGD, (g + 1) * FGD)
            t = _dot(a_ref[:, cols], cc_ref[...]).astype(BF16)
            rc_ref[0:CTX, cols] = t[0:CTX, 0:FGD]
            rc_ref[CTX:2 * CTX, cols] = t[0:CTX, FGD:2 * FGD]
            rl_ref[0:SEQ, cols] = t[CTX:LB, 0:FGD]
            rl_ref[SEQ:2 * SEQ, cols] = t[CTX:LB, FGD:2 * FGD]
        o_ref[...] = _dot(csc_ref[...], rc_ref[...]).astype(BF16)

    @pl.when(j > 0)
    def _():
        o_ref[...] = _dot(csl_ref[...], rl_ref[...]).astype(BF16)


def _fourier(pm, cs_lat, cs_ctx, cc):
    return pl.pallas_call(
        _fourier_kernel,
        grid=(NB, TPB),
        in_specs=[pl.BlockSpec((LB, FDIM), lambda b, j: (b, C_A // FDIM)),
                  pl.BlockSpec((TM, 2 * SEQ), lambda b, j: (jnp.maximum(j - 1, 0), 0)),
                  pl.BlockSpec((CTX, 2 * CTX), lambda b, j: (0, 0)),
                  pl.BlockSpec((FGD, 2 * FGD), lambda b, j: (0, 0))],
        out_specs=pl.BlockSpec((TM, FDIM), lambda b, j: (b * TPB + j, 0)),
        out_shape=jax.ShapeDtypeStruct((P, FDIM), BF16),
        scratch_shapes=[pltpu.VMEM((2 * SEQ, FDIM), BF16), pltpu.VMEM((2 * CTX, FDIM), BF16)],
        compiler_params=_cp(("arbitrary", "arbitrary")),
        name="fourier",
    )(pm, cs_lat, cs_ctx, cc)


SB = 256
SBC = SB // LC
NSB = LB // SB


def _gla_kernel(q_ref, k_ref, v_ref, lr_ref, wa_ref, ba_ref, o_ref,
                qd_ref, oacc_ref, ds_ref, gam_ref, sall_ref, sf_ref, sb_ref):
    ri = lax.broadcasted_iota(I32, (SB, SB), 0)
    ci = lax.broadcasted_iota(I32, (SB, SB), 1)
    same = (ri // LC) == (ci // LC)
    tri = (same & (ci <= ri)).astype(BF16)
    keep_f = same & (ci <= ri)
    keep_b = same & (ci > ri)
    rchunk = lax.broadcasted_iota(I32, (SB, LDK), 0) // LC
    scale = LDK ** -0.5
    wa = wa_ref[...]
    ba = ba_ref[...]

    def phase1(sb, carry):
        rows = pl.ds(pl.multiple_of(sb * SB, SB), SB)
        logits = _dot(lr_ref[rows, :].astype(BF16), wa) + ba
        g = (jnp.minimum(logits, 0.0) - jnp.log1p(jnp.exp(-jnp.abs(logits)))) * (1.0 / LTAU)
        g_hi, g_lo = _split(g)
        pre = _dot(tri, g_hi) + _dot(tri, g_lo)
        tot = jnp.concatenate(
            [jnp.broadcast_to(pre[c * LC + LC - 1:c * LC + LC, :], (LC, 2 * LDK)) for c in range(SBC)], axis=0)
        q = q_ref[rows, :].astype(F32) * scale
        k = k_ref[rows, :].astype(F32)
        v = v_ref[rows, :]
        s_sum = None
        kts = []
        for d in range(2):
            cols = slice(d * LDK, (d + 1) * LDK)
            t_d = tot[:, cols]
            b_d = pre[:, cols] if d == 0 else t_d - pre[:, cols] + g[:, cols]
            q_dec = (q * jnp.exp(b_d)).astype(BF16)
            k_inv = (k * jnp.exp(-b_d)).astype(BF16)
            k_tail = (k * jnp.exp(t_d - b_d)).astype(BF16)
            sc = jnp.where(keep_f if d == 0 else keep_b, _dot_t(q_dec, k_inv), 0.0)
            s_sum = sc if s_sum is None else s_sum + sc
            qd_ref[rows, cols] = q_dec
            zero = jnp.zeros_like(k_tail)
            kts += [jnp.where(rchunk == c, k_tail, zero) for c in range(SBC)]
            for c in range(SBC):
                gam_ref[d, sb * SBC + c] = jnp.exp(t_d[c * LC:c * LC + 1, :])
        oacc_ref[rows, :] = _dot(s_sum.astype(BF16), v)
        dst = _dot_lt(v, jnp.concatenate(kts, axis=1))
        for d in range(2):
            for c in range(SBC):
                j = d * SBC + c
                ds_ref[d, sb * SBC + c] = dst[:, j * LDK:(j + 1) * LDK]
        return carry

    lax.fori_loop(0, NSB, phase1, 0, unroll=True)

    sf_ref[...] = jnp.zeros_like(sf_ref)
    sb_ref[...] = jnp.zeros_like(sb_ref)

    def phase2(i, carry):
        nb = jnp.where(i < NCTXCH, NCTXCH - 1 - i, NCH + NCTXCH - 1 - i)
        s_f = sf_ref[...]
        s_b = sb_ref[...]
        sall_ref[i, :, 0:LDK] = s_f.astype(BF16)
        sall_ref[nb, :, LDK:2 * LDK] = s_b.astype(BF16)
        sf_ref[...] = s_f * gam_ref[0, i] + ds_ref[0, i]
        sb_ref[...] = s_b * gam_ref[1, nb] + ds_ref[1, nb]
        return carry

    lax.fori_loop(0, NCH, phase2, 0)

    def phase3(n, carry):
        rows = pl.ds(pl.multiple_of(n * LC, LC), LC)
        o_ref[rows, :] = (oacc_ref[rows, :] + _dot_t(qd_ref[rows, :], sall_ref[n])).astype(BF16)
        return carry

    lax.fori_loop(0, NCH, phase3, 0, unroll=4)


def _gla(pm, lr, wa_cat, ba_cat, layer):
    return pl.pallas_call(
        _gla_kernel,
        grid=(NB, LH),
        in_specs=[pl.BlockSpec((LB, LDK), lambda b, h: (b, C_Q // LDK + h)),
                  pl.BlockSpec((LB, LDK), lambda b, h: (b, C_K // LDK + h)),
                  pl.BlockSpec((LB, LDV), lambda b, h: (b, C_V // LDV + h)),
                  pl.BlockSpec((LB, LRW), lambda b, h: (b, 0)),
                  pl.BlockSpec((None, None, LRW, 2 * LDK), lambda b, h: (layer, h, 0, 0)),
                  pl.BlockSpec((None, None, 1, 2 * LDK), lambda b, h: (layer, h, 0, 0))],
        out_specs=pl.BlockSpec((LB, LDV), lambda b, h: (b, h)),
        out_shape=jax.ShapeDtypeStruct((P, LVD), BF16),
        scratch_shapes=[pltpu.VMEM((LB, 2 * LDK), BF16), pltpu.VMEM((LB, LDV), F32),
                        pltpu.VMEM((2, NCH, LDV, LDK), F32), pltpu.VMEM((2, NCH, 1, LDK), F32),
                        pltpu.VMEM((NCH, LDV, 2 * LDK), BF16),
                        pltpu.VMEM((LDV, LDK), F32), pltpu.VMEM((LDV, LDK), F32)],
        compiler_params=_cp(("arbitrary", "arbitrary")),
        name="gla",
    )(pm, pm, pm, lr, wa_cat, ba_cat)


def _merge_kernel(yf_ref, zu_ref, zv_ref, o_ref, ra_ref, rb_ref, g0_ref, g1_ref, g2_ref, h_ref,
                  gt1_ref, sh2_ref, sc2_ref, n2g_ref, gng_ref, ws_ref, bs_ref, lng_ref,
                  pf_ref, pg_ref, pl_ref, wo_ref, wrc_ref, br_ref,
                  hmid_ref, xm_ref, lg_ref):
    y = _sigmoid(g0_ref[...].astype(F32)) * _dot(yf_ref[...], pf_ref[...])

    u = _gelu_tanh(zu_ref[...].astype(F32))
    v = _rms(_gelu_tanh(zv_ref[...].astype(F32)), gng_ref[...]).astype(BF16)
    chunks = []
    for ch in range(TM // GCH):
        rows = slice(ch * GCH, (ch + 1) * GCH)
        heads = [_dot(ws_ref[hd], v[rows, hd * GHD:(hd + 1) * GHD]) + bs_ref[hd] for hd in range(GH)]
        chunks.append(jnp.concatenate(heads, axis=1))
    s = jnp.concatenate(chunks, axis=0)
    y += _sigmoid(g1_ref[...].astype(F32)) * _dot((u * s).astype(BF16), pg_ref[...])

    o = o_ref[...].astype(F32)
    lng = lng_ref[...]
    heads = [_rms(o[:, hd * LDV:(hd + 1) * LDV], lng[:, hd * LDV:(hd + 1) * LDV]) for hd in range(LH)]
    r = jnp.concatenate([ra_ref[...], rb_ref[...]], axis=1).astype(F32)
    ol = (jnp.concatenate(heads, axis=1) * _silu(r)).astype(BF16)
    y += _sigmoid(g2_ref[...].astype(F32)) * _dot(ol, pl_ref[...])

    hmid = h_ref[...] + gt1_ref[0] * _dot(y.astype(BF16), wo_ref[...])
    hmid_ref[...] = hmid

    xm = _rms(hmid, n2g_ref[0]) * (1.0 + sc2_ref[0]) + sh2_ref[0]
    xm_hi, xm_lo = _split(xm)
    xm_ref[...] = xm_hi
    hh_hl = _dot(xm_hi, wrc_ref[...])
    lg_ref[...] = hh_hl[:, 0:RW] + hh_hl[:, RW:2 * RW] + _dot(xm_lo, wrc_ref[:, 0:RW]) + br_ref[...]


def _merge(layer, yf, pm, o, h, mods3, norm2_g3, gng, ws, bsb, lng, pf, pg, plw, wo, wrc, br):
    row = _tile_row
    tile = lambda w, c: pl.BlockSpec((TM, w), lambda t: (t, c))
    lay3 = lambda a, b: pl.BlockSpec((None, a, b), lambda t: (layer, 0, 0), pipeline_mode=pl.Buffered(1))
    lay4 = lambda a, b, c: pl.BlockSpec((None, a, b, c), lambda t: (layer, 0, 0, 0))
    return pl.pallas_call(
        _merge_kernel,
        grid=(NT,),
        in_specs=[tile(FDIM, 0), tile(GDIM, C_ZU // GDIM), tile(GDIM, C_ZV // GDIM), tile(LVD, 0),
                  tile(LVD // 2, C_R // (LVD // 2)), tile(LVD // 2, C_R // (LVD // 2) + 1),
                  tile(D, 0), tile(D, 1), tile(D, 2), tile(D, 0),
                  _mod_spec(layer, 2, row), _mod_spec(layer, 3, row), _mod_spec(layer, 4, row),
                  pl.BlockSpec((1, 1, D), lambda t: (layer, 0, 0)),
                  lay3(1, GDIM), lay4(GH, GCH, GCH), lay4(GH, GCH, GCH), lay3(1, LVD),
                  lay3(FDIM, D), lay3(GDIM, D), lay3(LVD, D), lay3(D, D),
                  lay3(D, 2 * RW), lay3(1, RW)],
        out_specs=[tile(D, 0), tile(D, 0), tile(RW, 0)],
        out_shape=[jax.ShapeDtypeStruct((P, D), F32), jax.ShapeDtypeStruct((P, D), BF16),
                   jax.ShapeDtypeStruct((P, RW), F32)],
        compiler_params=_cp(("arbitrary",)),
        name="merge",
    )(yf, pm, pm, o, pm, pm, pm, pm, pm, h, mods3, mods3, mods3, norm2_g3, gng, ws, bsb, lng,
      pf, pg, plw, wo, wrc, br)


def _route_kernel(lg_ref, meta_ref, wts_ref, cnt_ref):
    lg = lg_ref[...]
    lane = lax.broadcasted_iota(I32, (TM, RW), 1)
    lane_f = lane.astype(F32)
    ninf = jnp.float32(-jnp.inf)

    def first_max(x):
        m = jnp.max(x, axis=-1, keepdims=True)
        first = jnp.min(jnp.where(x == m, lane_f, float(RW)), axis=-1, keepdims=True)
        return m, first.astype(I32)

    is_g = (lane >= NEXP) & (lane < NEXP + NGRP)
    gmax, glane = first_max(jnp.where(is_g, lg, ninf))
    gsum = jnp.sum(jnp.where(is_g, jnp.exp(lg - gmax), 0.0), axis=-1, keepdims=True)
    g_w = 1.0 / gsum
    lo = (glane - NEXP) * EPG
    in_grp = (lane >= lo) & (lane < lo + EPG)
    el = jnp.where(in_grp, lg, ninf)
    v1, l1 = first_max(el)
    v2, l2 = first_max(jnp.where(lane == l1, ninf, el))
    e = jnp.exp(v2 - v1)
    w1 = g_w / (1.0 + e)
    w2 = g_w * e / (1.0 + e)

    hit1 = lane == l1
    hit2 = lane == l2
    m = (hit1 | hit2).astype(BF16)
    ri = lax.broadcasted_iota(I32, (TM, TM), 0)
    ci = lax.broadcasted_iota(I32, (TM, TM), 1)
    before = _dot((ci < ri).astype(BF16), m)
    r1 = jnp.sum(jnp.where(hit1, before, 0.0), axis=-1, keepdims=True).astype(I32)
    r2 = jnp.sum(jnp.where(hit2, before, 0.0), axis=-1, keepdims=True).astype(I32)
    total = jnp.sum(m.astype(F32), axis=0, keepdims=True)
    cnt_ref[...] = jnp.broadcast_to(total, cnt_ref.shape).astype(I32)
    meta_ref[...] = jnp.where(lane == 0, l1, jnp.where(lane == 1, l2, jnp.where(lane == 2, r1,
                              jnp.where(lane == 3, r2, 0))))
    wts_ref[...] = jnp.where(lane == 0, w1, jnp.where(lane == 1, w2, 0.0))


def _route(logits):
    return pl.pallas_call(
        _route_kernel,
        grid=(NT,),
        in_specs=[pl.BlockSpec((TM, RW), lambda t: (t, 0))],
        out_specs=[pl.BlockSpec((TM, RW), lambda t: (t, 0)), pl.BlockSpec((TM, RW), lambda t: (t, 0)),
                   pl.BlockSpec((8, RW), lambda t: (t, 0))],
        out_shape=[jax.ShapeDtypeStruct((P, RW), I32), jax.ShapeDtypeStruct((P, RW), F32),
                   jax.ShapeDtypeStruct((NT * 8, RW), I32)],
        compiler_params=_cp(("arbitrary",)),
        name="route",
    )(logits)


def _plan(cnt8):
    cnt = cnt8.reshape(NT, 8, RW)[:, 0, :NEXP]
    c8 = (cnt + (CH - 1)) // CH
    lend = jnp.cumsum(c8, axis=1)
    lstart = lend - c8
    nloc = lend[:, -1]
    reg = jnp.sum(c8, axis=0)
    rpad = (reg + (TCH - 1)) // TCH * TCH
    rend = jnp.cumsum(rpad)
    rstart = rend - rpad
    gbase = rstart[None, :] + jnp.cumsum(c8, axis=0) - c8
    j = jnp.arange(LCH, dtype=I32)
    owner = jnp.sum((lend[:, None, :] <= j[None, :, None]).astype(I32), axis=2)
    sel = (owner[:, :, None] == jnp.arange(NEXP, dtype=I32)[None, None, :]).astype(I32)
    dch = jnp.sum(sel * (gbase - lstart)[:, None, :], axis=2) + j[None, :]
    n_act = rend[-1] // TCH
    tid = jnp.minimum(jnp.arange(NTILE, dtype=I32), n_act - 1)
    tile_expert = jnp.minimum(jnp.sum((rend[None, :] <= (tid * TCH)[:, None]).astype(I32), axis=1), NEXP - 1)
    ex = jnp.arange(NEXP, dtype=I32)
    nonempty = (reg > 0).astype(I32)
    seg_of_e = jnp.cumsum(nonempty) - 1
    nseg = jnp.sum(nonempty)
    seg = jnp.sum((tile_expert[:, None] == ex[None, :]).astype(I32) * seg_of_e[None, :], axis=1)
    seg_expert = jnp.sum(((seg_of_e[None, :] == ex[:, None]) & (reg[None, :] > 0)).astype(I32) * ex[None, :], axis=1)
    ahead = lambda k: jnp.where(seg + k < nseg,
                                jnp.sum(((seg + k)[:, None] == ex[None, :]).astype(I32) * seg_expert[None, :], axis=1), -1)
    lrow = jnp.pad((lstart * CH).astype(F32), ((0, 0), (0, RW - NEXP))).reshape(NT, 1, RW)
    return dict(nloc=nloc.astype(I32), dch=dch.astype(I32).reshape(NT, 1, LCH), lrow=lrow,
                pstart=(rstart + reg).astype(I32), npad=(rpad - reg).astype(I32),
                tile_expert=tile_expert.astype(I32), seg=seg.astype(I32),
                next1=ahead(1).astype(I32), next2=ahead(2).astype(I32),
                n_act=n_act.reshape(1).astype(I32))


def _local_pos(meta, lrow):
    lane = lax.broadcasted_iota(I32, (TM, RW), 1)
    meta_f = meta.astype(F32)
    col = lambda k: jnp.sum(jnp.where(lane == k, meta_f, 0.0), axis=-1, keepdims=True).astype(I32)
    l1, l2, r1, r2 = col(0), col(1), col(2), col(3)
    off = lambda l: jnp.sum(jnp.where(lane == l, lrow, 0.0), axis=-1, keepdims=True).astype(I32)
    return off(l1) + r1, off(l2) + r2


def _dispatch_kernel(nloc_ref, pstart_ref, npad_ref, na_ref, dch_ref, x_ref, meta_ref, lrow_ref, xs_hbm,
                     xloc, zbuf, sem, zsem):
    t = pl.program_id(0)
    slot = t % 2

    def chunk_copy(buf_slot, j, g):
        return pltpu.make_async_copy(xloc.at[buf_slot, pl.ds(pl.multiple_of(j * CH, CH), CH)],
                                     xs_hbm.at[pl.ds(pl.multiple_of(g * CH, CH), CH)], sem.at[buf_slot])

    def zero_chunks(first, n, wait):
        zcopy = lambda c: pltpu.make_async_copy(
            zbuf, xs_hbm.at[pl.ds(pl.multiple_of((first + c) * CH, CH), CH)], zsem.at[0])
        if wait:
            lax.fori_loop(0, n, lambda c, z: (zcopy(0).wait(), z)[1], 0)
        else:
            lax.fori_loop(0, n, lambda c, z: (zcopy(c).start(), z)[1], 0)

    def zero_fill(wait):
        for e in range(NEXP):
            zero_chunks(pstart_ref[e], npad_ref[e], wait)
        zero_chunks(na_ref[0] * TCH, NTILE * TCH - na_ref[0] * TCH, wait)

    @pl.when(t == 0)
    def _():
        zbuf[...] = jnp.zeros_like(zbuf)
        zero_fill(False)

    p1, p2 = _local_pos(meta_ref[...], lrow_ref[0])
    pos = lax.broadcasted_iota(I32, (TM, LROWS), 1)
    sel = ((pos == p1) | (pos == p2)).astype(BF16)
    xloc[slot] = _pack_bf16_pair(_dot_lt(sel, x_ref[...]))

    @pl.when(t > 0)
    def _():
        lax.fori_loop(0, nloc_ref[t - 1], lambda j, z: (chunk_copy(1 - slot, 0, 0).wait(), z)[1], 0)

    lax.fori_loop(0, nloc_ref[t], lambda j, z: (chunk_copy(slot, j, dch_ref[0, 0, j]).start(), z)[1], 0)

    @pl.when(t == NT - 1)
    def _():
        lax.fori_loop(0, nloc_ref[t], lambda j, z: (chunk_copy(slot, 0, 0).wait(), z)[1], 0)
        zero_fill(True)


def _dispatch(plan, xm, meta):
    grid_spec = pltpu.PrefetchScalarGridSpec(
        num_scalar_prefetch=4,
        grid=(NT,),
        in_specs=[pl.BlockSpec((1, 1, LCH), lambda t, *_: (t, 0, 0), memory_space=pltpu.SMEM),
                  pl.BlockSpec((TM, D), lambda t, *_: (t, 0)),
                  pl.BlockSpec((TM, RW), lambda t, *_: (t, 0)),
                  pl.BlockSpec((1, 1, RW), lambda t, *_: (t, 0, 0))],
        out_specs=pl.BlockSpec(memory_space=pl.ANY),
        scratch_shapes=[pltpu.VMEM((2, LROWS, DH), U32), pltpu.VMEM((CH, DH), U32),
                        pltpu.SemaphoreType.DMA((2,)), pltpu.SemaphoreType.DMA((1,))],
    )
    return pl.pallas_call(
        _dispatch_kernel,
        grid_spec=grid_spec,
        out_shape=jax.ShapeDtypeStruct((NSLOT, DH), U32),
        compiler_params=_cp(("arbitrary",)),
        name="dispatch",
    )(plan["nloc"], plan["pstart"], plan["npad"], plan["n_act"], plan["dch"], xm, meta, plan["lrow"])


def _experts_kernel(layer, te_ref, seg_ref, n1_ref, n2_ref, na_ref, x_ref, wg_hbm, wu_hbm, wd_hbm, y_ref,
                    wgf, wuf, wdf, wgb, wub, wdb, ybuf, wsem):
    i = pl.program_id(0)

    def weight_copies(e, b):
        return (pltpu.make_async_copy(wg_hbm.at[layer, e], wgf.at[b], wsem.at[b, 0]),
                pltpu.make_async_copy(wu_hbm.at[layer, e], wuf.at[b], wsem.at[b, 1]),
                pltpu.make_async_copy(wd_hbm.at[layer, e], wdf.at[b], wsem.at[b, 2]))

    @pl.when(i < na_ref[0])
    def _():
        e = te_ref[i]
        b = seg_ref[i] % 2
        first = jnp.logical_or(i == 0, e != te_ref[jnp.maximum(i - 1, 0)])

        def start_all(copies):
            for k, cp in enumerate(copies):
                cp.start(priority=k % 2)

        @pl.when(i == 0)
        def _():
            start_all(weight_copies(e, 0))

            @pl.when(n1_ref[0] >= 0)
            def _():
                start_all(weight_copies(n1_ref[0], 1))

        @pl.when(first)
        def _():
            for cp in weight_copies(e, b):
                cp.wait()
            wgb[...] = wgf[b].astype(BF16)
            wub[...] = wuf[b].astype(BF16)
            wdb[...] = wdf[b].astype(BF16)

            @pl.when(n2_ref[i] >= 0)
            def _():
                start_all(weight_copies(n2_ref[i], b))

    def compute():
        x_lo, x_hi = _unpack_bf16_pair(x_ref[...])
        hg = _dot(x_lo, wgb[0:DH, :]) + _dot(x_hi, wgb[DH:D, :])
        hu = _dot(x_lo, wub[0:DH, :]) + _dot(x_hi, wub[DH:D, :])
        act = (_silu(hg) * hu).astype(BF16)
        ybuf[...] = _dot(act, wdb[...])

    def emit():
        y_ref[...] = _pack_bf16_pair(ybuf[...].astype(BF16).astype(F32))

    @pl.when(i == 0)
    def _():
        compute()

    @pl.when((i >= 1) & (i < na_ref[0]))
    def _():
        emit()
        compute()

    @pl.when((i >= 1) & (i == na_ref[0]))
    def _():
        emit()

    @pl.when(i > na_ref[0])
    def _():
        y_ref[...] = jnp.zeros_like(y_ref)


def _experts(plan, xs, wg, wu, wd, layer):
    hbm = pl.BlockSpec(memory_space=pl.ANY)
    grid_spec = pltpu.PrefetchScalarGridSpec(
        num_scalar_prefetch=5,
        grid=(NTILE + 1,),
        in_specs=[pl.BlockSpec((TR, DH), lambda i, te, sg, n1, n2, na: (jnp.minimum(i, na[0] - 1), 0)),
                  hbm, hbm, hbm],
        out_specs=pl.BlockSpec((TR, DH), lambda i, *_: (jnp.maximum(i - 1, 0), 0)),
        scratch_shapes=[pltpu.VMEM((2, D, DEXP), F32), pltpu.VMEM((2, D, DEXP), F32), pltpu.VMEM((2, DEXP, D), F32),
                        pltpu.VMEM((D, DEXP), BF16), pltpu.VMEM((D, DEXP), BF16), pltpu.VMEM((DEXP, D), BF16),
                        pltpu.VMEM((TR, D), F32), pltpu.SemaphoreType.DMA((2, 3))],
    )
    return pl.pallas_call(
        functools.partial(_experts_kernel, layer),
        grid_spec=grid_spec,
        out_shape=jax.ShapeDtypeStruct((NSLOT, DH), U32),
        compiler_params=_cp(("arbitrary",)),
        name="experts",
    )(plan["tile_expert"], plan["seg"], plan["next1"], plan["next2"], plan["n_act"], xs, wg, wu, wd)


def _combine_kernel(last, nloc_ref, dch_ref, dchn_ref, meta_ref, w_ref, lrow_ref, h_ref, gt2_ref, g_ref,
                    sh_ref, sc_ref, ys_hbm, *rest):
    if last:
        out_ref, yloc, sem = rest
    else:
        h_out, xn_out, yloc, sem = rest
    t = pl.program_id(0)
    slot = t % 2

    def fetch(idx_ref, buf_slot, n):
        def body(j, z):
            g = idx_ref[0, 0, j]
            pltpu.make_async_copy(ys_hbm.at[pl.ds(pl.multiple_of(g * CH, CH), CH)],
                                  yloc.at[buf_slot, pl.ds(pl.multiple_of(j * CH, CH), CH)],
                                  sem.at[buf_slot]).start()
            return z
        lax.fori_loop(0, n, body, 0)

    @pl.when(t == 0)
    def _():
        yloc[...] = jnp.zeros_like(yloc)
        fetch(dch_ref, 0, nloc_ref[0])

    @pl.when(t + 1 < NT)
    def _():
        fetch(dchn_ref, 1 - slot, nloc_ref[jnp.minimum(t + 1, NT - 1)])

    def wait_one(j, z):
        pltpu.make_async_copy(ys_hbm.at[pl.ds(0, CH)], yloc.at[slot, pl.ds(0, CH)], sem.at[slot]).wait()
        return z
    lax.fori_loop(0, nloc_ref[t], wait_one, 0)

    p1, p2 = _local_pos(meta_ref[...], lrow_ref[0])
    pos = lax.broadcasted_iota(I32, (TM, LROWS), 1)
    y_lo, y_hi = _unpack_bf16_pair(yloc[slot])
    w = w_ref[...]
    q = (jnp.where(pos == p1, w[:, 0:1], 0.0) + jnp.where(pos == p2, w[:, 1:2], 0.0)).astype(BF16)
    moe = jnp.concatenate([_dot(q, y_lo), _dot(q, y_hi)], axis=1)
    h = h_ref[...] + gt2_ref[0] * moe
    if last:
        @pl.when(t % TPB > 0)
        def _():
            out_ref[0] = _rms(h, g_ref[0])
    else:
        h_out[...] = h
        xn_out[...] = (_rms(h, g_ref[0]) * (1.0 + sc_ref[0]) + sh_ref[0]).astype(BF16)


def _combine(layer, plan, ys, meta, wts, hmid, mods3, norm_g3):
    last = layer == NLAYER - 1
    row = lambda t, *_: _tile_row(t)
    nxt = 0 if last else layer + 1
    tile = lambda w: pl.BlockSpec((TM, w), lambda t, *_: (t, 0))
    if last:
        out_specs = pl.BlockSpec((1, TM, D), lambda t, *_: (t // TPB, jnp.maximum(t % TPB - 1, 0), 0))
        out_shape = jax.ShapeDtypeStruct((NB, SEQ, D), F32)
    else:
        out_specs = [tile(D), tile(D)]
        out_shape = [jax.ShapeDtypeStruct((P, D), F32), jax.ShapeDtypeStruct((P, D), BF16)]
    grid_spec = pltpu.PrefetchScalarGridSpec(
        num_scalar_prefetch=1,
        grid=(NT,),
        in_specs=[pl.BlockSpec((1, 1, LCH), lambda t, *_: (t, 0, 0), memory_space=pltpu.SMEM),
                  pl.BlockSpec((1, 1, LCH), lambda t, *_: (jnp.minimum(t + 1, NT - 1), 0, 0),
                               memory_space=pltpu.SMEM),
                  tile(RW), tile(RW), pl.BlockSpec((1, 1, RW), lambda t, *_: (t, 0, 0)), tile(D),
                  _mod_spec(layer, 5, row),
                  pl.BlockSpec((1, 1, D), lambda t, *_: (NLAYER if last else nxt, 0, 0)),
                  _mod_spec(nxt, 0, row), _mod_spec(nxt, 1, row),
                  pl.BlockSpec(memory_space=pl.ANY)],
        out_specs=out_specs,
        scratch_shapes=[pltpu.VMEM((2, LROWS, DH), U32), pltpu.SemaphoreType.DMA((2,))],
    )
    return pl.pallas_call(
        functools.partial(_combine_kernel, last),
        grid_spec=grid_spec,
        out_shape=out_shape,
        compiler_params=_cp(("arbitrary",)),
        name="final" if last else "combine",
    )(plan["nloc"], plan["dch"], plan["dch"], meta, wts, plan["lrow"], hmid, mods3, norm_g3, mods3, mods3, ys)


def kernel(x, c, ctx, c_ctx, w_mod, b_mod, norm1_g, norm2_g, w_in, p_fourier, gmlp_norm_g, gmlp_ws,
           gmlp_bs, p_gmlp, gla_w_a2, gla_b_a, gla_norm_g, p_gla, w_out, router_group_w, router_group_b,
           router_expert_w, router_expert_b, expert_w_gate, expert_w_up, expert_w_down, final_norm_g):
    cvec = jnp.concatenate([c, c_ctx[None, :], jnp.zeros((3, D), F32)], axis=0)
    w_int = jnp.swapaxes(w_in, 1, 2)
    wa_pad = jnp.stack([jnp.pad(gla_w_a2[:, 0], ((0, 0), (0, LRW - LRANK), (0, 0))),
                        jnp.pad(gla_w_a2[:, 1], ((0, 0), (LRANK, LRW - 2 * LRANK), (0, 0)))], axis=1)
    wa_cat = wa_pad.reshape(NLAYER, 2, LRW, LH, LDK).transpose(0, 3, 2, 1, 4).reshape(NLAYER, LH, LRW, 2 * LDK)
    wa_cat = wa_cat.astype(BF16)
    ba_cat = gla_b_a.reshape(NLAYER, 2, LH, LDK).transpose(0, 2, 1, 3).reshape(NLAYER, LH, 1, 2 * LDK)
    norm1_g3 = jnp.concatenate([norm1_g, final_norm_g[None, :]], axis=0).reshape(NLAYER + 1, 1, D)
    norm2_g3 = norm2_g.reshape(NLAYER, 1, D)
    gng = gmlp_norm_g.reshape(NLAYER, 1, GDIM)
    ws = gmlp_ws.astype(BF16)
    bsb = jnp.broadcast_to(gmlp_bs[:, :, :, None], (NLAYER, GH, GCH, GCH))
    lng = gla_norm_g.reshape(NLAYER, 1, LVD)
    pf, pg, plw, wo = (p_fourier.astype(BF16), p_gmlp.astype(BF16), p_gla.astype(BF16), w_out.astype(BF16))
    w_r = jnp.pad(jnp.concatenate([router_expert_w, router_group_w], axis=-1),
                  ((0, 0), (0, 0), (0, RW - NEXP - NGRP)))
    wrh = w_r.astype(BF16)
    wrc = jnp.concatenate([wrh, (w_r - wrh.astype(F32)).astype(BF16)], axis=-1)
    br =jnp.pad(jnp.concatenate([router_expert_b, router_group_b], axis=-1),
                 ((0, 0), (0, RW - NEXP - NGRP))).reshape(NLAYER, 1, RW)
    cs_lat, cs_ctx, cc = _dft_consts()

    mods3 = _mods(cvec, w_mod, b_mod).reshape(NLAYER * 8 * 6, 1, D)

    h, xn = _init(x, ctx, norm1_g3, mods3)
    res = None
    for layer in range(NLAYER):
        pm = _inproj(xn, w_int, layer)
        lr = _lrproj(xn, w_int, layer)
        o = _gla(pm, lr, wa_cat, ba_cat, layer)
        yf = _fourier(pm, cs_lat, cs_ctx, cc)
        hmid, xm, logits = _merge(layer, yf, pm, o, h, mods3, norm2_g3, gng, ws, bsb, lng,
                                  pf, pg, plw, wo, wrc, br)
        meta, wts, cnt8 = _route(logits)
        plan = _plan(cnt8)
        xs = _dispatch(plan, xm, meta)
        ys = _experts(plan, xs, expert_w_gate, expert_w_up, expert_w_down, layer)
        res = _combine(layer, plan, ys, meta, wts, hmid, mods3, norm1_g3)
        if layer + 1 < NLAYER:
            h, xn = res
    return res
```
